```python
import math
import jax, jax.numpy as jnp
from jax import lax
import numpy as np

D_MODEL = 2048
BATCH = 16
SEQ = 256
DEPTH = 4
DEC_BATCH = 4
DEC_SEQ = 1024
PAST_LEN = 512

GRID_W = 64
N_HEADS_NAT = 8
HEAD_DIM_NAT = 128
N_HEADS_DIFF = 8
DIFF_QK_DIM = 64
DIFF_V_DIM = 2 * DIFF_QK_DIM
WIN_H = 8
WIN_W = 16
QCB = 16
KCB = 32
Q_BLOCK = 128
N_EXPERTS = 16
N_GROUPS = 4
EXPERTS_PER_GROUP = N_EXPERTS // N_GROUPS
TOP_K = 2
D_FF = 1024
ROPE_BASE = 10000.0
EPS = 1e-6
NAT_WIDTH = N_HEADS_NAT * HEAD_DIM_NAT
DIFF_QK_WIDTH = N_HEADS_DIFF * 2 * DIFF_QK_DIM
DIFF_WIDTH = N_HEADS_DIFF * DIFF_V_DIM
MIX_WIDTH = NAT_WIDTH + DIFF_WIDTH
IN_COLS = 3 * NAT_WIDTH + 2 * DIFF_QK_WIDTH + DIFF_WIDTH
SPLITS = [NAT_WIDTH, 2 * NAT_WIDTH, 3 * NAT_WIDTH,
          3 * NAT_WIDTH + DIFF_QK_WIDTH, 3 * NAT_WIDTH + 2 * DIFF_QK_WIDTH]

kernel_name = "hybrid_natten_diffattn_groupmoe_dit_step"


def rmsnorm(x, g):
    xf = x.astype(jnp.float32)
    y = xf * lax.rsqrt(jnp.mean(xf * xf, axis=-1, keepdims=True) + EPS)
    return (y * g.astype(jnp.float32)).astype(x.dtype)


def modulation(cond, w, b):
    m = jax.nn.silu(cond) @ w + b
    return jnp.split(m[..., None, :], 6, axis=-1)


def axial_rope(x):
    n = x.shape[2]
    t = jnp.arange(n)
    row = (t // GRID_W).astype(jnp.float32)
    col = (t % GRID_W).astype(jnp.float32)
    half = x.shape[-1] // 2
    nf = half // 2
    inv = ROPE_BASE ** (-jnp.arange(nf, dtype=jnp.float32) / nf)

    def rot(xa, pos):
        ang = pos[:, None] * inv[None, :]
        cos = jnp.cos(ang).astype(x.dtype)
        sin = jnp.sin(ang).astype(x.dtype)
        x1, x2 = xa[..., :nf], xa[..., nf:]
        return jnp.concatenate([x1 * cos - x2 * sin, x1 * sin + x2 * cos], axis=-1)

    return jnp.concatenate([rot(x[..., :half], row), rot(x[..., half:], col)], axis=-1)


def map_query_blocks(fn, qs):
    B, H, S = qs[0].shape[:3]
    nb = S // Q_BLOCK
    blocks = tuple(q.reshape(B, H, nb, Q_BLOCK, q.shape[-1]).transpose(2, 0, 1, 3, 4) for q in qs)
    out = lax.map(lambda qb: fn(*qb), blocks)
    return out.transpose(1, 2, 0, 3, 4).reshape(B, H, S, out.shape[-1])


def softmax_attn(q, k, v):
    scale = q.shape[-1] ** -0.5

    def blk(qb):
        s = jnp.einsum('bhqd,bhkd->bhqk', qb, k).astype(jnp.float32) * scale
        p = jax.nn.softmax(s, axis=-1).astype(v.dtype)
        return jnp.einsum('bhqk,bhkd->bhqd', p, v)

    return map_query_blocks(blk, (q,))


def diff_attention(q1, q2, k1, k2, v, lam):
    scale = q1.shape[-1] ** -0.5

    def blk(q1b, q2b):
        s1 = jnp.einsum('bhqd,bhkd->bhqk', q1b, k1).astype(jnp.float32) * scale
        s2 = jnp.einsum('bhqd,bhkd->bhqk', q2b, k2).astype(jnp.float32) * scale
        p = jax.nn.softmax(s1, axis=-1) - lam * jax.nn.softmax(s2, axis=-1)
        return jnp.einsum('bhqk,bhkd->bhqd', p.astype(v.dtype), v)

    return map_query_blocks(blk, (q1, q2))


def natten_latent(q, k, v, kc, vc, rpb):
    B, H, N, d = q.shape
    rows = N // GRID_W
    kh = min(WIN_H, rows)
    nb = GRID_W // QCB
    scale = d ** -0.5
    r = jnp.arange(rows)
    row_idx = jnp.clip(r - kh // 2, 0, rows - kh)[:, None] + jnp.arange(kh)[None, :]
    col_idx = jnp.clip(jnp.arange(nb) * QCB - WIN_W // 2, 0, GRID_W - KCB)[:, None] + jnp.arange(KCB)[None, :]
    kg = k.reshape(B, H, rows, GRID_W, d)
    vg = v.reshape(B, H, rows, GRID_W, d)
    gi_r, gi_c = row_idx[:, None, :, None], col_idx[None, :, None, :]
    kn = kh * KCB
    k_nb = kg[:, :, gi_r, gi_c].reshape(B, H, rows, nb, kn, d)
    v_nb = vg[:, :, gi_r, gi_c].reshape(B, H, rows, nb, kn, d)
    qg = q.reshape(B, H, rows, nb, QCB, d)
    cq = jnp.arange(GRID_W).reshape(nb, QCB)
    qstart = jnp.clip(cq - WIN_W // 2, 0, GRID_W - WIN_W)[..., None]
    kcol = col_idx[:, None, :]
    valid = (kcol >= qstart) & (kcol < qstart + WIN_W)
    mask = jnp.broadcast_to(valid[:, :, None, :], (nb, QCB, kh, KCB)).reshape(nb, QCB, kn)
    dc = jnp.clip(kcol - cq[..., None], -(WIN_W - 1), WIN_W - 1) + WIN_W - 1
    dr = row_idx - r[:, None] + WIN_H - 1
    bias = rpb[:, dr[:, None, None, :, None], dc[None, :, :, None, :]]
    bias = bias.reshape(H, rows, nb, QCB, kn).astype(jnp.float32)
    s_nb = jnp.einsum('bhrnqd,bhrnkd->bhrnqk', qg, k_nb).astype(jnp.float32) * scale + bias[None]
    s_nb = jnp.where(mask, s_nb, -1e30)
    s_ctx = jnp.einsum('bhrnqd,bhpd->bhrnqp', qg, kc).astype(jnp.float32) * scale
    p = jax.nn.softmax(jnp.concatenate([s_nb, s_ctx], axis=-1), axis=-1).astype(v.dtype)
    out = (jnp.einsum('bhrnqk,bhrnkd->bhrnqd', p[..., :kn], v_nb)
           + jnp.einsum('bhrnqp,bhpd->bhrnqd', p[..., kn:], vc))
    return out.reshape(B, H, N, d)


def attn_sublayer(h, lw, layer_idx, ctx):
    B, N, _ = h.shape
    nq, nk, nv, dq, dk, dv = jnp.split(h @ lw['w_in'], SPLITS, axis=-1)
    heads = lambda t, nh: t.reshape(B, N, nh, -1).transpose(0, 2, 1, 3)
    nq, nk, nv = heads(nq, N_HEADS_NAT), heads(nk, N_HEADS_NAT), heads(nv, N_HEADS_NAT)
    dq, dk, dv = heads(dq, N_HEADS_DIFF), heads(dk, N_HEADS_DIFF), heads(dv, N_HEADS_DIFF)
    lam_init = 0.8 - 0.6 * math.exp(-0.3 * layer_idx)
    lp = lw['diff_lambda'].astype(jnp.float32)
    lam = jnp.exp(jnp.sum(lp[0] * lp[1])) - jnp.exp(jnp.sum(lp[2] * lp[3])) + lam_init
    if ctx is None:
        nat = softmax_attn(nq, nk, nv)
        q1, q2 = dq[..., :DIFF_QK_DIM], dq[..., DIFF_QK_DIM:]
        k1, k2 = dk[..., :DIFF_QK_DIM], dk[..., DIFF_QK_DIM:]
        vv = dv
        new_ctx = (nk, nv, dk, dv)
    else:
        ck_nat, cv_nat, ck_diff, cv_diff = ctx
        nat = natten_latent(nq, nk, nv, ck_nat, cv_nat, lw['nat_rpb'])
        q1 = axial_rope(dq[..., :DIFF_QK_DIM])
        q2 = axial_rope(dq[..., DIFF_QK_DIM:])
        k1 = jnp.concatenate([axial_rope(dk[..., :DIFF_QK_DIM]), ck_diff[..., :DIFF_QK_DIM]], axis=2)
        k2 = jnp.concatenate([axial_rope(dk[..., DIFF_QK_DIM:]), ck_diff[..., DIFF_QK_DIM:]], axis=2)
        vv = jnp.concatenate([dv, cv_diff], axis=2)
        new_ctx = None
    diff = diff_attention(q1, q2, k1, k2, vv, lam)
    diff = rmsnorm(diff, lw['diff_subln']) * (1.0 - lam_init)
    merged = jnp.concatenate([nat, diff], axis=1).transpose(0, 2, 1, 3).reshape(B, N, MIX_WIDTH)
    return merged @ lw['w_out'], new_ctx


def moe(h, w_router, router_bias, w_gate, w_up, w_down):
    B, N, D = h.shape
    t = h.reshape(B * N, D)
    probs = jax.nn.softmax((t @ w_router).astype(jnp.float32), axis=-1)
    sel = probs + router_bias.astype(jnp.float32)
    gscore = lax.top_k(sel.reshape(-1, N_GROUPS, EXPERTS_PER_GROUP), 2)[0].sum(-1)
    g = jnp.argmax(gscore, axis=-1)
    expert_group = jnp.arange(N_EXPERTS) // EXPERTS_PER_GROUP
    masked = jnp.where(expert_group[None, :] == g[:, None], sel, -jnp.inf)
    _, idx = lax.top_k(masked, TOP_K)
    w = jnp.take_along_axis(probs, idx, axis=-1)
    w = w / jnp.sum(w, axis=-1, keepdims=True)
    combine = jnp.einsum('tk,tke->te', w, jax.nn.one_hot(idx, N_EXPERTS, dtype=jnp.float32)).astype(h.dtype)
    hid = jax.nn.silu(jnp.einsum('td,edf->tef', t, w_gate)) * jnp.einsum('td,edf->tef', t, w_up)
    out = jnp.einsum('tef,efd->td', hid * combine[..., None], w_down)
    return out.reshape(B, N, D)


def block(x, mod, lw, layer_idx, ctx, w_router, router_bias):
    sh1, sc1, g1, sh2, sc2, g2 = mod
    h = rmsnorm(x, lw['norm1']) * (1 + sc1) + sh1
    a, new_ctx = attn_sublayer(h, lw, layer_idx, ctx)
    x = x + g1 * a
    h = rmsnorm(x, lw['norm2']) * (1 + sc2) + sh2
    x = x + g2 * moe(h, w_router, router_bias, lw['w_gate'], lw['w_up'], lw['w_down'])
    return x, new_ctx


def setup_inputs(seed: int = 0) -> dict:
    key = jax.random.key(seed)
    ks = jax.random.split(key, 24)
    f32 = jnp.float32
    nrm = lambda k, shape, s: jax.random.normal(k, shape, f32) * s
    D = D_MODEL
    return {
        'x_prompt': nrm(ks[0], (BATCH, SEQ, D), 1.0),
        'x_sample': nrm(ks[1], (DEC_BATCH, DEC_SEQ, D), 1.0),
        'cache_nat_k': nrm(ks[2], (DEC_BATCH, DEPTH, N_HEADS_NAT, PAST_LEN, HEAD_DIM_NAT), 1.0),
        'cache_nat_v': nrm(ks[3], (DEC_BATCH, DEPTH, N_HEADS_NAT, PAST_LEN, HEAD_DIM_NAT), 1.0),
        'cache_diff_k': nrm(ks[4], (DEC_BATCH, DEPTH, N_HEADS_DIFF, PAST_LEN, 2 * DIFF_QK_DIM), 1.0),
        'cache_diff_v': nrm(ks[5], (DEC_BATCH, DEPTH, N_HEADS_DIFF, PAST_LEN, DIFF_V_DIM), 1.0),
        'c': nrm(ks[6], (DEC_BATCH, D), 1.0),
        'c_ctx': nrm(ks[7], (D,), 1.0),
        'w_ada': nrm(ks[8], (DEPTH, D, 6 * D), 0.5 * D ** -0.5),
        'b_ada': nrm(ks[9], (DEPTH, 6 * D), 0.02),
        'norm1': 1.0 + nrm(ks[10], (DEPTH, D), 0.05),
        'norm2': 1.0 + nrm(ks[11], (DEPTH, D), 0.05),
        'norm_final': 1.0 + nrm(ks[12], (D,), 0.05),
        'w_in': nrm(ks[13], (DEPTH, D, IN_COLS), D ** -0.5),
        'w_out': nrm(ks[14], (DEPTH, MIX_WIDTH, D), MIX_WIDTH ** -0.5),
        'nat_rpb': nrm(ks[15], (DEPTH, N_HEADS_NAT, 2 * WIN_H - 1, 2 * WIN_W - 1), 0.5),
        'diff_lambda': nrm(ks[16], (DEPTH, 4, DIFF_QK_DIM), 0.1),
        'diff_subln': 1.0 + nrm(ks[17], (DEPTH, DIFF_V_DIM), 0.05),
        'w_router': nrm(ks[18], (D, N_EXPERTS), D ** -0.5),
        'router_bias': nrm(ks[19], (N_EXPERTS,), 0.01),
        'w_gate': nrm(ks[20], (DEPTH, N_EXPERTS, D, D_FF), D ** -0.5),
        'w_up': nrm(ks[21], (DEPTH, N_EXPERTS, D, D_FF), D ** -0.5),
        'w_down': nrm(ks[22], (DEPTH, N_EXPERTS, D_FF, D), D_FF ** -0.5),
    }


def reference(x_prompt, x_sample, cache_nat_k, cache_nat_v, cache_diff_k, cache_diff_v, c, c_ctx,
              w_ada, b_ada, norm1, norm2, norm_final, w_in, w_out, nat_rpb, diff_lambda, diff_subln,
              w_router, router_bias, w_gate, w_up, w_down):
    xp, xs = x_prompt, x_sample
    nat_k_list, nat_v_list, diff_k_list, diff_v_list = [], [], [], []
    for l in range(DEPTH):
        lw = {'norm1': norm1[l], 'norm2': norm2[l], 'w_in': w_in[l], 'w_out': w_out[l],
              'nat_rpb': nat_rpb[l], 'diff_lambda': diff_lambda[l], 'diff_subln': diff_subln[l],
              'w_gate': w_gate[l], 'w_up': w_up[l], 'w_down': w_down[l]}
        mod_ctx = modulation(c_ctx, w_ada[l], b_ada[l])
        mod_lat = modulation(c, w_ada[l], b_ada[l])
        xp, (nk, nv, dk, dv) = block(xp, mod_ctx, lw, l, None, w_router, router_bias)
        nat_k_list.append(nk)
        nat_v_list.append(nv)
        diff_k_list.append(dk)
        diff_v_list.append(dv)
        ctx = (cache_nat_k[:, l], cache_nat_v[:, l], cache_diff_k[:, l], cache_diff_v[:, l])
        xs, _ = block(xs, mod_lat, lw, l, ctx, w_router, router_bias)
    y_prompt = rmsnorm(xp, norm_final)
    y_sample = rmsnorm(xs, norm_final)
    new_nat_k = jnp.stack(nat_k_list, axis=1)
    new_nat_v = jnp.stack(nat_v_list, axis=1)
    new_diff_k = jnp.stack(diff_k_list, axis=1)
    new_diff_v = jnp.stack(diff_v_list, axis=1)
    return (y_prompt, y_sample, new_nat_k, new_nat_v, new_diff_k, new_diff_v)
```

```python
import functools
import math

import jax
import jax.numpy as jnp
from jax import lax
from jax.experimental import pallas as pl
from jax.experimental.pallas import tpu as pltpu

D_MODEL = 2048
BATCH = 16
SEQ = 256
DEPTH = 4
DEC_BATCH = 4
DEC_SEQ = 1024
PAST_LEN = 512
GRID_W = 64
GRID_H = DEC_SEQ // GRID_W
N_HEADS = 8
HEAD_DIM = 128
DIFF_QK_DIM = 64
WIN_H = 8
WIN_W = 16
N_EXPERTS = 16
N_GROUPS = 4
EXPERTS_PER_GROUP = N_EXPERTS // N_GROUPS
D_FF = 1024
ROPE_BASE = 10000.0
EPS = 1e-6
IN_COLS = 6 * N_HEADS * HEAD_DIM
N_PROMPT = BATCH * SEQ
N_TOK = N_PROMPT + DEC_BATCH * DEC_SEQ
N_PAIRS = 2 * N_TOK

PRE_TM = 256
INPROJ_TM = 1024
INPROJ_TN = 512
OUTPROJ_TM = 512
OUTPROJ_TK = 512
DISPATCH_TM = 256
EXPERT_TM = 512
EXPERT_FC = 256
EXPERT_CAP = N_PAIRS + N_EXPERTS * EXPERT_TM
EXPERT_TILES = EXPERT_CAP // EXPERT_TM
MOD_TN = 1024
Q_BLOCK = 256

F32 = jnp.float32
BF16 = jnp.bfloat16
MIB = 1024 * 1024
NT_DIMS = (((1,), (1,)), ((), ()))


def _params(vmem_mib):
    return pltpu.CompilerParams(vmem_limit_bytes=vmem_mib * MIB)


def _lam_init(layer):
    return 0.8 - 0.6 * math.exp(-0.3 * layer)


def _rms(x, gain):
    return x * lax.rsqrt(jnp.mean(x * x, axis=-1, keepdims=True) + EPS) * gain


def _mod_row(i, tm):
    first = i * tm
    return jnp.where(first < N_PROMPT, 0, 1 + (first - N_PROMPT) // DEC_SEQ)


def _mod_kernel(cond_ref, w_ref, b_ref, o_ref):
    c = cond_ref[...]
    s = c * jax.nn.sigmoid(c)
    o_ref[...] = jnp.dot(s.astype(BF16), w_ref[...].astype(BF16),
                         preferred_element_type=F32) + b_ref[...]


def _modulation(cond, w_ada, b_ada):
    n = 6 * D_MODEL
    out = pl.pallas_call(
        _mod_kernel,
        out_shape=jax.ShapeDtypeStruct((DEPTH, 8, n), F32),
        grid=(DEPTH, n // MOD_TN),
        in_specs=[
            pl.BlockSpec((8, D_MODEL), lambda l, j: (0, 0)),
            pl.BlockSpec((None, D_MODEL, MOD_TN), lambda l, j: (l, 0, j)),
            pl.BlockSpec((None, 1, MOD_TN), lambda l, j: (l, 0, j)),
        ],
        out_specs=pl.BlockSpec((None, 8, MOD_TN), lambda l, j: (l, 0, j)),
        compiler_params=_params(40),
        name="adaln_modulation",
    )(cond, w_ada, b_ada.reshape(DEPTH, 1, n))
    return out.reshape(DEPTH, 8, 6, D_MODEL)


def _gather_expert_rows(pos_ref, ys_hbm, ybuf, sem, base, tm):
    def issue(r, carry):
        for k in range(2):
            p = pos_ref[k * N_TOK + base + r]
            pltpu.make_async_copy(ys_hbm.at[pl.ds(p, 1), :],
                                  ybuf.at[k, pl.ds(r, 1), :], sem.at[k]).start()
        return carry

    lax.fori_loop(0, tm, issue, 0, unroll=8)
    for k in range(2):
        pltpu.make_async_copy(ys_hbm.at[pl.ds(0, tm), :], ybuf.at[k], sem.at[k]).wait()


def _pre_first_kernel(x_ref, mod_ref, g_ref, h_ref):
    x = x_ref[...]
    h = _rms(x, g_ref[...]) * (1.0 + mod_ref[1:2, :]) + mod_ref[0:1, :]
    h_ref[...] = h.astype(BF16)


def _pre_mid_kernel(pos_ref, x_ref, ys_hbm, cw_ref, modp_ref, mod_ref, g_ref,
                    xo_ref, h_ref, ybuf, sem):
    base = pl.program_id(0) * PRE_TM
    _gather_expert_rows(pos_ref, ys_hbm, ybuf, sem, base, PRE_TM)
    cw = cw_ref[...]
    y = cw[:, 0:1] * ybuf[0] + cw[:, 1:2] * ybuf[1]
    x = x_ref[...] + modp_ref[5:6, :] * y
    xo_ref[...] = x
    h = _rms(x, g_ref[...]) * (1.0 + mod_ref[1:2, :]) + mod_ref[0:1, :]
    h_ref[...] = h.astype(BF16)


def _pre_final_kernel(pos_ref, x_ref, ys_hbm, cw_ref, modp_ref, g_ref, y_ref, ybuf, sem):
    base = pl.program_id(0) * PRE_TM
    _gather_expert_rows(pos_ref, ys_hbm, ybuf, sem, base, PRE_TM)
    cw = cw_ref[...]
    y = cw[:, 0:1] * ybuf[0] + cw[:, 1:2] * ybuf[1]
    x = x_ref[...] + modp_ref[5:6, :] * y
    y_ref[...] = _rms(x, g_ref[...])


def _pre_first(x, mods, layer, gain):
    tm = PRE_TM
    return pl.pallas_call(
        _pre_first_kernel,
        out_shape=jax.ShapeDtypeStruct((N_TOK, D_MODEL), BF16),
        grid=(N_TOK // tm,),
        in_specs=[
            pl.BlockSpec((tm, D_MODEL), lambda i: (i, 0)),
            pl.BlockSpec((None, None, 6, D_MODEL), lambda i: (layer, _mod_row(i, tm), 0, 0)),
            pl.BlockSpec((1, D_MODEL), lambda i: (0, 0)),
        ],
        out_specs=pl.BlockSpec((tm, D_MODEL), lambda i: (i, 0)),
        compiler_params=_params(32),
        name="pre_first",
    )(x, mods, gain)


def _pre_combine(x, ys, pos, cw, mods, prev_layer, gain, final):
    tm = PRE_TM
    row = lambda i, pos_ref: (i, 0)
    fixed = lambda i, pos_ref: (0, 0)
    in_specs = [
        pl.BlockSpec((tm, D_MODEL), row),
        pl.BlockSpec(memory_space=pl.ANY),
        pl.BlockSpec((tm, 2), row),
        pl.BlockSpec((None, None, 6, D_MODEL),
                     lambda i, pos_ref: (prev_layer, _mod_row(i, tm), 0, 0)),
    ]
    args = [x, ys, cw, mods]
    if final:
        kern = _pre_final_kernel
        out_shape = jax.ShapeDtypeStruct((N_TOK, D_MODEL), F32)
        out_specs = pl.BlockSpec((tm, D_MODEL), row)
    else:
        kern = _pre_mid_kernel
        in_specs.append(pl.BlockSpec((None, None, 6, D_MODEL),
                                     lambda i, pos_ref: (prev_layer + 1, _mod_row(i, tm), 0, 0)))
        args.append(mods)
        out_shape = (jax.ShapeDtypeStruct((N_TOK, D_MODEL), F32),
                     jax.ShapeDtypeStruct((N_TOK, D_MODEL), BF16))
        out_specs = (pl.BlockSpec((tm, D_MODEL), row), pl.BlockSpec((tm, D_MODEL), row))
    in_specs.append(pl.BlockSpec((1, D_MODEL), fixed))
    args.append(gain)
    return pl.pallas_call(
        kern,
        out_shape=out_shape,
        grid_spec=pltpu.PrefetchScalarGridSpec(
            num_scalar_prefetch=1,
            grid=(N_TOK // tm,),
            in_specs=in_specs,
            out_specs=out_specs,
            scratch_shapes=[pltpu.VMEM((2, tm, D_MODEL), F32), pltpu.SemaphoreType.DMA((2,))],
        ),
        compiler_params=_params(40),
        name="pre_final" if final else "pre_combine",
    )(pos, *args)


def _in_proj_kernel(h_ref, w_ref, o_ref):
    o_ref[...] = jnp.dot(h_ref[...], w_ref[...].astype(BF16), preferred_element_type=F32)


def _in_proj(h, w_in, layer):
    tm, tn = INPROJ_TM, INPROJ_TN
    return pl.pallas_call(
        _in_proj_kernel,
        out_shape=jax.ShapeDtypeStruct((N_TOK, IN_COLS), F32),
        grid=(N_TOK // tm, IN_COLS // tn),
        in_specs=[
            pl.BlockSpec((tm, D_MODEL), lambda i, j: (i, 0)),
            pl.BlockSpec((None, D_MODEL, tn), lambda i, j: (layer, 0, j)),
        ],
        out_specs=pl.BlockSpec((tm, tn), lambda i, j: (i, j)),
        compiler_params=_params(48),
        name="in_proj",
    )(h, w_in)


def _lambda(lam_ref, layer):
    lp = lam_ref[...]
    a = jnp.sum(lp[0:1, :] * lp[1:2, :], axis=-1, keepdims=True)
    b = jnp.sum(lp[2:3, :] * lp[3:4, :], axis=-1, keepdims=True)
    return jnp.exp(a) - jnp.exp(b) + _lam_init(layer)


def _split_q(q):
    lane = lax.broadcasted_iota(jnp.int32, q.shape, 1)
    q1 = jnp.where(lane < DIFF_QK_DIM, q, 0.0).astype(BF16)
    q2 = jnp.where(lane >= DIFF_QK_DIM, q, 0.0).astype(BF16)
    return q1, q2


def _sub_ln(o, gain, layer):
    return _rms(o, gain) * (1.0 - _lam_init(layer))


def _attn_prompt_kernel(nq_ref, nk_ref, nv_ref, dq_ref, dk_ref, dv_ref, lam_ref, sg_ref,
                        c0_in, c1_in, c2_in, c3_in,
                        on_ref, od_ref, onk_ref, onv_ref, odk_ref, odv_ref, *, layer):
    del c0_in, c1_in, c2_in, c3_in
    nk = nk_ref[...]
    nv = nv_ref[...]
    dk = dk_ref[...]
    dv = dv_ref[...]
    onk_ref[...] = nk
    onv_ref[...] = nv
    odk_ref[...] = dk
    odv_ref[...] = dv

    q = (nq_ref[...] * (HEAD_DIM ** -0.5)).astype(BF16)
    s = lax.dot_general(q, nk.astype(BF16), NT_DIMS, preferred_element_type=F32)
    e = jnp.exp(s - jnp.max(s, axis=-1, keepdims=True))
    z = jnp.sum(e, axis=-1, keepdims=True)
    on_ref[...] = jnp.dot(e.astype(BF16), nv.astype(BF16), preferred_element_type=F32) / z

    lam = _lambda(lam_ref, layer)
    q1, q2 = _split_q(dq_ref[...] * (DIFF_QK_DIM ** -0.5))
    kb = dk.astype(BF16)
    s1 = lax.dot_general(q1, kb, NT_DIMS, preferred_element_type=F32)
    s2 = lax.dot_general(q2, kb, NT_DIMS, preferred_element_type=F32)
    e1 = jnp.exp(s1 - jnp.max(s1, axis=-1, keepdims=True))
    e2 = jnp.exp(s2 - jnp.max(s2, axis=-1, keepdims=True))
    a1 = 1.0 / jnp.sum(e1, axis=-1, keepdims=True)
    a2 = lam / jnp.sum(e2, axis=-1, keepdims=True)
    p = e1 * a1 - e2 * a2
    o = jnp.dot(p.astype(BF16), dv.astype(BF16), preferred_element_type=F32)
    od_ref[...] = _sub_ln(o, sg_ref[...], layer)


def _attn_prompt(qkv, lam_p, subln, caches, layer):
    h8 = N_HEADS
    blk = lambda off: pl.BlockSpec((SEQ, HEAD_DIM), lambda b, h: (b, off + h))
    in_specs = [blk(0), blk(h8), blk(2 * h8), blk(3 * h8), blk(4 * h8), blk(5 * h8),
                pl.BlockSpec((None, 4, DIFF_QK_DIM), lambda b, h: (layer, 0, 0)),
                pl.BlockSpec((None, 1, HEAD_DIM), lambda b, h: (layer, 0, 0))]
    args = [qkv] * 6 + [lam_p, subln]
    aliases = {}
    for n, cache in enumerate(caches):
        aliases[len(args)] = 2 + n
        in_specs.append(pl.BlockSpec(memory_space=pl.ANY))
        args.append(cache)
    cache_shape = jax.ShapeDtypeStruct((BATCH, DEPTH, N_HEADS, SEQ, HEAD_DIM), F32)
    cache_spec = pl.BlockSpec((None, None, None, SEQ, HEAD_DIM), lambda b, h: (b, layer, h, 0, 0))
    out = pl.pallas_call(
        functools.partial(_attn_prompt_kernel, layer=layer),
        out_shape=(jax.ShapeDtypeStruct((N_PROMPT, h8 * HEAD_DIM), F32),
                   jax.ShapeDtypeStruct((N_PROMPT, h8 * HEAD_DIM), F32),
                   cache_shape, cache_shape, cache_shape, cache_shape),
        grid=(BATCH, h8),
        in_specs=in_specs,
        out_specs=(pl.BlockSpec((SEQ, HEAD_DIM), lambda b, h: (b, h)),
                   pl.BlockSpec((SEQ, HEAD_DIM), lambda b, h: (b, h)),
                   cache_spec, cache_spec, cache_spec, cache_spec),
        input_output_aliases=aliases,
        compiler_params=_params(32),
        name="attn_prompt",
    )(*args)
    return out[0], out[1], out[2:]


def _attn_latent_kernel(nq_ref, nk_ref, nv_ref, dq_ref, dk_ref, dv_ref,
                        cnk_ref, cnv_ref, cdk_ref, cdv_ref, tab_ref, cos_ref, sin_ref,
                        lam_ref, sg_ref, on_ref, od_ref,
                        kb, vb, ckb, cvb, q1b, q2b, *, layer):
    kb[...] = nk_ref[...].astype(BF16)
    vb[...] = nv_ref[...].astype(BF16)
    ckb[...] = cnk_ref[...].astype(BF16)
    cvb[...] = cnv_ref[...].astype(BF16)
    win = WIN_H * GRID_W

    def nat_row(r, carry):
        rs = jnp.clip(r - WIN_H // 2, 0, GRID_H - WIN_H)
        d0 = rs - r + WIN_H - 1
        q0 = pl.multiple_of(r * GRID_W, GRID_W)
        k0 = pl.multiple_of(rs * GRID_W, GRID_W)
        q = (nq_ref[pl.ds(q0, GRID_W), :] * (HEAD_DIM ** -0.5)).astype(BF16)
        s_nb = lax.dot_general(q, kb[pl.ds(k0, win), :], NT_DIMS, preferred_element_type=F32)
        bias = jnp.concatenate([tab_ref[d0 + 2 * i] for i in range(WIN_H // 2)], axis=1)
        s_nb = s_nb + bias
        s_cx = lax.dot_general(q, ckb[...], NT_DIMS, preferred_element_type=F32)
        m = jnp.maximum(jnp.max(s_nb, axis=-1, keepdims=True),
                        jnp.max(s_cx, axis=-1, keepdims=True))
        e_nb = jnp.exp(s_nb - m)
        e_cx = jnp.exp(s_cx - m)
        z = jnp.sum(e_nb, axis=-1, keepdims=True) + jnp.sum(e_cx, axis=-1, keepdims=True)
        o = (jnp.dot(e_nb.astype(BF16), vb[pl.ds(k0, win), :], preferred_element_type=F32)
             + jnp.dot(e_cx.astype(BF16), cvb[...], preferred_element_type=F32))
        on_ref[pl.ds(q0, GRID_W), :] = o / z
        return carry

    lax.fori_loop(0, GRID_H, nat_row, 0)

    lane = lax.broadcasted_iota(jnp.int32, (DEC_SEQ, HEAD_DIM), 1)
    first_half = (lane & 16) == 0

    def rope(x):
        rot = jnp.where(first_half, pltpu.roll(x, HEAD_DIM - 16, 1), pltpu.roll(x, 16, 1))
        return x * cos_ref[...] + rot * sin_ref[...]

    q1, q2 = _split_q(rope(dq_ref[...]) * (DIFF_QK_DIM ** -0.5))
    q1b[...] = q1
    q2b[...] = q2
    kb[...] = rope(dk_ref[...]).astype(BF16)
    vb[...] = dv_ref[...].astype(BF16)
    ckb[...] = cdk_ref[...].astype(BF16)
    cvb[...] = cdv_ref[...].astype(BF16)
    lam = _lambda(lam_ref, layer)

    def diff_block(j, carry):
        r0 = pl.multiple_of(j * Q_BLOCK, Q_BLOCK)
        qa = q1b[pl.ds(r0, Q_BLOCK), :]
        qb = q2b[pl.ds(r0, Q_BLOCK), :]
        s1l = lax.dot_general(qa, kb[...], NT_DIMS, preferred_element_type=F32)
        s1c = lax.dot_general(qa, ckb[...], NT_DIMS, preferred_element_type=F32)
        s2l = lax.dot_general(qb, kb[...], NT_DIMS, preferred_element_type=F32)
        s2c = lax.dot_general(qb, ckb[...], NT_DIMS, preferred_element_type=F32)
        m1 = jnp.maximum(jnp.max(s1l, axis=-1, keepdims=True), jnp.max(s1c, axis=-1, keepdims=True))
        m2 = jnp.maximum(jnp.max(s2l, axis=-1, keepdims=True), jnp.max(s2c, axis=-1, keepdims=True))
        e1l = jnp.exp(s1l - m1)
        e1c = jnp.exp(s1c - m1)
        e2l = jnp.exp(s2l - m2)
        e2c = jnp.exp(s2c - m2)
        a1 = 1.0 / (jnp.sum(e1l, axis=-1, keepdims=True) + jnp.sum(e1c, axis=-1, keepdims=True))
        a2 = lam / (jnp.sum(e2l, axis=-1, keepdims=True) + jnp.sum(e2c, axis=-1, keepdims=True))
        p_l = (e1l * a1 - e2l * a2).astype(BF16)
        p_c = (e1c * a1 - e2c * a2).astype(BF16)
        o = (jnp.dot(p_l, vb[...], preferred_element_type=F32)
             + jnp.dot(p_c, cvb[...], preferred_element_type=F32))
        od_ref[pl.ds(r0, Q_BLOCK), :] = _sub_ln(o, sg_ref[...], layer)
        return carry

    lax.fori_loop(0, DEC_SEQ // Q_BLOCK, diff_block, 0)


def _attn_latent(qkv, caches, tab, cos, sin, lam_p, subln, layer):
    h8 = N_HEADS
    row0 = N_PROMPT // DEC_SEQ
    blk = lambda off: pl.BlockSpec((DEC_SEQ, HEAD_DIM), lambda b, h: (row0 + b, off + h))
    cblk = pl.BlockSpec((None, None, None, PAST_LEN, HEAD_DIM), lambda b, h: (b, layer, h, 0, 0))
    fixed = pl.BlockSpec((DEC_SEQ, HEAD_DIM), lambda b, h: (0, 0))
    out_spec = pl.BlockSpec((DEC_SEQ, HEAD_DIM), lambda b, h: (b, h))
    merged_shape = jax.ShapeDtypeStruct((N_TOK - N_PROMPT, h8 * HEAD_DIM), F32)
    return pl.pallas_call(
        functools.partial(_attn_latent_kernel, layer=layer),
        out_shape=(merged_shape, merged_shape),
        grid=(DEC_BATCH, h8),
        in_specs=[blk(0), blk(h8), blk(2 * h8), blk(3 * h8), blk(4 * h8), blk(5 * h8),
                  cblk, cblk, cblk, cblk,
                  pl.BlockSpec((None, 2 * WIN_H - 2, GRID_W, 2 * GRID_W), lambda b, h: (h, 0, 0, 0)),
                  fixed, fixed,
                  pl.BlockSpec((None, 4, DIFF_QK_DIM), lambda b, h: (layer, 0, 0)),
                  pl.BlockSpec((None, 1, HEAD_DIM), lambda b, h: (layer, 0, 0))],
        out_specs=(out_spec, out_spec),
        scratch_shapes=[pltpu.VMEM((DEC_SEQ, HEAD_DIM), BF16), pltpu.VMEM((DEC_SEQ, HEAD_DIM), BF16),
                        pltpu.VMEM((PAST_LEN, HEAD_DIM), BF16), pltpu.VMEM((PAST_LEN, HEAD_DIM), BF16),
                        pltpu.VMEM((DEC_SEQ, HEAD_DIM), BF16), pltpu.VMEM((DEC_SEQ, HEAD_DIM), BF16)],
        compiler_params=_params(48),
        name="attn_latent",
    )(qkv, qkv, qkv, qkv, qkv, qkv, *caches, tab, cos, sin, lam_p, subln)


def _route(logits, bias):
    ex = jnp.exp(logits - jnp.max(logits, axis=0, keepdims=True))
    probs = ex / jnp.sum(ex, axis=0, keepdims=True)
    sel = probs + bias
    srow = [sel[e:e + 1, :] for e in range(N_EXPERTS)]
    prow = [probs[e:e + 1, :] for e in range(N_EXPERTS)]
    gscore = []
    for g in range(N_GROUPS):
        v = srow[g * EXPERTS_PER_GROUP:(g + 1) * EXPERTS_PER_GROUP]
        best = None
        for a in range(EXPERTS_PER_GROUP):
            for b in range(a + 1, EXPERTS_PER_GROUP):
                pair = v[a] + v[b]
                best = pair if best is None else jnp.maximum(best, pair)
        gscore.append(best)
    gbest = gscore[0]
    gidx = jnp.zeros(gbest.shape, jnp.int32)
    for g in range(1, N_GROUPS):
        better = gscore[g] > gbest
        gidx = jnp.where(better, g, gidx)
        gbest = jnp.where(better, gscore[g], gbest)
    neg = jnp.full(gbest.shape, -jnp.inf, F32)
    picks = []
    taken = None
    for _ in range(2):
        best = neg
        idx = jnp.zeros(gbest.shape, jnp.int32)
        wgt = jnp.zeros(gbest.shape, F32)
        for e in range(N_EXPERTS):
            ok = gidx == (e // EXPERTS_PER_GROUP)
            if taken is not None:
                ok = jnp.logical_and(ok, taken != e)
            cand = jnp.where(ok, srow[e], neg)
            better = cand > best
            idx = jnp.where(better, e, idx)
            wgt = jnp.where(better, prow[e], wgt)
            best = jnp.where(better, cand, best)
        picks.append((idx, wgt))
        taken = idx
    (i0, w0), (i1, w1) = picks
    wsum = w0 + w1
    return i0, i1, w0 / wsum, w1 / wsum


def _out_proj_kernel(pn_ref, pd_ref, sn_ref, sd_ref, w_ref, x_ref, mod_ref, g_ref, wr_ref, rb_ref,
                     xo_ref, h_ref, idx_ref, cw_ref, acc_ref):
    i = pl.program_id(0)
    k = pl.program_id(1)
    nk = pl.num_programs(1)
    half = nk // 2
    prompt = i < N_PROMPT // OUTPROJ_TM
    nat = k < half

    @pl.when(k == 0)
    def _():
        acc_ref[...] = jnp.zeros_like(acc_ref)

    def accumulate(cond, m_ref):
        @pl.when(cond)
        def _():
            acc_ref[...] += jnp.dot(m_ref[...].astype(BF16), w_ref[...].astype(BF16),
                                    preferred_element_type=F32)

    accumulate(jnp.logical_and(prompt, nat), pn_ref)
    accumulate(jnp.logical_and(prompt, jnp.logical_not(nat)), pd_ref)
    accumulate(jnp.logical_and(jnp.logical_not(prompt), nat), sn_ref)
    accumulate(jnp.logical_and(jnp.logical_not(prompt), jnp.logical_not(nat)), sd_ref)

    @pl.when(k == nk - 1)
    def _():
        x = x_ref[...] + mod_ref[2:3, :] * acc_ref[...]
        xo_ref[...] = x
        h = _rms(x, g_ref[...]) * (1.0 + mod_ref[4:5, :]) + mod_ref[3:4, :]
        h_ref[...] = h
        logits = lax.dot_general(wr_ref[...], h, NT_DIMS, precision=lax.Precision.HIGHEST,
                                 preferred_element_type=F32)
        i0, i1, w0, w1 = _route(logits, rb_ref[...])
        idx_ref[...] = jnp.concatenate([i0, i1], axis=0)
        cw_ref[...] = jnp.concatenate([w0, w1], axis=0)


def _out_proj(merged, w_out, x, mods, layer, gain, w_router_t, router_bias):
    tm, tk = OUTPROJ_TM, OUTPROJ_TK
    nk = D_MODEL // tk
    half = nk // 2
    npt = N_PROMPT // tm
    row = lambda i, k: (i, 0)
    fixed = lambda i, k: (0, 0)
    p_row = lambda i: jnp.minimum(i, npt - 1)
    s_row = lambda i: jnp.maximum(i - npt, 0)
    n_col = lambda k: jnp.minimum(k, half - 1)
    d_col = lambda k: jnp.maximum(k - half, 0)
    return pl.pallas_call(
        _out_proj_kernel,
        out_shape=(jax.ShapeDtypeStruct((N_TOK, D_MODEL), F32),
                   jax.ShapeDtypeStruct((N_TOK, D_MODEL), F32),
                   jax.ShapeDtypeStruct((2, N_TOK), jnp.int32),
                   jax.ShapeDtypeStruct((2, N_TOK), F32)),
        grid=(N_TOK // tm, nk),
        in_specs=[
            pl.BlockSpec((tm, tk), lambda i, k: (p_row(i), n_col(k))),
            pl.BlockSpec((tm, tk), lambda i, k: (p_row(i), d_col(k))),
            pl.BlockSpec((tm, tk), lambda i, k: (s_row(i), n_col(k))),
            pl.BlockSpec((tm, tk), lambda i, k: (s_row(i), d_col(k))),
            pl.BlockSpec((None, tk, D_MODEL), lambda i, k: (layer, k, 0)),
            pl.BlockSpec((tm, D_MODEL), row),
            pl.BlockSpec((None, None, 6, D_MODEL), lambda i, k: (layer, _mod_row(i, tm), 0, 0)),
            pl.BlockSpec((1, D_MODEL), fixed),
            pl.BlockSpec((N_EXPERTS, D_MODEL), fixed),
            pl.BlockSpec((N_EXPERTS, 1), fixed),
        ],
        out_specs=(pl.BlockSpec((tm, D_MODEL), row), pl.BlockSpec((tm, D_MODEL), row),
                   pl.BlockSpec((2, tm), lambda i, k: (0, i)),
                   pl.BlockSpec((2, tm), lambda i, k: (0, i))),
        scratch_shapes=[pltpu.VMEM((tm, D_MODEL), F32)],
        compiler_params=_params(52),
        name="out_proj_router",
    )(*merged, w_out, x, mods, gain, w_router_t, router_bias)


def _dispatch_kernel(pos_ref, h_ref, xs_in, xs_out, sem):
    del xs_in
    tm = DISPATCH_TM
    base = pl.program_id(0) * tm

    def issue(r, carry):
        for k in range(2):
            p = pos_ref[k * N_TOK + base + r]
            pltpu.make_async_copy(h_ref.at[pl.ds(r, 1), :], xs_out.at[pl.ds(p, 1), :], sem).start()
        return carry

    lax.fori_loop(0, tm, issue, 0, unroll=8)
    for _ in range(2):
        pltpu.make_async_copy(h_ref, xs_out.at[pl.ds(0, tm), :], sem).wait()


def _dispatch(pos, h2, xs_buf):
    tm = DISPATCH_TM
    return pl.pallas_call(
        _dispatch_kernel,
        out_shape=jax.ShapeDtypeStruct((EXPERT_CAP, D_MODEL), F32),
        grid_spec=pltpu.PrefetchScalarGridSpec(
            num_scalar_prefetch=1,
            grid=(N_TOK // tm,),
            in_specs=[pl.BlockSpec((tm, D_MODEL), lambda i, pos_ref: (i, 0)),
                      pl.BlockSpec(memory_space=pl.ANY)],
            out_specs=pl.BlockSpec(memory_space=pl.ANY),
            scratch_shapes=[pltpu.SemaphoreType.DMA(())],
        ),
        input_output_aliases={2: 0},
        compiler_params=_params(32),
        name="expert_dispatch",
    )(pos, h2, xs_buf)


def _experts_kernel(te_ref, tv_ref, xs_ref, wg_ref, wu_ref, wd_ref, o_ref, xb_ref):
    del te_ref
    n = pl.program_id(0)
    c = pl.program_id(1)
    valid = tv_ref[n] == 1

    @pl.when(jnp.logical_and(valid, c == 0))
    def _():
        xb_ref[...] = xs_ref[...].astype(BF16)

    @pl.when(c == 0)
    def _():
        o_ref[...] = jnp.zeros_like(o_ref)

    @pl.when(valid)
    def _():
        xb = xb_ref[...]
        g = jnp.dot(xb, wg_ref[...].astype(BF16), preferred_element_type=F32)
        u = jnp.dot(xb, wu_ref[...].astype(BF16), preferred_element_type=F32)
        hid = (g * jax.nn.sigmoid(g) * u).astype(BF16)
        o_ref[...] += jnp.dot(hid, wd_ref[...].astype(BF16), preferred_element_type=F32)


def _experts(tile_expert, tile_valid, xs, w_gate, w_up, w_down, layer):
    tm, fc = EXPERT_TM, EXPERT_FC
    nc = D_FF // fc

    def chunk(n, c, te, tv):
        return jnp.where(tv[n] == 1, c, nc - 1)

    return pl.pallas_call(
        _experts_kernel,
        out_shape=jax.ShapeDtypeStruct((EXPERT_CAP, D_MODEL), F32),
        grid_spec=pltpu.PrefetchScalarGridSpec(
            num_scalar_prefetch=2,
            grid=(EXPERT_TILES, nc),
            in_specs=[
                pl.BlockSpec((tm, D_MODEL), lambda n, c, te, tv: (n, 0)),
                pl.BlockSpec((None, None, D_MODEL, fc),
                             lambda n, c, te, tv: (layer, te[n], 0, chunk(n, c, te, tv))),
                pl.BlockSpec((None, None, D_MODEL, fc),
                             lambda n, c, te, tv: (layer, te[n], 0, chunk(n, c, te, tv))),
                pl.BlockSpec((None, None, fc, D_MODEL),
                             lambda n, c, te, tv: (layer, te[n], chunk(n, c, te, tv), 0)),
            ],
            out_specs=pl.BlockSpec((tm, D_MODEL), lambda n, c, te, tv: (n, 0)),
            scratch_shapes=[pltpu.VMEM((tm, D_MODEL), BF16)],
        ),
        compiler_params=_params(48),
        name="expert_mlp",
    )(tile_expert, tile_valid, xs, w_gate, w_up, w_down)


def _dispatch_plan(idx):
    e = idx.reshape(-1)
    onehot = (e[:, None] == jnp.arange(N_EXPERTS, dtype=jnp.int32)[None, :]).astype(jnp.int32)
    csum = jnp.cumsum(onehot, axis=0)
    counts = csum[-1]
    rank = jnp.sum(onehot * csum, axis=1) - 1
    padded = ((counts + EXPERT_TM - 1) // EXPERT_TM) * EXPERT_TM
    ends = jnp.cumsum(padded)
    starts = ends - padded
    pos = jnp.sum(onehot * starts[None, :], axis=1) + rank
    tile_start = jnp.arange(EXPERT_TILES, dtype=jnp.int32) * EXPERT_TM
    tile_expert = jnp.minimum(
        jnp.sum((tile_start[:, None] >= ends[None, :]).astype(jnp.int32), axis=1), N_EXPERTS - 1)
    tile_valid = (tile_start < ends[-1]).astype(jnp.int32)
    return pos.astype(jnp.int32), tile_expert.astype(jnp.int32), tile_valid


def _bias_tables(nat_rpb):
    c = jnp.arange(GRID_W)[:, None]
    kc = jnp.arange(GRID_W)[None, :]
    qstart = jnp.clip(c - WIN_W // 2, 0, GRID_W - WIN_W)
    valid = (kc >= qstart) & (kc < qstart + WIN_W)
    dc = jnp.clip(kc - c, -(WIN_W - 1), WIN_W - 1) + WIN_W - 1
    tab = jnp.where(valid, nat_rpb[:, :, :, dc], -1e30)
    return jnp.concatenate([tab[:, :, :-1], tab[:, :, 1:]], axis=-1)


def _rope_tables():
    t = jnp.arange(DEC_SEQ)
    row = (t // GRID_W).astype(F32)
    col = (t % GRID_W).astype(F32)
    nf = DIFF_QK_DIM // 4
    inv = ROPE_BASE ** (-jnp.arange(nf, dtype=F32) / nf)
    ar = row[:, None] * inv[None, :]
    ac = col[:, None] * inv[None, :]
    cos = jnp.concatenate([jnp.cos(ar), jnp.cos(ar), jnp.cos(ac), jnp.cos(ac)], axis=-1)
    sin = jnp.concatenate([-jnp.sin(ar), jnp.sin(ar), -jnp.sin(ac), jnp.sin(ac)], axis=-1)
    return jnp.tile(cos, (1, 2)), jnp.tile(sin, (1, 2))


def kernel(x_prompt, x_sample, cache_nat_k, cache_nat_v, cache_diff_k, cache_diff_v, c, c_ctx,
           w_ada, b_ada, norm1, norm2, norm_final, w_in, w_out, nat_rpb, diff_lambda, diff_subln,
           w_router, router_bias, w_gate, w_up, w_down):
    x = jnp.concatenate([x_prompt.reshape(N_PROMPT, D_MODEL),
                         x_sample.reshape(DEC_BATCH * DEC_SEQ, D_MODEL)], axis=0)
    cond = jnp.concatenate([c_ctx[None, :], c, jnp.zeros((8 - 1 - DEC_BATCH, D_MODEL), F32)], axis=0)
    mods = _modulation(cond, w_ada, b_ada)
    tabs = _bias_tables(nat_rpb)
    cos, sin = _rope_tables()
    subln = diff_subln.reshape(DEPTH, 1, HEAD_DIM)
    w_router_t = w_router.T
    rbias = router_bias.reshape(N_EXPERTS, 1)
    lat_caches = (cache_nat_k, cache_nat_v, cache_diff_k, cache_diff_v)

    new_caches = tuple(jnp.zeros((BATCH, DEPTH, N_HEADS, SEQ, HEAD_DIM), F32) for _ in range(4))
    xs_buf = jnp.zeros((EXPERT_CAP, D_MODEL), F32)
    ys = pos = cw = None
    for layer in range(DEPTH):
        gain1 = norm1[layer].reshape(1, D_MODEL)
        if layer == 0:
            h = _pre_first(x, mods, layer, gain1)
        else:
            x, h = _pre_combine(x, ys, pos, cw, mods, layer - 1, gain1, final=False)
        qkv = _in_proj(h, w_in, layer)
        pn, pd, new_caches = _attn_prompt(qkv, diff_lambda, subln, new_caches, layer)
        sn, sd = _attn_latent(qkv, lat_caches, tabs[layer], cos, sin, diff_lambda, subln, layer)
        x, h2, idx, cwt = _out_proj((pn, pd, sn, sd), w_out, x, mods, layer,
                                    norm2[layer].reshape(1, D_MODEL), w_router_t, rbias)
        pos, tile_expert, tile_valid = _dispatch_plan(idx)
        cw = cwt.T
        xs_buf = _dispatch(pos, h2, xs_buf)
        ys = _experts(tile_expert, tile_valid, xs_buf, w_gate, w_up, w_down, layer)
    y = _pre_combine(x, ys, pos, cw, mods, DEPTH - 1, norm_final.reshape(1, D_MODEL), final=True)
    y_prompt = y[:N_PROMPT].reshape(BATCH, SEQ, D_MODEL)
    y_sample = y[N_PROMPT:].reshape(DEC_BATCH, DEC_SEQ, D_MODEL)
    return (y_prompt, y_sample) + tuple(new_caches)
```

```python
import functools
import math

import jax
import jax.numpy as jnp
from jax import lax
from jax.experimental import pallas as pl
from jax.experimental.pallas import tpu as pltpu

D_MODEL = 2048
BATCH = 16
SEQ = 256
DEPTH = 4
DEC_BATCH = 4
DEC_SEQ = 1024
PAST_LEN = 512
GRID_W = 64
GRID_H = DEC_SEQ // GRID_W
N_HEADS = 8
HEAD_DIM = 128
DIFF_QK_DIM = 64
WIN_H = 8
WIN_W = 16
N_EXPERTS = 16
N_GROUPS = 4
EXPERTS_PER_GROUP = N_EXPERTS // N_GROUPS
D_FF = 1024
ROPE_BASE = 10000.0
EPS = 1e-6
IN_COLS = 6 * N_HEADS * HEAD_DIM
N_PROMPT = BATCH * SEQ
N_TOK = N_PROMPT + DEC_BATCH * DEC_SEQ
N_PAIRS = 2 * N_TOK

PRE_TM = 256
INPROJ_TM = 2048
INPROJ_TN = 512
OUTPROJ_TM = 256
OUTPROJ_WCHUNKS = 4
DISPATCH_TM = 256
EXPERT_TM = 256
EXPERT_CAP = N_PAIRS + N_EXPERTS * EXPERT_TM
EXPERT_TILES = EXPERT_CAP // EXPERT_TM
EXPERT_WCHUNKS = 4
EXPERT_STEPS = EXPERT_WCHUNKS * (N_EXPERTS + 1) + EXPERT_TILES
MOD_TN = 1024
Q_BLOCK = 256

F32 = jnp.float32
BF16 = jnp.bfloat16
MIB = 1024 * 1024
NT_DIMS = (((1,), (1,)), ((), ()))


def _params(vmem_mib):
    return pltpu.CompilerParams(vmem_limit_bytes=vmem_mib * MIB)


def _lam_init(layer):
    return 0.8 - 0.6 * math.exp(-0.3 * layer)


def _rms(x, gain):
    return x * lax.rsqrt(jnp.mean(x * x, axis=-1, keepdims=True) + EPS) * gain


def _mod_row(i, tm):
    first = i * tm
    return jnp.where(first < N_PROMPT, 0, 1 + (first - N_PROMPT) // DEC_SEQ)


def _mod_kernel(cond_ref, w_ref, b_ref, o_ref):
    c = cond_ref[...]
    s = c * jax.nn.sigmoid(c)
    o_ref[...] = jnp.dot(s.astype(BF16), w_ref[...].astype(BF16),
                         preferred_element_type=F32) + b_ref[...]


def _modulation(cond, w_ada, b_ada):
    n = 6 * D_MODEL
    out = pl.pallas_call(
        _mod_kernel,
        out_shape=jax.ShapeDtypeStruct((DEPTH, 8, n), F32),
        grid=(DEPTH, n // MOD_TN),
        in_specs=[
            pl.BlockSpec((8, D_MODEL), lambda l, j: (0, 0)),
            pl.BlockSpec((None, D_MODEL, MOD_TN), lambda l, j: (l, 0, j)),
            pl.BlockSpec((None, 1, MOD_TN), lambda l, j: (l, 0, j)),
        ],
        out_specs=pl.BlockSpec((None, 8, MOD_TN), lambda l, j: (l, 0, j)),
        compiler_params=_params(40),
        name="adaln_modulation",
    )(cond, w_ada, b_ada.reshape(DEPTH, 1, n))
    return out.reshape(DEPTH, 8, 6, D_MODEL)


def _gather_expert_rows(pos_ref, ys_hbm, ybuf, sem, base, tm):
    def issue(r, carry):
        for k in range(2):
            p = pos_ref[k * N_TOK + base + r]
            pltpu.make_async_copy(ys_hbm.at[pl.ds(p, 1), :],
                                  ybuf.at[k, pl.ds(r, 1), :], sem.at[k]).start()
        return carry

    lax.fori_loop(0, tm, issue, 0, unroll=8)
    for k in range(2):
        pltpu.make_async_copy(ys_hbm.at[pl.ds(0, tm), :], ybuf.at[k], sem.at[k]).wait()


def _pre_first_kernel(x_ref, mod_ref, g_ref, h_ref):
    x = x_ref[...]
    h = _rms(x, g_ref[...]) * (1.0 + mod_ref[1:2, :]) + mod_ref[0:1, :]
    h_ref[...] = h.astype(BF16)


def _pre_mid_kernel(pos_ref, x_ref, ys_hbm, cw_ref, modp_ref, mod_ref, g_ref,
                    xo_ref, h_ref, ybuf, sem):
    base = pl.program_id(0) * PRE_TM
    _gather_expert_rows(pos_ref, ys_hbm, ybuf, sem, base, PRE_TM)
    cw = cw_ref[...]
    y = cw[:, 0:1] * ybuf[0] + cw[:, 1:2] * ybuf[1]
    x = x_ref[...] + modp_ref[5:6, :] * y
    xo_ref[...] = x
    h = _rms(x, g_ref[...]) * (1.0 + mod_ref[1:2, :]) + mod_ref[0:1, :]
    h_ref[...] = h.astype(BF16)


def _pre_final_kernel(pos_ref, x_ref, ys_hbm, cw_ref, modp_ref, g_ref, y_ref, ybuf, sem):
    base = pl.program_id(0) * PRE_TM
    _gather_expert_rows(pos_ref, ys_hbm, ybuf, sem, base, PRE_TM)
    cw = cw_ref[...]
    y = cw[:, 0:1] * ybuf[0] + cw[:, 1:2] * ybuf[1]
    x = x_ref[...] + modp_ref[5:6, :] * y
    y_ref[...] = _rms(x, g_ref[...])


def _pre_first(x, mods, layer, gain):
    tm = PRE_TM
    return pl.pallas_call(
        _pre_first_kernel,
        out_shape=jax.ShapeDtypeStruct((N_TOK, D_MODEL), BF16),
        grid=(N_TOK // tm,),
        in_specs=[
            pl.BlockSpec((tm, D_MODEL), lambda i: (i, 0)),
            pl.BlockSpec((None, None, 6, D_MODEL), lambda i: (layer, _mod_row(i, tm), 0, 0)),
            pl.BlockSpec((1, D_MODEL), lambda i: (0, 0)),
        ],
        out_specs=pl.BlockSpec((tm, D_MODEL), lambda i: (i, 0)),
        compiler_params=_params(32),
        name="pre_first",
    )(x, mods, gain)


def _pre_combine(x, ys, pos, cw, mods, prev_layer, gain, final):
    tm = PRE_TM
    row = lambda i, pos_ref: (i, 0)
    fixed = lambda i, pos_ref: (0, 0)
    in_specs = [
        pl.BlockSpec((tm, D_MODEL), row),
        pl.BlockSpec(memory_space=pl.ANY),
        pl.BlockSpec((tm, 2), row),
        pl.BlockSpec((None, None, 6, D_MODEL),
                     lambda i, pos_ref: (prev_layer, _mod_row(i, tm), 0, 0)),
    ]
    args = [x, ys, cw, mods]
    if final:
        kern = _pre_final_kernel
        out_shape = jax.ShapeDtypeStruct((N_TOK, D_MODEL), F32)
        out_specs = pl.BlockSpec((tm, D_MODEL), row)
    else:
        kern = _pre_mid_kernel
        in_specs.append(pl.BlockSpec((None, None, 6, D_MODEL),
                                     lambda i, pos_ref: (prev_layer + 1, _mod_row(i, tm), 0, 0)))
        args.append(mods)
        out_shape = (jax.ShapeDtypeStruct((N_TOK, D_MODEL), F32),
                     jax.ShapeDtypeStruct((N_TOK, D_MODEL), BF16))
        out_specs = (pl.BlockSpec((tm, D_MODEL), row), pl.BlockSpec((tm, D_MODEL), row))
    in_specs.append(pl.BlockSpec((1, D_MODEL), fixed))
    args.append(gain)
    return pl.pallas_call(
        kern,
        out_shape=out_shape,
        grid_spec=pltpu.PrefetchScalarGridSpec(
            num_scalar_prefetch=1,
            grid=(N_TOK // tm,),
            in_specs=in_specs,
            out_specs=out_specs,
            scratch_shapes=[pltpu.VMEM((2, tm, D_MODEL), F32), pltpu.SemaphoreType.DMA((2,))],
        ),
        compiler_params=_params(40),
        name="pre_final" if final else "pre_combine",
    )(pos, *args)


def _in_proj_kernel(h_ref, w_ref, o_ref):
    o_ref[...] = jnp.dot(h_ref[...], w_ref[...].astype(BF16), preferred_element_type=F32)


def _in_proj(h, w_in, layer):
    tm, tn = INPROJ_TM, INPROJ_TN
    return pl.pallas_call(
        _in_proj_kernel,
        out_shape=jax.ShapeDtypeStruct((N_TOK, IN_COLS), F32),
        grid=(N_TOK // tm, IN_COLS // tn),
        in_specs=[
            pl.BlockSpec((tm, D_MODEL), lambda i, j: (i, 0)),
            pl.BlockSpec((None, D_MODEL, tn), lambda i, j: (layer, 0, j)),
        ],
        out_specs=pl.BlockSpec((tm, tn), lambda i, j: (i, j)),
        compiler_params=_params(48),
        name="in_proj",
    )(h, w_in)


def _lambda(lam_ref, layer):
    lp = lam_ref[...]
    a = jnp.sum(lp[0:1, :] * lp[1:2, :], axis=-1, keepdims=True)
    b = jnp.sum(lp[2:3, :] * lp[3:4, :], axis=-1, keepdims=True)
    return jnp.exp(a) - jnp.exp(b) + _lam_init(layer)


def _split_q(q):
    lane = lax.broadcasted_iota(jnp.int32, q.shape, 1)
    q1 = jnp.where(lane < DIFF_QK_DIM, q, 0.0).astype(BF16)
    q2 = jnp.where(lane >= DIFF_QK_DIM, q, 0.0).astype(BF16)
    return q1, q2


def _sub_ln(o, gain, layer):
    return _rms(o, gain) * (1.0 - _lam_init(layer))


def _attn_prompt_kernel(nq_ref, nk_ref, nv_ref, dq_ref, dk_ref, dv_ref, lam_ref, sg_ref,
                        c0_in, c1_in, c2_in, c3_in,
                        on_ref, od_ref, onk_ref, onv_ref, odk_ref, odv_ref, *, layer):
    del c0_in, c1_in, c2_in, c3_in
    nk = nk_ref[...]
    nv = nv_ref[...]
    dk = dk_ref[...]
    dv = dv_ref[...]
    onk_ref[...] = nk
    onv_ref[...] = nv
    odk_ref[...] = dk
    odv_ref[...] = dv

    q = (nq_ref[...] * (HEAD_DIM ** -0.5)).astype(BF16)
    s = lax.dot_general(q, nk.astype(BF16), NT_DIMS, preferred_element_type=F32)
    e = jnp.exp(s - jnp.max(s, axis=-1, keepdims=True))
    z = jnp.sum(e, axis=-1, keepdims=True)
    o = jnp.dot(e.astype(BF16), nv.astype(BF16), preferred_element_type=F32) / z
    on_ref[...] = o.astype(BF16)

    lam = _lambda(lam_ref, layer)
    q1, q2 = _split_q(dq_ref[...] * (DIFF_QK_DIM ** -0.5))
    kb = dk.astype(BF16)
    s1 = lax.dot_general(q1, kb, NT_DIMS, preferred_element_type=F32)
    s2 = lax.dot_general(q2, kb, NT_DIMS, preferred_element_type=F32)
    e1 = jnp.exp(s1 - jnp.max(s1, axis=-1, keepdims=True))
    e2 = jnp.exp(s2 - jnp.max(s2, axis=-1, keepdims=True))
    a1 = 1.0 / jnp.sum(e1, axis=-1, keepdims=True)
    a2 = lam / jnp.sum(e2, axis=-1, keepdims=True)
    p = e1 * a1 - e2 * a2
    o = jnp.dot(p.astype(BF16), dv.astype(BF16), preferred_element_type=F32)
    od_ref[...] = _sub_ln(o, sg_ref[...], layer).astype(BF16)


def _attn_prompt(qkv, lam_p, subln, caches, layer):
    h8 = N_HEADS
    blk = lambda off: pl.BlockSpec((SEQ, HEAD_DIM), lambda b, h: (b, off + h))
    in_specs = [blk(0), blk(h8), blk(2 * h8), blk(3 * h8), blk(4 * h8), blk(5 * h8),
                pl.BlockSpec((None, 4, DIFF_QK_DIM), lambda b, h: (layer, 0, 0)),
                pl.BlockSpec((None, 1, HEAD_DIM), lambda b, h: (layer, 0, 0))]
    args = [qkv] * 6 + [lam_p, subln]
    aliases = {}
    for n, cache in enumerate(caches):
        aliases[len(args)] = 2 + n
        in_specs.append(pl.BlockSpec(memory_space=pl.ANY))
        args.append(cache)
    cache_shape = jax.ShapeDtypeStruct((BATCH, DEPTH, N_HEADS, SEQ, HEAD_DIM), F32)
    cache_spec = pl.BlockSpec((None, None, None, SEQ, HEAD_DIM), lambda b, h: (b, layer, h, 0, 0))
    out = pl.pallas_call(
        functools.partial(_attn_prompt_kernel, layer=layer),
        out_shape=(jax.ShapeDtypeStruct((N_PROMPT, h8 * HEAD_DIM), BF16),
                   jax.ShapeDtypeStruct((N_PROMPT, h8 * HEAD_DIM), BF16),
                   cache_shape, cache_shape, cache_shape, cache_shape),
        grid=(BATCH, h8),
        in_specs=in_specs,
        out_specs=(pl.BlockSpec((SEQ, HEAD_DIM), lambda b, h: (b, h)),
                   pl.BlockSpec((SEQ, HEAD_DIM), lambda b, h: (b, h)),
                   cache_spec, cache_spec, cache_spec, cache_spec),
        input_output_aliases=aliases,
        compiler_params=_params(32),
        name="attn_prompt",
    )(*args)
    return out[0], out[1], out[2:]


def _attn_latent_kernel(nq_ref, nk_ref, nv_ref, dq_ref, dk_ref, dv_ref,
                        cnk_ref, cnv_ref, cdk_ref, cdv_ref, tab_ref, cos_ref, sin_ref,
                        lam_ref, sg_ref, on_ref, od_ref,
                        kb, vb, ckb, cvb, q1b, q2b, *, layer):
    kb[...] = nk_ref[...].astype(BF16)
    vb[...] = nv_ref[...].astype(BF16)
    ckb[...] = cnk_ref[...].astype(BF16)
    cvb[...] = cnv_ref[...].astype(BF16)
    win = WIN_H * GRID_W

    def nat_row(r, carry):
        rs = jnp.clip(r - WIN_H // 2, 0, GRID_H - WIN_H)
        d0 = rs - r + WIN_H - 1
        q0 = pl.multiple_of(r * GRID_W, GRID_W)
        k0 = pl.multiple_of(rs * GRID_W, GRID_W)
        q = (nq_ref[pl.ds(q0, GRID_W), :] * (HEAD_DIM ** -0.5)).astype(BF16)
        s_nb = lax.dot_general(q, kb[pl.ds(k0, win), :], NT_DIMS, preferred_element_type=F32)
        bias = jnp.concatenate([tab_ref[d0 + 2 * i] for i in range(WIN_H // 2)], axis=1)
        s_nb = s_nb + bias
        s_cx = lax.dot_general(q, ckb[...], NT_DIMS, preferred_element_type=F32)
        m = jnp.maximum(jnp.max(s_nb, axis=-1, keepdims=True),
                        jnp.max(s_cx, axis=-1, keepdims=True))
        e_nb = jnp.exp(s_nb - m)
        e_cx = jnp.exp(s_cx - m)
        z = jnp.sum(e_nb, axis=-1, keepdims=True) + jnp.sum(e_cx, axis=-1, keepdims=True)
        o = (jnp.dot(e_nb.astype(BF16), vb[pl.ds(k0, win), :], preferred_element_type=F32)
             + jnp.dot(e_cx.astype(BF16), cvb[...], preferred_element_type=F32))
        on_ref[pl.ds(q0, GRID_W), :] = (o / z).astype(BF16)
        return carry

    lax.fori_loop(0, GRID_H, nat_row, 0)

    lane = lax.broadcasted_iota(jnp.int32, (DEC_SEQ, HEAD_DIM), 1)
    first_half = (lane & 16) == 0

    def rope(x):
        rot = jnp.where(first_half, pltpu.roll(x, HEAD_DIM - 16, 1), pltpu.roll(x, 16, 1))
        return x * cos_ref[...] + rot * sin_ref[...]

    q1, q2 = _split_q(rope(dq_ref[...]) * (DIFF_QK_DIM ** -0.5))
    q1b[...] = q1
    q2b[...] = q2
    kb[...] = rope(dk_ref[...]).astype(BF16)
    vb[...] = dv_ref[...].astype(BF16)
    ckb[...] = cdk_ref[...].astype(BF16)
    cvb[...] = cdv_ref[...].astype(BF16)
    lam = _lambda(lam_ref, layer)

    def diff_block(j, carry):
        r0 = pl.multiple_of(j * Q_BLOCK, Q_BLOCK)
        qa = q1b[pl.ds(r0, Q_BLOCK), :]
        qb = q2b[pl.ds(r0, Q_BLOCK), :]
        s1l = lax.dot_general(qa, kb[...], NT_DIMS, preferred_element_type=F32)
        s1c = lax.dot_general(qa, ckb[...], NT_DIMS, preferred_element_type=F32)
        s2l = lax.dot_general(qb, kb[...], NT_DIMS, preferred_element_type=F32)
        s2c = lax.dot_general(qb, ckb[...], NT_DIMS, preferred_element_type=F32)
        m1 = jnp.maximum(jnp.max(s1l, axis=-1, keepdims=True), jnp.max(s1c, axis=-1, keepdims=True))
        m2 = jnp.maximum(jnp.max(s2l, axis=-1, keepdims=True), jnp.max(s2c, axis=-1, keepdims=True))
        e1l = jnp.exp(s1l - m1)
        e1c = jnp.exp(s1c - m1)
        e2l = jnp.exp(s2l - m2)
        e2c = jnp.exp(s2c - m2)
        a1 = 1.0 / (jnp.sum(e1l, axis=-1, keepdims=True) + jnp.sum(e1c, axis=-1, keepdims=True))
        a2 = lam / (jnp.sum(e2l, axis=-1, keepdims=True) + jnp.sum(e2c, axis=-1, keepdims=True))
        p_l = (e1l * a1 - e2l * a2).astype(BF16)
        p_c = (e1c * a1 - e2c * a2).astype(BF16)
        o = (jnp.dot(p_l, vb[...], preferred_element_type=F32)
             + jnp.dot(p_c, cvb[...], preferred_element_type=F32))
        od_ref[pl.ds(r0, Q_BLOCK), :] = _sub_ln(o, sg_ref[...], layer).astype(BF16)
        return carry

    lax.fori_loop(0, DEC_SEQ // Q_BLOCK, diff_block, 0)


def _attn_latent(qkv, caches, tab, cos, sin, lam_p, subln, layer):
    h8 = N_HEADS
    row0 = N_PROMPT // DEC_SEQ
    blk = lambda off: pl.BlockSpec((DEC_SEQ, HEAD_DIM), lambda b, h: (row0 + b, off + h))
    cblk = pl.BlockSpec((None, None, None, PAST_LEN, HEAD_DIM), lambda b, h: (b, layer, h, 0, 0))
    fixed = pl.BlockSpec((DEC_SEQ, HEAD_DIM), lambda b, h: (0, 0))
    out_spec = pl.BlockSpec((DEC_SEQ, HEAD_DIM), lambda b, h: (b, h))
    merged_shape = jax.ShapeDtypeStruct((N_TOK - N_PROMPT, h8 * HEAD_DIM), BF16)
    return pl.pallas_call(
        functools.partial(_attn_latent_kernel, layer=layer),
        out_shape=(merged_shape, merged_shape),
        grid=(DEC_BATCH, h8),
        in_specs=[blk(0), blk(h8), blk(2 * h8), blk(3 * h8), blk(4 * h8), blk(5 * h8),
                  cblk, cblk, cblk, cblk,
                  pl.BlockSpec((None, 2 * WIN_H - 2, GRID_W, 2 * GRID_W), lambda b, h: (h, 0, 0, 0)),
                  fixed, fixed,
                  pl.BlockSpec((None, 4, DIFF_QK_DIM), lambda b, h: (layer, 0, 0)),
                  pl.BlockSpec((None, 1, HEAD_DIM), lambda b, h: (layer, 0, 0))],
        out_specs=(out_spec, out_spec),
        scratch_shapes=[pltpu.VMEM((DEC_SEQ, HEAD_DIM), BF16), pltpu.VMEM((DEC_SEQ, HEAD_DIM), BF16),
                        pltpu.VMEM((PAST_LEN, HEAD_DIM), BF16), pltpu.VMEM((PAST_LEN, HEAD_DIM), BF16),
                        pltpu.VMEM((DEC_SEQ, HEAD_DIM), BF16), pltpu.VMEM((DEC_SEQ, HEAD_DIM), BF16)],
        compiler_params=_params(48),
        name="attn_latent",
    )(qkv, qkv, qkv, qkv, qkv, qkv, *caches, tab, cos, sin, lam_p, subln)


def _route(logits, bias):
    ex = jnp.exp(logits - jnp.max(logits, axis=0, keepdims=True))
    probs = ex / jnp.sum(ex, axis=0, keepdims=True)
    sel = probs + bias
    srow = [sel[e:e + 1, :] for e in range(N_EXPERTS)]
    prow = [probs[e:e + 1, :] for e in range(N_EXPERTS)]
    gscore = []
    for g in range(N_GROUPS):
        v = srow[g * EXPERTS_PER_GROUP:(g + 1) * EXPERTS_PER_GROUP]
        best = None
        for a in range(EXPERTS_PER_GROUP):
            for b in range(a + 1, EXPERTS_PER_GROUP):
                pair = v[a] + v[b]
                best = pair if best is None else jnp.maximum(best, pair)
        gscore.append(best)
    gbest = gscore[0]
    gidx = jnp.zeros(gbest.shape, jnp.int32)
    for g in range(1, N_GROUPS):
        better = gscore[g] > gbest
        gidx = jnp.where(better, g, gidx)
        gbest = jnp.where(better, gscore[g], gbest)
    neg = jnp.full(gbest.shape, -jnp.inf, F32)
    picks = []
    taken = None
    for _ in range(2):
        best = neg
        idx = jnp.zeros(gbest.shape, jnp.int32)
        wgt = jnp.zeros(gbest.shape, F32)
        for e in range(N_EXPERTS):
            ok = gidx == (e // EXPERTS_PER_GROUP)
            if taken is not None:
                ok = jnp.logical_and(ok, taken != e)
            cand = jnp.where(ok, srow[e], neg)
            better = cand > best
            idx = jnp.where(better, e, idx)
            wgt = jnp.where(better, prow[e], wgt)
            best = jnp.where(better, cand, best)
        picks.append((idx, wgt))
        taken = idx
    (i0, w0), (i1, w1) = picks
    wsum = w0 + w1
    return i0, i1, w0 / wsum, w1 / wsum


def _split_bf16(v):
    hi = v.astype(BF16)
    lo = (v - hi.astype(F32)).astype(BF16)
    return hi, lo


def _load_resident_weight(w_hbm, wbf, stage, sem, n_chunks):
    ck = wbf.shape[0] // n_chunks
    copies = [pltpu.make_async_copy(w_hbm.at[pl.ds(c * ck, ck), :], stage.at[c % 2], sem.at[c % 2])
              for c in range(n_chunks)]
    copies[0].start()
    for c in range(n_chunks):
        if c + 1 < n_chunks:
            copies[c + 1].start()
        copies[c].wait()
        wbf[c * ck:(c + 1) * ck, :] = stage[c % 2].astype(BF16)


def _out_proj_kernel(pn_ref, pd_ref, sn_ref, sd_ref, w_hbm, x_ref, mod_ref, g_ref, wr_ref, rb_ref,
                     xo_ref, h_ref, idx_ref, cw_ref, wbf, stage, mbuf, sem, *, layer):
    i = pl.program_id(0)
    half = D_MODEL // 2

    @pl.when(i == 0)
    def _():
        _load_resident_weight(w_hbm.at[layer], wbf, stage, sem, OUTPROJ_WCHUNKS)

    prompt = i < N_PROMPT // OUTPROJ_TM

    @pl.when(prompt)
    def _():
        mbuf[:, :half] = pn_ref[...]
        mbuf[:, half:] = pd_ref[...]

    @pl.when(jnp.logical_not(prompt))
    def _():
        mbuf[:, :half] = sn_ref[...]
        mbuf[:, half:] = sd_ref[...]

    a = jnp.dot(mbuf[...], wbf[...], preferred_element_type=F32)
    x = x_ref[...] + mod_ref[2:3, :] * a
    xo_ref[...] = x
    h = _rms(x, g_ref[...]) * (1.0 + mod_ref[4:5, :]) + mod_ref[3:4, :]
    h_ref[...] = h
    w_hi, w_lo = _split_bf16(wr_ref[...])
    h_hi, h_lo = _split_bf16(h)
    part = lax.dot_general(jnp.concatenate([w_hi, w_lo], axis=0), h_hi, NT_DIMS,
                           preferred_element_type=F32)
    logits = (part[:N_EXPERTS] + part[N_EXPERTS:]
              + lax.dot_general(w_hi, h_lo, NT_DIMS, preferred_element_type=F32))
    i0, i1, w0, w1 = _route(logits, rb_ref[...])
    idx_ref[...] = jnp.concatenate([i0, i1], axis=0)
    cw_ref[...] = jnp.concatenate([w0, w1], axis=0)


def _out_proj(merged, w_out, x, mods, layer, gain, w_router_t, router_bias):
    tm = OUTPROJ_TM
    half = D_MODEL // 2
    npt = N_PROMPT // tm
    row = lambda i: (i, 0)
    fixed = lambda i: (0, 0)
    p_row = lambda i: (jnp.minimum(i, npt - 1), 0)
    s_row = lambda i: (jnp.maximum(i - npt, 0), 0)
    return pl.pallas_call(
        functools.partial(_out_proj_kernel, layer=layer),
        out_shape=(jax.ShapeDtypeStruct((N_TOK, D_MODEL), F32),
                   jax.ShapeDtypeStruct((N_TOK, D_MODEL), F32),
                   jax.ShapeDtypeStruct((2, N_TOK), jnp.int32),
                   jax.ShapeDtypeStruct((2, N_TOK), F32)),
        grid=(N_TOK // tm,),
        in_specs=[
            pl.BlockSpec((tm, half), p_row),
            pl.BlockSpec((tm, half), p_row),
            pl.BlockSpec((tm, half), s_row),
            pl.BlockSpec((tm, half), s_row),
            pl.BlockSpec(memory_space=pl.ANY),
            pl.BlockSpec((tm, D_MODEL), row),
            pl.BlockSpec((None, None, 6, D_MODEL), lambda i: (layer, _mod_row(i, tm), 0, 0)),
            pl.BlockSpec((1, D_MODEL), fixed),
            pl.BlockSpec((N_EXPERTS, D_MODEL), fixed),
            pl.BlockSpec((N_EXPERTS, 1), fixed),
        ],
        out_specs=(pl.BlockSpec((tm, D_MODEL), row), pl.BlockSpec((tm, D_MODEL), row),
                   pl.BlockSpec((2, tm), lambda i: (0, i)),
                   pl.BlockSpec((2, tm), lambda i: (0, i))),
        scratch_shapes=[pltpu.VMEM((D_MODEL, D_MODEL), BF16),
                        pltpu.VMEM((2, D_MODEL // OUTPROJ_WCHUNKS, D_MODEL), F32),
                        pltpu.VMEM((tm, D_MODEL), BF16),
                        pltpu.SemaphoreType.DMA((2,))],
        compiler_params=_params(48),
        name="out_proj_router",
    )(*merged, w_out, x, mods, gain, w_router_t, router_bias)


def _dispatch_kernel(pos_ref, h_ref, xs_in, xs_out, sem):
    del xs_in
    tm = DISPATCH_TM
    base = pl.program_id(0) * tm

    def issue(r, carry):
        for k in range(2):
            p = pos_ref[k * N_TOK + base + r]
            pltpu.make_async_copy(h_ref.at[pl.ds(r, 1), :], xs_out.at[pl.ds(p, 1), :], sem).start()
        return carry

    lax.fori_loop(0, tm, issue, 0, unroll=8)
    for _ in range(2):
        pltpu.make_async_copy(h_ref, xs_out.at[pl.ds(0, tm), :], sem).wait()


def _dispatch(pos, h2, xs_buf):
    tm = DISPATCH_TM
    return pl.pallas_call(
        _dispatch_kernel,
        out_shape=jax.ShapeDtypeStruct((EXPERT_CAP, D_MODEL), F32),
        grid_spec=pltpu.PrefetchScalarGridSpec(
            num_scalar_prefetch=1,
            grid=(N_TOK // tm,),
            in_specs=[pl.BlockSpec((tm, D_MODEL), lambda i, pos_ref: (i, 0)),
                      pl.BlockSpec(memory_space=pl.ANY)],
            out_specs=pl.BlockSpec(memory_space=pl.ANY),
            scratch_shapes=[pltpu.SemaphoreType.DMA(())],
        ),
        input_output_aliases={2: 0},
        compiler_params=_params(32),
        name="expert_dispatch",
    )(pos, h2, xs_buf)


def _experts_kernel(we_ref, wc_ref, wdo_ref, wsl_ref, ti_ref, tdo_ref, tsl_ref,
                    xs_ref, wg_ref, wu_ref, wd_ref, o_ref, wg_bf, wu_bf, wd_bf):
    del we_ref, ti_ref
    s = pl.program_id(0)

    @pl.when(wdo_ref[s] == 1)
    def _():
        slot = wsl_ref[s]
        c = wc_ref[s]
        ck = D_MODEL // EXPERT_WCHUNKS
        r0 = pl.multiple_of(c * ck, ck)
        wg_bf[slot, pl.ds(r0, ck), :] = wg_ref[...].astype(BF16)
        wu_bf[slot, pl.ds(r0, ck), :] = wu_ref[...].astype(BF16)
        cf = D_FF // EXPERT_WCHUNKS
        r1 = pl.multiple_of(c * cf, cf)
        wd_bf[slot, pl.ds(r1, cf), :] = wd_ref[...].astype(BF16)

    @pl.when(tdo_ref[s] == 1)
    def _():
        slot = tsl_ref[s]
        xb = xs_ref[...].astype(BF16)
        g = jnp.dot(xb, wg_bf[slot], preferred_element_type=F32)
        u = jnp.dot(xb, wu_bf[slot], preferred_element_type=F32)
        hid = (g * jax.nn.sigmoid(g) * u).astype(BF16)
        o_ref[...] = jnp.dot(hid, wd_bf[slot], preferred_element_type=F32)

    @pl.when(tdo_ref[s] == 2)
    def _():
        o_ref[...] = jnp.zeros_like(o_ref)


def _experts(plan, xs, w_gate, w_up, w_down, layer):
    tm = EXPERT_TM
    ck = D_MODEL // EXPERT_WCHUNKS
    cf = D_FF // EXPERT_WCHUNKS
    tile = lambda s, we, wc, wdo, wsl, ti, tdo, tsl: (ti[s], 0)
    wblk = lambda s, we, wc, wdo, wsl, ti, tdo, tsl: (layer, we[s], wc[s], 0)
    return pl.pallas_call(
        _experts_kernel,
        out_shape=jax.ShapeDtypeStruct((EXPERT_CAP, D_MODEL), F32),
        grid_spec=pltpu.PrefetchScalarGridSpec(
            num_scalar_prefetch=7,
            grid=(EXPERT_STEPS,),
            in_specs=[
                pl.BlockSpec((tm, D_MODEL), tile),
                pl.BlockSpec((None, None, ck, D_FF), wblk),
                pl.BlockSpec((None, None, ck, D_FF), wblk),
                pl.BlockSpec((None, None, cf, D_MODEL), wblk),
            ],
            out_specs=pl.BlockSpec((tm, D_MODEL), tile),
            scratch_shapes=[pltpu.VMEM((2, D_MODEL, D_FF), BF16),
                            pltpu.VMEM((2, D_MODEL, D_FF), BF16),
                            pltpu.VMEM((2, D_FF, D_MODEL), BF16)],
        ),
        compiler_params=_params(56),
        name="expert_mlp",
    )(*plan, xs, w_gate, w_up, w_down)


def _dispatch_plan(idx):
    e = idx.reshape(-1)
    onehot = (e[:, None] == jnp.arange(N_EXPERTS, dtype=jnp.int32)[None, :]).astype(jnp.int32)
    csum = jnp.cumsum(onehot, axis=0)
    counts = csum[-1]
    rank = jnp.sum(onehot * csum, axis=1) - 1
    nt = (counts + EXPERT_TM - 1) // EXPERT_TM
    tile_end = jnp.cumsum(nt)
    tile_start = tile_end - nt
    pos = jnp.sum(onehot * (tile_start * EXPERT_TM)[None, :], axis=1) + rank

    nw = EXPERT_WCHUNKS
    length = jnp.maximum(nt, nw)
    phase_end = nw + jnp.cumsum(length)
    phase_start = phase_end - length
    zero = jnp.zeros((1,), jnp.int32)
    s = jnp.arange(EXPERT_STEPS, dtype=jnp.int32)
    ph = jnp.sum((s[:, None] >= phase_start[None, :]).astype(jnp.int32), axis=1)
    ex = ph - 1
    k = s - jnp.concatenate([zero, phase_start])[ph]
    nt_s = jnp.concatenate([zero, nt])[ph]
    ts_s = jnp.concatenate([zero, tile_start])[ph]
    live = s < phase_end[-1]
    last = N_EXPERTS - 1
    w_do = (live & (k < nw) & (ex < last)).astype(jnp.int32)
    w_e = jnp.minimum(ex + 1, last)
    w_c = jnp.where(ex < last, jnp.minimum(k, nw - 1), nw - 1)
    w_slot = (ex + 1) % 2
    t_do = (live & (k < nt_s)).astype(jnp.int32)
    t_idx = jnp.maximum(ts_s + jnp.minimum(k + 1, nt_s) - 1, 0)
    z = tile_end[-1] + (s - phase_end[-1])
    t_do = jnp.where(jnp.logical_not(live) & (z < EXPERT_TILES), 2, t_do)
    t_idx = jnp.where(live, t_idx, jnp.minimum(z, EXPERT_TILES - 1))
    t_slot = jnp.maximum(ex, 0) % 2
    plan = tuple(a.astype(jnp.int32) for a in (w_e, w_c, w_do, w_slot, t_idx, t_do, t_slot))
    return pos.astype(jnp.int32), plan


def _bias_tables(nat_rpb):
    c = jnp.arange(GRID_W)[:, None]
    kc = jnp.arange(GRID_W)[None, :]
    qstart = jnp.clip(c - WIN_W // 2, 0, GRID_W - WIN_W)
    valid = (kc >= qstart) & (kc < qstart + WIN_W)
    dc = jnp.clip(kc - c, -(WIN_W - 1), WIN_W - 1) + WIN_W - 1
    tab = jnp.where(valid, nat_rpb[:, :, :, dc], -1e30)
    return jnp.concatenate([tab[:, :, :-1], tab[:, :, 1:]], axis=-1)


def _rope_tables():
    t = jnp.arange(DEC_SEQ)
    row = (t // GRID_W).astype(F32)
    col = (t % GRID_W).astype(F32)
    nf = DIFF_QK_DIM // 4
    inv = ROPE_BASE ** (-jnp.arange(nf, dtype=F32) / nf)
    ar = row[:, None] * inv[None, :]
    ac = col[:, None] * inv[None, :]
    cos = jnp.concatenate([jnp.cos(ar), jnp.cos(ar), jnp.cos(ac), jnp.cos(ac)], axis=-1)
    sin = jnp.concatenate([-jnp.sin(ar), jnp.sin(ar), -jnp.sin(ac), jnp.sin(ac)], axis=-1)
    return jnp.tile(cos, (1, 2)), jnp.tile(sin, (1, 2))


def kernel(x_prompt, x_sample, cache_nat_k, cache_nat_v, cache_diff_k, cache_diff_v, c, c_ctx,
           w_ada, b_ada, norm1, norm2, norm_final, w_in, w_out, nat_rpb, diff_lambda, diff_subln,
           w_router, router_bias, w_gate, w_up, w_down):
    x = jnp.concatenate([x_prompt.reshape(N_PROMPT, D_MODEL),
                         x_sample.reshape(DEC_BATCH * DEC_SEQ, D_MODEL)], axis=0)
    cond = jnp.concatenate([c_ctx[None, :], c, jnp.zeros((8 - 1 - DEC_BATCH, D_MODEL), F32)], axis=0)
    mods = _modulation(cond, w_ada, b_ada)
    tabs = _bias_tables(nat_rpb)
    cos, sin = _rope_tables()
    subln = diff_subln.reshape(DEPTH, 1, HEAD_DIM)
    w_router_t = w_router.T
    rbias = router_bias.reshape(N_EXPERTS, 1)
    lat_caches = (cache_nat_k, cache_nat_v, cache_diff_k, cache_diff_v)

    new_caches = tuple(jnp.zeros((BATCH, DEPTH, N_HEADS, SEQ, HEAD_DIM), F32) for _ in range(4))
    xs_buf = jnp.zeros((EXPERT_CAP, D_MODEL), F32)
    ys = pos = cw = None
    for layer in range(DEPTH):
        gain1 = norm1[layer].reshape(1, D_MODEL)
        if layer == 0:
            h = _pre_first(x, mods, layer, gain1)
        else:
            x, h = _pre_combine(x, ys, pos, cw, mods, layer - 1, gain1, final=False)
        qkv = _in_proj(h, w_in, layer)
        pn, pd, new_caches = _attn_prompt(qkv, diff_lambda, subln, new_caches, layer)
        sn, sd = _attn_latent(qkv, lat_caches, tabs[layer], cos, sin, diff_lambda, subln, layer)
        x, h2, idx, cwt = _out_proj((pn, pd, sn, sd), w_out, x, mods, layer,
                                    norm2[layer].reshape(1, D_MODEL), w_router_t, rbias)
        pos, plan = _dispatch_plan(idx)
        cw = cwt.T
        xs_buf = _dispatch(pos, h2, xs_buf)
        ys = _experts(plan, xs_buf, w_gate, w_up, w_down, layer)
    y = _pre_combine(x, ys, pos, cw, mods, DEPTH - 1, norm_final.reshape(1, D_MODEL), final=True)
    y_prompt = y[:N_PROMPT].reshape(BATCH, SEQ, D_MODEL)
    y_sample = y[N_PROMPT:].reshape(DEC_BATCH, DEC_SEQ, D_MODEL)
    return (y_prompt, y_sample) + tuple(new_caches)
```

```python
import functools
import math

import jax
import jax.numpy as jnp
from jax import lax
from jax.experimental import pallas as pl
from jax.experimental.pallas import tpu as pltpu

D_MODEL = 2048
BATCH = 16
SEQ = 256
DEPTH = 4
DEC_BATCH = 4
DEC_SEQ = 1024
PAST_LEN = 512
GRID_W = 64
GRID_H = DEC_SEQ // GRID_W
N_HEADS = 8
HEAD_DIM = 128
DIFF_QK_DIM = 64
WIN_H = 8
WIN_W = 16
N_EXPERTS = 16
N_GROUPS = 4
EXPERTS_PER_GROUP = N_EXPERTS // N_GROUPS
D_FF = 1024
ROPE_BASE = 10000.0
EPS = 1e-6
IN_COLS = 6 * N_HEADS * HEAD_DIM
N_PROMPT = BATCH * SEQ
N_TOK = N_PROMPT + DEC_BATCH * DEC_SEQ
N_PAIRS = 2 * N_TOK

PRE_TM = 256
INPROJ_TM = 2048
INPROJ_TN = 512
OUTPROJ_TM = 256
OUTPROJ_WCHUNKS = 4
DISPATCH_TM = 256
EXPERT_TM = 256
EXPERT_CAP = N_PAIRS + N_EXPERTS * EXPERT_TM
EXPERT_TILES = EXPERT_CAP // EXPERT_TM
EXPERT_WCHUNKS = 4
EXPERT_STEPS = EXPERT_WCHUNKS * (N_EXPERTS + 1) + EXPERT_TILES
MOD_TN = 1024
Q_BLOCK = 256
PROMPT_HEADS = 4
NAT_QROWS = 4
NAT_KROWS = 12
LOG2E = 1.4426950408889634
MASK_VALUE = -1e30

F32 = jnp.float32
BF16 = jnp.bfloat16
MIB = 1024 * 1024
NT_DIMS = (((1,), (1,)), ((), ()))


def _params(vmem_mib):
    return pltpu.CompilerParams(vmem_limit_bytes=vmem_mib * MIB)


def _lam_init(layer):
    return 0.8 - 0.6 * math.exp(-0.3 * layer)


def _rms(x, gain):
    return x * lax.rsqrt(jnp.mean(x * x, axis=-1, keepdims=True) + EPS) * gain


def _mod_row(i, tm):
    first = i * tm
    return jnp.where(first < N_PROMPT, 0, 1 + (first - N_PROMPT) // DEC_SEQ)


def _mod_kernel(cond_ref, w_ref, b_ref, o_ref):
    c = cond_ref[...]
    s = c * jax.nn.sigmoid(c)
    o_ref[...] = jnp.dot(s.astype(BF16), w_ref[...].astype(BF16),
                         preferred_element_type=F32) + b_ref[...]


def _modulation(cond, w_ada, b_ada):
    n = 6 * D_MODEL
    out = pl.pallas_call(
        _mod_kernel,
        out_shape=jax.ShapeDtypeStruct((DEPTH, 8, n), F32),
        grid=(DEPTH, n // MOD_TN),
        in_specs=[
            pl.BlockSpec((8, D_MODEL), lambda l, j: (0, 0)),
            pl.BlockSpec((None, D_MODEL, MOD_TN), lambda l, j: (l, 0, j)),
            pl.BlockSpec((None, 1, MOD_TN), lambda l, j: (l, 0, j)),
        ],
        out_specs=pl.BlockSpec((None, 8, MOD_TN), lambda l, j: (l, 0, j)),
        compiler_params=_params(40),
        name="adaln_modulation",
    )(cond, w_ada, b_ada.reshape(DEPTH, 1, n))
    return out.reshape(DEPTH, 8, 6, D_MODEL)


def _gather_expert_rows(pos_ref, ys_hbm, ybuf, sem, tm):
    i = pl.program_id(0)
    n = pl.num_programs(0)

    def issue_tile(tile, slot):
        def issue(r, carry):
            for k in range(2):
                p = pos_ref[k * N_TOK + tile * tm + r]
                pltpu.make_async_copy(ys_hbm.at[pl.ds(p, 1), :],
                                      ybuf.at[slot, k, pl.ds(r, 1), :], sem.at[slot, k]).start()
            return carry

        lax.fori_loop(0, tm, issue, 0, unroll=8)

    @pl.when(i == 0)
    def _():
        issue_tile(0, 0)

    @pl.when(i + 1 < n)
    def _():
        issue_tile(i + 1, (i + 1) % 2)

    slot = i % 2
    for k in range(2):
        pltpu.make_async_copy(ys_hbm.at[pl.ds(0, tm), :], ybuf.at[slot, k], sem.at[slot, k]).wait()
    return slot


def _pre_first_kernel(x_ref, mod_ref, g_ref, h_ref):
    x = x_ref[...]
    h = _rms(x, g_ref[...]) * (1.0 + mod_ref[1:2, :]) + mod_ref[0:1, :]
    h_ref[...] = h.astype(BF16)


def _pre_mid_kernel(pos_ref, x_ref, ys_hbm, cw_ref, modp_ref, mod_ref, g_ref,
                    xo_ref, h_ref, ybuf, sem):
    slot = _gather_expert_rows(pos_ref, ys_hbm, ybuf, sem, PRE_TM)
    cw = cw_ref[...]
    y = cw[:, 0:1] * ybuf[slot, 0] + cw[:, 1:2] * ybuf[slot, 1]
    x = x_ref[...] + modp_ref[5:6, :] * y
    xo_ref[...] = x
    h = _rms(x, g_ref[...]) * (1.0 + mod_ref[1:2, :]) + mod_ref[0:1, :]
    h_ref[...] = h.astype(BF16)


def _pre_final_kernel(pos_ref, x_ref, ys_hbm, cw_ref, modp_ref, g_ref, y_ref, ybuf, sem):
    slot = _gather_expert_rows(pos_ref, ys_hbm, ybuf, sem, PRE_TM)
    cw = cw_ref[...]
    y = cw[:, 0:1] * ybuf[slot, 0] + cw[:, 1:2] * ybuf[slot, 1]
    x = x_ref[...] + modp_ref[5:6, :] * y
    y_ref[...] = _rms(x, g_ref[...])


def _pre_first(x, mods, layer, gain):
    tm = PRE_TM
    return pl.pallas_call(
        _pre_first_kernel,
        out_shape=jax.ShapeDtypeStruct((N_TOK, D_MODEL), BF16),
        grid=(N_TOK // tm,),
        in_specs=[
            pl.BlockSpec((tm, D_MODEL), lambda i: (i, 0)),
            pl.BlockSpec((None, None, 6, D_MODEL), lambda i: (layer, _mod_row(i, tm), 0, 0)),
            pl.BlockSpec((1, D_MODEL), lambda i: (0, 0)),
        ],
        out_specs=pl.BlockSpec((tm, D_MODEL), lambda i: (i, 0)),
        compiler_params=_params(32),
        name="pre_first",
    )(x, mods, gain)


def _pre_combine(x, ys, pos, cw, mods, prev_layer, gain, final):
    tm = PRE_TM
    row = lambda i, pos_ref: (i, 0)
    fixed = lambda i, pos_ref: (0, 0)
    in_specs = [
        pl.BlockSpec((tm, D_MODEL), row),
        pl.BlockSpec(memory_space=pl.ANY),
        pl.BlockSpec((tm, 2), row),
        pl.BlockSpec((None, None, 6, D_MODEL),
                     lambda i, pos_ref: (prev_layer, _mod_row(i, tm), 0, 0)),
    ]
    args = [x, ys, cw, mods]
    if final:
        kern = _pre_final_kernel
        out_shape = jax.ShapeDtypeStruct((N_TOK, D_MODEL), F32)
        out_specs = pl.BlockSpec((tm, D_MODEL), row)
    else:
        kern = _pre_mid_kernel
        in_specs.append(pl.BlockSpec((None, None, 6, D_MODEL),
                                     lambda i, pos_ref: (prev_layer + 1, _mod_row(i, tm), 0, 0)))
        args.append(mods)
        out_shape = (jax.ShapeDtypeStruct((N_TOK, D_MODEL), F32),
                     jax.ShapeDtypeStruct((N_TOK, D_MODEL), BF16))
        out_specs = (pl.BlockSpec((tm, D_MODEL), row), pl.BlockSpec((tm, D_MODEL), row))
    in_specs.append(pl.BlockSpec((1, D_MODEL), fixed))
    args.append(gain)
    return pl.pallas_call(
        kern,
        out_shape=out_shape,
        grid_spec=pltpu.PrefetchScalarGridSpec(
            num_scalar_prefetch=1,
            grid=(N_TOK // tm,),
            in_specs=in_specs,
            out_specs=out_specs,
            scratch_shapes=[pltpu.VMEM((2, 2, tm, D_MODEL), F32), pltpu.SemaphoreType.DMA((2, 2))],
        ),
        compiler_params=_params(40),
        name="pre_final" if final else "pre_combine",
    )(pos, *args)


def _in_proj_kernel(h_ref, w_ref, o_ref):
    o_ref[...] = jnp.dot(h_ref[...], w_ref[...].astype(BF16), preferred_element_type=F32)


def _in_proj(h, w_in, layer):
    tm, tn = INPROJ_TM, INPROJ_TN
    return pl.pallas_call(
        _in_proj_kernel,
        out_shape=jax.ShapeDtypeStruct((N_TOK, IN_COLS), F32),
        grid=(N_TOK // tm, IN_COLS // tn),
        in_specs=[
            pl.BlockSpec((tm, D_MODEL), lambda i, j: (i, 0)),
            pl.BlockSpec((None, D_MODEL, tn), lambda i, j: (layer, 0, j)),
        ],
        out_specs=pl.BlockSpec((tm, tn), lambda i, j: (i, j)),
        compiler_params=_params(48),
        name="in_proj",
    )(h, w_in)


def _lambda(lam_ref, layer):
    lp = lam_ref[...]
    a = jnp.sum(lp[0:1, :] * lp[1:2, :], axis=-1, keepdims=True)
    b = jnp.sum(lp[2:3, :] * lp[3:4, :], axis=-1, keepdims=True)
    return jnp.exp(a) - jnp.exp(b) + _lam_init(layer)


def _split_q(q):
    lane = lax.broadcasted_iota(jnp.int32, q.shape, 1)
    q1 = jnp.where(lane < DIFF_QK_DIM, q, 0.0).astype(BF16)
    q2 = jnp.where(lane >= DIFF_QK_DIM, q, 0.0).astype(BF16)
    return q1, q2


def _sub_ln(o, gain, layer):
    return _rms(o, gain) * (1.0 - _lam_init(layer))


def _attn_prompt_kernel(nq_ref, nk_ref, nv_ref, dq_ref, dk_ref, dv_ref, lam_ref, sg_ref,
                        c0_in, c1_in, c2_in, c3_in,
                        on_ref, od_ref, onk_ref, onv_ref, odk_ref, odv_ref, *, layer):
    del c0_in, c1_in, c2_in, c3_in
    ones = jnp.ones((SEQ, HEAD_DIM), BF16)
    lam = _lambda(lam_ref, layer)

    def softmax_pv(q, k, v):
        s = lax.dot_general(q, k, NT_DIMS, preferred_element_type=F32)
        e = jnp.exp2(s - jnp.max(s, axis=-1, keepdims=True)).astype(BF16)
        return (jnp.dot(e, v, preferred_element_type=F32)
                / jnp.dot(e, ones, preferred_element_type=F32))

    for hh in range(PROMPT_HEADS):
        cols = slice(hh * HEAD_DIM, (hh + 1) * HEAD_DIM)
        nk = nk_ref[:, cols]
        nv = nv_ref[:, cols]
        dk = dk_ref[:, cols]
        dv = dv_ref[:, cols]
        onk_ref[hh] = nk
        onv_ref[hh] = nv
        odk_ref[hh] = dk
        odv_ref[hh] = dv
        q = (nq_ref[:, cols] * (HEAD_DIM ** -0.5 * LOG2E)).astype(BF16)
        on_ref[:, cols] = softmax_pv(q, nk.astype(BF16), nv.astype(BF16)).astype(BF16)
        q1, q2 = _split_q(dq_ref[:, cols] * (DIFF_QK_DIM ** -0.5 * LOG2E))
        kb = dk.astype(BF16)
        vb = dv.astype(BF16)
        o = softmax_pv(q1, kb, vb) - lam * softmax_pv(q2, kb, vb)
        od_ref[:, cols] = _sub_ln(o, sg_ref[...], layer).astype(BF16)


def _attn_prompt(qkv, lam_p, subln, caches, layer):
    hp = PROMPT_HEADS
    h8 = N_HEADS // hp
    blk = lambda off: pl.BlockSpec((SEQ, hp * HEAD_DIM), lambda b, h: (b, off + h))
    in_specs = [blk(0), blk(h8), blk(2 * h8), blk(3 * h8), blk(4 * h8), blk(5 * h8),
                pl.BlockSpec((None, 4, DIFF_QK_DIM), lambda b, h: (layer, 0, 0)),
                pl.BlockSpec((None, 1, HEAD_DIM), lambda b, h: (layer, 0, 0))]
    args = [qkv] * 6 + [lam_p, subln]
    aliases = {}
    for n, cache in enumerate(caches):
        aliases[len(args)] = 2 + n
        in_specs.append(pl.BlockSpec(memory_space=pl.ANY))
        args.append(cache)
    cache_shape = jax.ShapeDtypeStruct((BATCH, DEPTH, N_HEADS, SEQ, HEAD_DIM), F32)
    cache_spec = pl.BlockSpec((None, None, hp, SEQ, HEAD_DIM), lambda b, h: (b, layer, h, 0, 0))
    merged_shape = jax.ShapeDtypeStruct((N_PROMPT, N_HEADS * HEAD_DIM), BF16)
    merged_spec = pl.BlockSpec((SEQ, hp * HEAD_DIM), lambda b, h: (b, h))
    out = pl.pallas_call(
        functools.partial(_attn_prompt_kernel, layer=layer),
        out_shape=(merged_shape, merged_shape,
                   cache_shape, cache_shape, cache_shape, cache_shape),
        grid=(BATCH, h8),
        in_specs=in_specs,
        out_specs=(merged_spec, merged_spec,
                   cache_spec, cache_spec, cache_spec, cache_spec),
        input_output_aliases=aliases,
        compiler_params=_params(32),
        name="attn_prompt",
    )(*args)
    return out[0], out[1], out[2:]


def _attn_latent_kernel(nq_ref, nk_ref, nv_ref, dq_ref, dk_ref, dv_ref,
                        cnk_ref, cnv_ref, cdk_ref, cdv_ref, tab_ref, cos_ref, sin_ref,
                        lam_ref, sg_ref, on_ref, od_ref,
                        kb, vb, ckb, cvb, q1b, q2b, *, layer):
    kb[...] = nk_ref[...].astype(BF16)
    vb[...] = nv_ref[...].astype(BF16)
    ckb[...] = cnk_ref[...].astype(BF16)
    cvb[...] = cnv_ref[...].astype(BF16)
    masked = jnp.full((GRID_W, 2 * GRID_W), MASK_VALUE, F32)
    left = lax.broadcasted_iota(jnp.int32, (GRID_W, 2 * GRID_W), 1) < GRID_W

    def bias_tile(r, kr):
        lo = min(max(r - WIN_H // 2, 0), GRID_H - WIN_H)
        ok0 = lo <= kr < lo + WIN_H
        ok1 = lo <= kr + 1 < lo + WIN_H
        if not (ok0 or ok1):
            return masked
        tile = tab_ref[kr - r + WIN_H]
        if ok0 and ok1:
            return tile
        return jnp.where(left if ok0 else jnp.logical_not(left), tile, masked)

    for blk in range(GRID_H // NAT_QROWS):
        r0 = blk * NAT_QROWS
        k0 = min(max(r0 - WIN_H // 2, 0), GRID_H - NAT_KROWS)
        rows = slice(r0 * GRID_W, (r0 + NAT_QROWS) * GRID_W)
        keys = slice(k0 * GRID_W, (k0 + NAT_KROWS) * GRID_W)
        q = (nq_ref[rows, :] * (HEAD_DIM ** -0.5 * LOG2E)).astype(BF16)
        bias = jnp.concatenate(
            [jnp.concatenate([bias_tile(r0 + a, k0 + 2 * i) for i in range(NAT_KROWS // 2)], axis=1)
             for a in range(NAT_QROWS)], axis=0)
        s_nb = lax.dot_general(q, kb[keys, :], NT_DIMS, preferred_element_type=F32) + bias
        s_cx = lax.dot_general(q, ckb[...], NT_DIMS, preferred_element_type=F32)
        m = jnp.maximum(jnp.max(s_nb, axis=-1, keepdims=True),
                        jnp.max(s_cx, axis=-1, keepdims=True))
        e_nb = jnp.exp2(s_nb - m)
        e_cx = jnp.exp2(s_cx - m)
        z = jnp.sum(e_nb, axis=-1, keepdims=True) + jnp.sum(e_cx, axis=-1, keepdims=True)
        o = (jnp.dot(e_nb.astype(BF16), vb[keys, :], preferred_element_type=F32)
             + jnp.dot(e_cx.astype(BF16), cvb[...], preferred_element_type=F32))
        on_ref[rows, :] = (o / z).astype(BF16)

    lane = lax.broadcasted_iota(jnp.int32, (DEC_SEQ, HEAD_DIM), 1)
    first_half = (lane & 16) == 0

    def rope(x):
        rot = jnp.where(first_half, pltpu.roll(x, HEAD_DIM - 16, 1), pltpu.roll(x, 16, 1))
        return x * cos_ref[...] + rot * sin_ref[...]

    q1, q2 = _split_q(rope(dq_ref[...]) * (DIFF_QK_DIM ** -0.5 * LOG2E))
    q1b[...] = q1
    q2b[...] = q2
    kb[...] = rope(dk_ref[...]).astype(BF16)
    vb[...] = dv_ref[...].astype(BF16)
    ckb[...] = cdk_ref[...].astype(BF16)
    cvb[...] = cdv_ref[...].astype(BF16)
    lam = _lambda(lam_ref, layer)

    def diff_block(j, carry):
        r0 = pl.multiple_of(j * Q_BLOCK, Q_BLOCK)

        def softmax_pv(q):
            s_l = lax.dot_general(q, kb[...], NT_DIMS, preferred_element_type=F32)
            s_c = lax.dot_general(q, ckb[...], NT_DIMS, preferred_element_type=F32)
            m = jnp.maximum(jnp.max(s_l, axis=-1, keepdims=True),
                            jnp.max(s_c, axis=-1, keepdims=True))
            e_l = jnp.exp2(s_l - m)
            e_c = jnp.exp2(s_c - m)
            z = jnp.sum(e_l, axis=-1, keepdims=True) + jnp.sum(e_c, axis=-1, keepdims=True)
            o = (jnp.dot(e_l.astype(BF16), vb[...], preferred_element_type=F32)
                 + jnp.dot(e_c.astype(BF16), cvb[...], preferred_element_type=F32))
            return o, z

        o1, z1 = softmax_pv(q1b[pl.ds(r0, Q_BLOCK), :])
        o2, z2 = softmax_pv(q2b[pl.ds(r0, Q_BLOCK), :])
        o = o1 * (1.0 / z1) - o2 * (lam / z2)
        od_ref[pl.ds(r0, Q_BLOCK), :] = _sub_ln(o, sg_ref[...], layer).astype(BF16)
        return carry

    lax.fori_loop(0, DEC_SEQ // Q_BLOCK, diff_block, 0, unroll=True)


def _attn_latent(qkv, caches, tab, cos, sin, lam_p, subln, layer):
    h8 = N_HEADS
    row0 = N_PROMPT // DEC_SEQ
    blk = lambda off: pl.BlockSpec((DEC_SEQ, HEAD_DIM), lambda b, h: (row0 + b, off + h))
    cblk = pl.BlockSpec((None, None, None, PAST_LEN, HEAD_DIM), lambda b, h: (b, layer, h, 0, 0))
    fixed = pl.BlockSpec((DEC_SEQ, HEAD_DIM), lambda b, h: (0, 0))
    out_spec = pl.BlockSpec((DEC_SEQ, HEAD_DIM), lambda b, h: (b, h))
    merged_shape = jax.ShapeDtypeStruct((N_TOK - N_PROMPT, h8 * HEAD_DIM), BF16)
    return pl.pallas_call(
        functools.partial(_attn_latent_kernel, layer=layer),
        out_shape=(merged_shape, merged_shape),
        grid=(DEC_BATCH, h8),
        in_specs=[blk(0), blk(h8), blk(2 * h8), blk(3 * h8), blk(4 * h8), blk(5 * h8),
                  cblk, cblk, cblk, cblk,
                  pl.BlockSpec((None, 2 * WIN_H, GRID_W, 2 * GRID_W), lambda b, h: (h, 0, 0, 0)),
                  fixed, fixed,
                  pl.BlockSpec((None, 4, DIFF_QK_DIM), lambda b, h: (layer, 0, 0)),
                  pl.BlockSpec((None, 1, HEAD_DIM), lambda b, h: (layer, 0, 0))],
        out_specs=(out_spec, out_spec),
        scratch_shapes=[pltpu.VMEM((DEC_SEQ, HEAD_DIM), BF16), pltpu.VMEM((DEC_SEQ, HEAD_DIM), BF16),
                        pltpu.VMEM((PAST_LEN, HEAD_DIM), BF16), pltpu.VMEM((PAST_LEN, HEAD_DIM), BF16),
                        pltpu.VMEM((DEC_SEQ, HEAD_DIM), BF16), pltpu.VMEM((DEC_SEQ, HEAD_DIM), BF16)],
        compiler_params=_params(48),
        name="attn_latent",
    )(qkv, qkv, qkv, qkv, qkv, qkv, *caches, tab, cos, sin, lam_p, subln)


def _route(logits, bias):
    ex = jnp.exp(logits - jnp.max(logits, axis=0, keepdims=True))
    probs = ex / jnp.sum(ex, axis=0, keepdims=True)
    sel = probs + bias
    srow = [sel[e:e + 1, :] for e in range(N_EXPERTS)]
    prow = [probs[e:e + 1, :] for e in range(N_EXPERTS)]
    gscore = []
    for g in range(N_GROUPS):
        v = srow[g * EXPERTS_PER_GROUP:(g + 1) * EXPERTS_PER_GROUP]
        best = None
        for a in range(EXPERTS_PER_GROUP):
            for b in range(a + 1, EXPERTS_PER_GROUP):
                pair = v[a] + v[b]
                best = pair if best is None else jnp.maximum(best, pair)
        gscore.append(best)
    gbest = gscore[0]
    gidx = jnp.zeros(gbest.shape, jnp.int32)
    for g in range(1, N_GROUPS):
        better = gscore[g] > gbest
        gidx = jnp.where(better, g, gidx)
        gbest = jnp.where(better, gscore[g], gbest)
    neg = jnp.full(gbest.shape, -jnp.inf, F32)
    picks = []
    taken = None
    for _ in range(2):
        best = neg
        idx = jnp.zeros(gbest.shape, jnp.int32)
        wgt = jnp.zeros(gbest.shape, F32)
        for e in range(N_EXPERTS):
            ok = gidx == (e // EXPERTS_PER_GROUP)
            if taken is not None:
                ok = jnp.logical_and(ok, taken != e)
            cand = jnp.where(ok, srow[e], neg)
            better = cand > best
            idx = jnp.where(better, e, idx)
            wgt = jnp.where(better, prow[e], wgt)
            best = jnp.where(better, cand, best)
        picks.append((idx, wgt))
        taken = idx
    (i0, w0), (i1, w1) = picks
    wsum = w0 + w1
    return i0, i1, w0 / wsum, w1 / wsum


def _split_bf16(v):
    hi = v.astype(BF16)
    lo = (v - hi.astype(F32)).astype(BF16)
    return hi, lo


def _load_resident_weight(w_hbm, wbf, stage, sem, n_chunks):
    ck = wbf.shape[0] // n_chunks
    copies = [pltpu.make_async_copy(w_hbm.at[pl.ds(c * ck, ck), :], stage.at[c % 2], sem.at[c % 2])
              for c in range(n_chunks)]
    copies[0].start()
    for c in range(n_chunks):
        if c + 1 < n_chunks:
            copies[c + 1].start()
        copies[c].wait()
        wbf[c * ck:(c + 1) * ck, :] = stage[c % 2].astype(BF16)


def _out_proj_kernel(pn_ref, pd_ref, sn_ref, sd_ref, w_hbm, x_ref, mod_ref, g_ref, wr_ref, rb_ref,
                     xo_ref, h_ref, idx_ref, cw_ref, wbf, stage, mbuf, sem, *, layer):
    i = pl.program_id(0)
    half = D_MODEL // 2

    @pl.when(i == 0)
    def _():
        _load_resident_weight(w_hbm.at[layer], wbf, stage, sem, OUTPROJ_WCHUNKS)

    prompt = i < N_PROMPT // OUTPROJ_TM

    @pl.when(prompt)
    def _():
        mbuf[:, :half] = pn_ref[...]
        mbuf[:, half:] = pd_ref[...]

    @pl.when(jnp.logical_not(prompt))
    def _():
        mbuf[:, :half] = sn_ref[...]
        mbuf[:, half:] = sd_ref[...]

    a = jnp.dot(mbuf[...], wbf[...], preferred_element_type=F32)
    x = x_ref[...] + mod_ref[2:3, :] * a
    xo_ref[...] = x
    h = _rms(x, g_ref[...]) * (1.0 + mod_ref[4:5, :]) + mod_ref[3:4, :]
    h_ref[...] = h
    w_hi, w_lo = _split_bf16(wr_ref[...])
    h_hi, h_lo = _split_bf16(h)
    part = lax.dot_general(jnp.concatenate([w_hi, w_lo], axis=0), h_hi, NT_DIMS,
                           preferred_element_type=F32)
    logits = (part[:N_EXPERTS] + part[N_EXPERTS:]
              + lax.dot_general(w_hi, h_lo, NT_DIMS, preferred_element_type=F32))
    i0, i1, w0, w1 = _route(logits, rb_ref[...])
    idx_ref[...] = jnp.concatenate([i0, i1], axis=0)
    cw_ref[...] = jnp.concatenate([w0, w1], axis=0)


def _out_proj(merged, w_out, x, mods, layer, gain, w_router_t, router_bias):
    tm = OUTPROJ_TM
    half = D_MODEL // 2
    npt = N_PROMPT // tm
    row = lambda i: (i, 0)
    fixed = lambda i: (0, 0)
    p_row = lambda i: (jnp.minimum(i, npt - 1), 0)
    s_row = lambda i: (jnp.maximum(i - npt, 0), 0)
    return pl.pallas_call(
        functools.partial(_out_proj_kernel, layer=layer),
        out_shape=(jax.ShapeDtypeStruct((N_TOK, D_MODEL), F32),
                   jax.ShapeDtypeStruct((N_TOK, D_MODEL), F32),
                   jax.ShapeDtypeStruct((2, N_TOK), jnp.int32),
                   jax.ShapeDtypeStruct((2, N_TOK), F32)),
        grid=(N_TOK // tm,),
        in_specs=[
            pl.BlockSpec((tm, half), p_row),
            pl.BlockSpec((tm, half), p_row),
            pl.BlockSpec((tm, half), s_row),
            pl.BlockSpec((tm, half), s_row),
            pl.BlockSpec(memory_space=pl.ANY),
            pl.BlockSpec((tm, D_MODEL), row),
            pl.BlockSpec((None, None, 6, D_MODEL), lambda i: (layer, _mod_row(i, tm), 0, 0)),
            pl.BlockSpec((1, D_MODEL), fixed),
            pl.BlockSpec((N_EXPERTS, D_MODEL), fixed),
            pl.BlockSpec((N_EXPERTS, 1), fixed),
        ],
        out_specs=(pl.BlockSpec((tm, D_MODEL), row), pl.BlockSpec((tm, D_MODEL), row),
                   pl.BlockSpec((2, tm), lambda i: (0, i)),
                   pl.BlockSpec((2, tm), lambda i: (0, i))),
        scratch_shapes=[pltpu.VMEM((D_MODEL, D_MODEL), BF16),
                        pltpu.VMEM((2, D_MODEL // OUTPROJ_WCHUNKS, D_MODEL), F32),
                        pltpu.VMEM((tm, D_MODEL), BF16),
                        pltpu.SemaphoreType.DMA((2,))],
        compiler_params=_params(48),
        name="out_proj_router",
    )(*merged, w_out, x, mods, gain, w_router_t, router_bias)


def _dispatch_kernel(pos_ref, h_ref, xs_in, xs_out, sem):
    del xs_in
    tm = DISPATCH_TM
    base = pl.program_id(0) * tm

    def issue(r, carry):
        for k in range(2):
            p = pos_ref[k * N_TOK + base + r]
            pltpu.make_async_copy(h_ref.at[pl.ds(r, 1), :], xs_out.at[pl.ds(p, 1), :], sem).start()
        return carry

    lax.fori_loop(0, tm, issue, 0, unroll=8)
    for _ in range(2):
        pltpu.make_async_copy(h_ref, xs_out.at[pl.ds(0, tm), :], sem).wait()


def _dispatch(pos, h2, xs_buf):
    tm = DISPATCH_TM
    return pl.pallas_call(
        _dispatch_kernel,
        out_shape=jax.ShapeDtypeStruct((EXPERT_CAP, D_MODEL), F32),
        grid_spec=pltpu.PrefetchScalarGridSpec(
            num_scalar_prefetch=1,
            grid=(N_TOK // tm,),
            in_specs=[pl.BlockSpec((tm, D_MODEL), lambda i, pos_ref: (i, 0)),
                      pl.BlockSpec(memory_space=pl.ANY)],
            out_specs=pl.BlockSpec(memory_space=pl.ANY),
            scratch_shapes=[pltpu.SemaphoreType.DMA(())],
        ),
        input_output_aliases={2: 0},
        compiler_params=_params(32),
        name="expert_dispatch",
    )(pos, h2, xs_buf)


def _experts_kernel(we_ref, wc_ref, wdo_ref, wsl_ref, ti_ref, tdo_ref, tsl_ref,
                    xs_ref, wg_ref, wu_ref, wd_ref, o_ref, wg_bf, wu_bf, wd_bf):
    del we_ref, ti_ref
    s = pl.program_id(0)

    @pl.when(wdo_ref[s] == 1)
    def _():
        slot = wsl_ref[s]
        c = wc_ref[s]
        ck = D_MODEL // EXPERT_WCHUNKS
        r0 = pl.multiple_of(c * ck, ck)
        wg_bf[slot, pl.ds(r0, ck), :] = wg_ref[...].astype(BF16)
        wu_bf[slot, pl.ds(r0, ck), :] = wu_ref[...].astype(BF16)
        cf = D_FF // EXPERT_WCHUNKS
        r1 = pl.multiple_of(c * cf, cf)
        wd_bf[slot, pl.ds(r1, cf), :] = wd_ref[...].astype(BF16)

    @pl.when(tdo_ref[s] == 1)
    def _():
        slot = tsl_ref[s]
        xb = xs_ref[...].astype(BF16)
        g = jnp.dot(xb, wg_bf[slot], preferred_element_type=F32)
        u = jnp.dot(xb, wu_bf[slot], preferred_element_type=F32)
        hid = (g * jax.nn.sigmoid(g) * u).astype(BF16)
        o_ref[...] = jnp.dot(hid, wd_bf[slot], preferred_element_type=F32)

    @pl.when(tdo_ref[s] == 2)
    def _():
        o_ref[...] = jnp.zeros_like(o_ref)


def _experts(plan, xs, w_gate, w_up, w_down, layer):
    tm = EXPERT_TM
    ck = D_MODEL // EXPERT_WCHUNKS
    cf = D_FF // EXPERT_WCHUNKS
    tile = lambda s, we, wc, wdo, wsl, ti, tdo, tsl: (ti[s], 0)
    wblk = lambda s, we, wc, wdo, wsl, ti, tdo, tsl: (layer, we[s], wc[s], 0)
    return pl.pallas_call(
        _experts_kernel,
        out_shape=jax.ShapeDtypeStruct((EXPERT_CAP, D_MODEL), F32),
        grid_spec=pltpu.PrefetchScalarGridSpec(
            num_scalar_prefetch=7,
            grid=(EXPERT_STEPS,),
            in_specs=[
                pl.BlockSpec((tm, D_MODEL), tile),
                pl.BlockSpec((None, None, ck, D_FF), wblk),
                pl.BlockSpec((None, None, ck, D_FF), wblk),
                pl.BlockSpec((None, None, cf, D_MODEL), wblk),
            ],
            out_specs=pl.BlockSpec((tm, D_MODEL), tile),
            scratch_shapes=[pltpu.VMEM((2, D_MODEL, D_FF), BF16),
                            pltpu.VMEM((2, D_MODEL, D_FF), BF16),
                            pltpu.VMEM((2, D_FF, D_MODEL), BF16)],
        ),
        compiler_params=_params(56),
        name="expert_mlp",
    )(*plan, xs, w_gate, w_up, w_down)


def _dispatch_plan(idx):
    e = idx.reshape(-1)
    onehot = (e[:, None] == jnp.arange(N_EXPERTS, dtype=jnp.int32)[None, :]).astype(jnp.int32)
    csum = jnp.cumsum(onehot, axis=0)
    counts = csum[-1]
    rank = jnp.sum(onehot * csum, axis=1) - 1
    nt = (counts + EXPERT_TM - 1) // EXPERT_TM
    tile_end = jnp.cumsum(nt)
    tile_start = tile_end - nt
    pos = jnp.sum(onehot * (tile_start * EXPERT_TM)[None, :], axis=1) + rank

    nw = EXPERT_WCHUNKS
    length = jnp.maximum(nt, nw)
    phase_end = nw + jnp.cumsum(length)
    phase_start = phase_end - length
    zero = jnp.zeros((1,), jnp.int32)
    s = jnp.arange(EXPERT_STEPS, dtype=jnp.int32)
    ph = jnp.sum((s[:, None] >= phase_start[None, :]).astype(jnp.int32), axis=1)
    ex = ph - 1
    k = s - jnp.concatenate([zero, phase_start])[ph]
    nt_s = jnp.concatenate([zero, nt])[ph]
    ts_s = jnp.concatenate([zero, tile_start])[ph]
    live = s < phase_end[-1]
    last = N_EXPERTS - 1
    w_do = (live & (k < nw) & (ex < last)).astype(jnp.int32)
    w_e = jnp.minimum(ex + 1, last)
    w_c = jnp.where(ex < last, jnp.minimum(k, nw - 1), nw - 1)
    w_slot = (ex + 1) % 2
    t_do = (live & (k < nt_s)).astype(jnp.int32)
    t_idx = jnp.maximum(ts_s + jnp.minimum(k + 1, nt_s) - 1, 0)
    z = tile_end[-1] + (s - phase_end[-1])
    t_do = jnp.where(jnp.logical_not(live) & (z < EXPERT_TILES), 2, t_do)
    t_idx = jnp.where(live, t_idx, jnp.minimum(z, EXPERT_TILES - 1))
    t_slot = jnp.maximum(ex, 0) % 2
    plan = tuple(a.astype(jnp.int32) for a in (w_e, w_c, w_do, w_slot, t_idx, t_do, t_slot))
    return pos.astype(jnp.int32), plan


def _bias_tables(nat_rpb):
    w = GRID_W
    c = jnp.arange(w)[:, None]
    kc = jnp.arange(w)[None, :]
    qstart = jnp.clip(c - WIN_W // 2, 0, w - WIN_W)
    valid = (kc >= qstart) & (kc < qstart + WIN_W)
    lead = nat_rpb.shape[:-1]
    pad_l = w - WIN_W
    row = jnp.pad(nat_rpb, [(0, 0)] * len(lead) + [(pad_l, 2 * w - pad_l - nat_rpb.shape[-1])])
    skew = jnp.tile(row, w)[..., :w * (2 * w - 1)].reshape(lead + (w, 2 * w - 1))
    tab = jnp.where(valid, skew[..., w - 1:2 * w - 1] * LOG2E, MASK_VALUE)
    tab = jnp.pad(tab, [(0, 0), (0, 0), (1, 1), (0, 0), (0, 0)], constant_values=MASK_VALUE)
    return jnp.concatenate([tab[:, :, :-1], tab[:, :, 1:]], axis=-1)


def _rope_tables():
    t = jnp.arange(DEC_SEQ)
    row = (t // GRID_W).astype(F32)
    col = (t % GRID_W).astype(F32)
    nf = DIFF_QK_DIM // 4
    inv = ROPE_BASE ** (-jnp.arange(nf, dtype=F32) / nf)
    ar = row[:, None] * inv[None, :]
    ac = col[:, None] * inv[None, :]
    cos = jnp.concatenate([jnp.cos(ar), jnp.cos(ar), jnp.cos(ac), jnp.cos(ac)], axis=-1)
    sin = jnp.concatenate([-jnp.sin(ar), jnp.sin(ar), -jnp.sin(ac), jnp.sin(ac)], axis=-1)
    return jnp.tile(cos, (1, 2)), jnp.tile(sin, (1, 2))


def kernel(x_prompt, x_sample, cache_nat_k, cache_nat_v, cache_diff_k, cache_diff_v, c, c_ctx,
           w_ada, b_ada, norm1, norm2, norm_final, w_in, w_out, nat_rpb, diff_lambda, diff_subln,
           w_router, router_bias, w_gate, w_up, w_down):
    x = jnp.concatenate([x_prompt.reshape(N_PROMPT, D_MODEL),
                         x_sample.reshape(DEC_BATCH * DEC_SEQ, D_MODEL)], axis=0)
    cond = jnp.concatenate([c_ctx[None, :], c, jnp.zeros((8 - 1 - DEC_BATCH, D_MODEL), F32)], axis=0)
    mods = _modulation(cond, w_ada, b_ada)
    tabs = _bias_tables(nat_rpb)
    cos, sin = _rope_tables()
    subln = diff_subln.reshape(DEPTH, 1, HEAD_DIM)
    w_router_t = w_router.T
    rbias = router_bias.reshape(N_EXPERTS, 1)
    lat_caches = (cache_nat_k, cache_nat_v, cache_diff_k, cache_diff_v)

    new_caches = tuple(jnp.zeros((BATCH, DEPTH, N_HEADS, SEQ, HEAD_DIM), F32) for _ in range(4))
    xs_buf = jnp.zeros((EXPERT_CAP, D_MODEL), F32)
    ys = pos = cw = None
    for layer in range(DEPTH):
        gain1 = norm1[layer].reshape(1, D_MODEL)
        if layer == 0:
            h = _pre_first(x, mods, layer, gain1)
        else:
            x, h = _pre_combine(x, ys, pos, cw, mods, layer - 1, gain1, final=False)
        qkv = _in_proj(h, w_in, layer)
        pn, pd, new_caches = _attn_prompt(qkv, diff_lambda, subln, new_caches, layer)
        sn, sd = _attn_latent(qkv, lat_caches, tabs[layer], cos, sin, diff_lambda, subln, layer)
        x, h2, idx, cwt = _out_proj((pn, pd, sn, sd), w_out, x, mods, layer,
                                    norm2[layer].reshape(1, D_MODEL), w_router_t, rbias)
        pos, plan = _dispatch_plan(idx)
        cw = cwt.T
        xs_buf = _dispatch(pos, h2, xs_buf)
        ys = _experts(plan, xs_buf, w_gate, w_up, w_down, layer)
    y = _pre_combine(x, ys, pos, cw, mods, DEPTH - 1, norm_final.reshape(1, D_MODEL), final=True)
    y_prompt = y[:N_PROMPT].reshape(BATCH, SEQ, D_MODEL)
    y_sample = y[N_PROMPT:].reshape(DEC_BATCH, DEC_SEQ, D_MODEL)
    return (y_prompt, y_sample) + tuple(new_caches)
```

```python
import functools
import math

import jax
import jax.numpy as jnp
from jax import lax
from jax.experimental import pallas as pl
from jax.experimental.pallas import tpu as pltpu

D_MODEL = 2048
BATCH = 16
SEQ = 256
DEPTH = 4
DEC_BATCH = 4
DEC_SEQ = 1024
PAST_LEN = 512
GRID_W = 64
GRID_H = DEC_SEQ // GRID_W
N_HEADS = 8
HEAD_DIM = 128
DIFF_QK_DIM = 64
WIN_H = 8
WIN_W = 16
N_EXPERTS = 16
N_GROUPS = 4
EXPERTS_PER_GROUP = N_EXPERTS // N_GROUPS
D_FF = 1024
ROPE_BASE = 10000.0
EPS = 1e-6
IN_COLS = 6 * N_HEADS * HEAD_DIM
N_PROMPT = BATCH * SEQ
N_TOK = N_PROMPT + DEC_BATCH * DEC_SEQ
N_PAIRS = 2 * N_TOK

PRE_TM = 256
INPROJ_TM = 2048
INPROJ_TN = 512
OUTPROJ_TM = 256
OUTPROJ_WCHUNKS = 4
DISPATCH_TM = 256
EXPERT_TM = 256
EXPERT_CAP = N_PAIRS + N_EXPERTS * EXPERT_TM
EXPERT_TILES = EXPERT_CAP // EXPERT_TM
EXPERT_WCHUNKS = 4
EXPERT_STEPS = EXPERT_WCHUNKS * (N_EXPERTS + 1) + EXPERT_TILES
MOD_TN = 1024
Q_BLOCK = 256
PROMPT_HEADS = 4
NAT_QROWS = 4
NAT_KROWS = 12
LOG2E = 1.4426950408889634
MASK_VALUE = -1e30

F32 = jnp.float32
BF16 = jnp.bfloat16
MIB = 1024 * 1024
NT_DIMS = (((1,), (1,)), ((), ()))


def _params(vmem_mib):
    return pltpu.CompilerParams(vmem_limit_bytes=vmem_mib * MIB)


def _lam_init(layer):
    return 0.8 - 0.6 * math.exp(-0.3 * layer)


def _rms(x, gain):
    return x * lax.rsqrt(jnp.mean(x * x, axis=-1, keepdims=True) + EPS) * gain


def _mod_row(i, tm):
    first = i * tm
    return jnp.where(first < N_PROMPT, 0, 1 + (first - N_PROMPT) // DEC_SEQ)


def _mod_kernel(cond_ref, w_ref, b_ref, o_ref):
    c = cond_ref[...]
    s = c * jax.nn.sigmoid(c)
    o_ref[...] = jnp.dot(s.astype(BF16), w_ref[...].astype(BF16),
                         preferred_element_type=F32) + b_ref[...]


def _modulation(cond, w_ada, b_ada):
    n = 6 * D_MODEL
    out = pl.pallas_call(
        _mod_kernel,
        out_shape=jax.ShapeDtypeStruct((DEPTH, 8, n), F32),
        grid=(DEPTH, n // MOD_TN),
        in_specs=[
            pl.BlockSpec((8, D_MODEL), lambda l, j: (0, 0)),
            pl.BlockSpec((None, D_MODEL, MOD_TN), lambda l, j: (l, 0, j)),
            pl.BlockSpec((None, 1, MOD_TN), lambda l, j: (l, 0, j)),
        ],
        out_specs=pl.BlockSpec((None, 8, MOD_TN), lambda l, j: (l, 0, j)),
        compiler_params=_params(40),
        name="adaln_modulation",
    )(cond, w_ada, b_ada.reshape(DEPTH, 1, n))
    return out.reshape(DEPTH, 8, 6, D_MODEL)


def _gather_expert_rows(pos_ref, ys_hbm, ybuf, sem, tm):
    i = pl.program_id(0)
    n = pl.num_programs(0)

    def issue_tile(tile, slot):
        def issue(r, carry):
            for k in range(2):
                p = pos_ref[k * N_TOK + tile * tm + r]
                pltpu.make_async_copy(ys_hbm.at[pl.ds(p, 1), :],
                                      ybuf.at[slot, k, pl.ds(r, 1), :], sem.at[slot, k]).start()
            return carry

        lax.fori_loop(0, tm, issue, 0, unroll=8)

    @pl.when(i == 0)
    def _():
        issue_tile(0, 0)

    @pl.when(i + 1 < n)
    def _():
        issue_tile(i + 1, (i + 1) % 2)

    slot = i % 2
    for k in range(2):
        pltpu.make_async_copy(ys_hbm.at[pl.ds(0, tm), :], ybuf.at[slot, k], sem.at[slot, k]).wait()
    return slot


def _select_stream(xp_ref, xs_ref, tm):
    prompt = pl.program_id(0) < N_PROMPT // tm
    return jnp.where(prompt, xp_ref[...], xs_ref[...])


def _pre_first_kernel(xp_ref, xs_ref, mod_ref, g_ref, h_ref):
    x = _select_stream(xp_ref, xs_ref, PRE_TM)
    h = _rms(x, g_ref[...]) * (1.0 + mod_ref[1:2, :]) + mod_ref[0:1, :]
    h_ref[...] = h.astype(BF16)


def _pre_mid_kernel(pos_ref, x_ref, ys_hbm, cw_ref, modp_ref, mod_ref, g_ref,
                    xo_ref, h_ref, ybuf, sem):
    slot = _gather_expert_rows(pos_ref, ys_hbm, ybuf, sem, PRE_TM)
    cw = cw_ref[...]
    y = cw[:, 0:1] * ybuf[slot, 0] + cw[:, 1:2] * ybuf[slot, 1]
    x = x_ref[...] + modp_ref[5:6, :] * y
    xo_ref[...] = x
    h = _rms(x, g_ref[...]) * (1.0 + mod_ref[1:2, :]) + mod_ref[0:1, :]
    h_ref[...] = h.astype(BF16)


def _pre_final_kernel(pos_ref, x_ref, ys_hbm, cw_ref, modp_ref, g_ref, yp_ref, ys_ref, ybuf, sem):
    slot = _gather_expert_rows(pos_ref, ys_hbm, ybuf, sem, PRE_TM)
    cw = cw_ref[...]
    y = cw[:, 0:1] * ybuf[slot, 0] + cw[:, 1:2] * ybuf[slot, 1]
    x = x_ref[...] + modp_ref[5:6, :] * y
    out = _rms(x, g_ref[...])
    prompt = pl.program_id(0) < N_PROMPT // PRE_TM

    @pl.when(prompt)
    def _():
        yp_ref[...] = out

    @pl.when(jnp.logical_not(prompt))
    def _():
        ys_ref[...] = out


def _stream_specs(tm):
    npt = N_PROMPT // tm
    return [pl.BlockSpec((tm, D_MODEL), lambda i, *_: (jnp.minimum(i, npt - 1), 0)),
            pl.BlockSpec((tm, D_MODEL), lambda i, *_: (jnp.maximum(i - npt, 0), 0))]


def _pre_first(xp, xs, mods, layer, gain):
    tm = PRE_TM
    return pl.pallas_call(
        _pre_first_kernel,
        out_shape=jax.ShapeDtypeStruct((N_TOK, D_MODEL), BF16),
        grid=(N_TOK // tm,),
        in_specs=_stream_specs(tm) + [
            pl.BlockSpec((None, None, 6, D_MODEL), lambda i: (layer, _mod_row(i, tm), 0, 0)),
            pl.BlockSpec((1, D_MODEL), lambda i: (0, 0)),
        ],
        out_specs=pl.BlockSpec((tm, D_MODEL), lambda i: (i, 0)),
        compiler_params=_params(32),
        name="pre_first",
    )(xp, xs, mods, gain)


def _pre_combine(x, ys, pos, cw, mods, prev_layer, gain, final):
    tm = PRE_TM
    row = lambda i, pos_ref: (i, 0)
    fixed = lambda i, pos_ref: (0, 0)
    in_specs = [
        pl.BlockSpec((tm, D_MODEL), row),
        pl.BlockSpec(memory_space=pl.ANY),
        pl.BlockSpec((tm, 2), row),
        pl.BlockSpec((None, None, 6, D_MODEL),
                     lambda i, pos_ref: (prev_layer, _mod_row(i, tm), 0, 0)),
    ]
    args = [x, ys, cw, mods]
    if final:
        kern = _pre_final_kernel
        out_shape = (jax.ShapeDtypeStruct((N_PROMPT, D_MODEL), F32),
                     jax.ShapeDtypeStruct((N_TOK - N_PROMPT, D_MODEL), F32))
        out_specs = tuple(_stream_specs(tm))
    else:
        kern = _pre_mid_kernel
        in_specs.append(pl.BlockSpec((None, None, 6, D_MODEL),
                                     lambda i, pos_ref: (prev_layer + 1, _mod_row(i, tm), 0, 0)))
        args.append(mods)
        out_shape = (jax.ShapeDtypeStruct((N_TOK, D_MODEL), F32),
                     jax.ShapeDtypeStruct((N_TOK, D_MODEL), BF16))
        out_specs = (pl.BlockSpec((tm, D_MODEL), row), pl.BlockSpec((tm, D_MODEL), row))
    in_specs.append(pl.BlockSpec((1, D_MODEL), fixed))
    args.append(gain)
    return pl.pallas_call(
        kern,
        out_shape=out_shape,
        grid_spec=pltpu.PrefetchScalarGridSpec(
            num_scalar_prefetch=1,
            grid=(N_TOK // tm,),
            in_specs=in_specs,
            out_specs=out_specs,
            scratch_shapes=[pltpu.VMEM((2, 2, tm, D_MODEL), F32), pltpu.SemaphoreType.DMA((2, 2))],
        ),
        compiler_params=_params(40),
        name="pre_final" if final else "pre_combine",
    )(pos, *args)


def _in_proj_kernel(h_ref, w_ref, o_ref):
    o_ref[...] = jnp.dot(h_ref[...], w_ref[...].astype(BF16), preferred_element_type=F32)


def _in_proj(h, w_in, layer):
    tm, tn = INPROJ_TM, INPROJ_TN
    return pl.pallas_call(
        _in_proj_kernel,
        out_shape=jax.ShapeDtypeStruct((N_TOK, IN_COLS), F32),
        grid=(N_TOK // tm, IN_COLS // tn),
        in_specs=[
            pl.BlockSpec((tm, D_MODEL), lambda i, j: (i, 0)),
            pl.BlockSpec((None, D_MODEL, tn), lambda i, j: (layer, 0, j)),
        ],
        out_specs=pl.BlockSpec((tm, tn), lambda i, j: (i, j)),
        compiler_params=_params(48),
        name="in_proj",
    )(h, w_in)


def _lambda(lam_ref, layer):
    lp = lam_ref[...]
    a = jnp.sum(lp[0:1, :] * lp[1:2, :], axis=-1, keepdims=True)
    b = jnp.sum(lp[2:3, :] * lp[3:4, :], axis=-1, keepdims=True)
    return jnp.exp(a) - jnp.exp(b) + _lam_init(layer)


def _split_q(q):
    lane = lax.broadcasted_iota(jnp.int32, q.shape, 1)
    q1 = jnp.where(lane < DIFF_QK_DIM, q, 0.0).astype(BF16)
    q2 = jnp.where(lane >= DIFF_QK_DIM, q, 0.0).astype(BF16)
    return q1, q2


def _sub_ln(o, gain, layer):
    return _rms(o, gain) * (1.0 - _lam_init(layer))


def _attn_prompt_kernel(nq_ref, nk_ref, nv_ref, dq_ref, dk_ref, dv_ref, lam_ref, sg_ref,
                        c0_in, c1_in, c2_in, c3_in,
                        on_ref, od_ref, onk_ref, onv_ref, odk_ref, odv_ref, *, layer):
    del c0_in, c1_in, c2_in, c3_in
    ones = jnp.ones((SEQ, HEAD_DIM), BF16)
    lam = _lambda(lam_ref, layer)

    def softmax_pv(q, k, v):
        s = lax.dot_general(q, k, NT_DIMS, preferred_element_type=F32)
        e = jnp.exp2(s - jnp.max(s, axis=-1, keepdims=True)).astype(BF16)
        oz = jnp.dot(e, jnp.concatenate([v, ones], axis=1), preferred_element_type=F32)
        return oz[:, :HEAD_DIM] / oz[:, HEAD_DIM:]

    for hh in range(PROMPT_HEADS):
        cols = slice(hh * HEAD_DIM, (hh + 1) * HEAD_DIM)
        nk = nk_ref[:, cols]
        nv = nv_ref[:, cols]
        dk = dk_ref[:, cols]
        dv = dv_ref[:, cols]
        onk_ref[hh] = nk
        onv_ref[hh] = nv
        odk_ref[hh] = dk
        odv_ref[hh] = dv
        q = (nq_ref[:, cols] * (HEAD_DIM ** -0.5 * LOG2E)).astype(BF16)
        on_ref[:, cols] = softmax_pv(q, nk.astype(BF16), nv.astype(BF16)).astype(BF16)
        q1, q2 = _split_q(dq_ref[:, cols] * (DIFF_QK_DIM ** -0.5 * LOG2E))
        kb = dk.astype(BF16)
        vb = dv.astype(BF16)
        o = softmax_pv(q1, kb, vb) - lam * softmax_pv(q2, kb, vb)
        od_ref[:, cols] = _sub_ln(o, sg_ref[...], layer).astype(BF16)


def _attn_prompt(qkv, lam_p, subln, caches, layer):
    hp = PROMPT_HEADS
    h8 = N_HEADS // hp
    blk = lambda off: pl.BlockSpec((SEQ, hp * HEAD_DIM), lambda b, h: (b, off + h))
    in_specs = [blk(0), blk(h8), blk(2 * h8), blk(3 * h8), blk(4 * h8), blk(5 * h8),
                pl.BlockSpec((None, 4, DIFF_QK_DIM), lambda b, h: (layer, 0, 0)),
                pl.BlockSpec((None, 1, HEAD_DIM), lambda b, h: (layer, 0, 0))]
    args = [qkv] * 6 + [lam_p, subln]
    aliases = {}
    for n, cache in enumerate(caches):
        aliases[len(args)] = 2 + n
        in_specs.append(pl.BlockSpec(memory_space=pl.ANY))
        args.append(cache)
    cache_shape = jax.ShapeDtypeStruct((BATCH, DEPTH, N_HEADS, SEQ, HEAD_DIM), F32)
    cache_spec = pl.BlockSpec((None, None, hp, SEQ, HEAD_DIM), lambda b, h: (b, layer, h, 0, 0))
    merged_shape = jax.ShapeDtypeStruct((N_PROMPT, N_HEADS * HEAD_DIM), BF16)
    merged_spec = pl.BlockSpec((SEQ, hp * HEAD_DIM), lambda b, h: (b, h))
    out = pl.pallas_call(
        functools.partial(_attn_prompt_kernel, layer=layer),
        out_shape=(merged_shape, merged_shape,
                   cache_shape, cache_shape, cache_shape, cache_shape),
        grid=(BATCH, h8),
        in_specs=in_specs,
        out_specs=(merged_spec, merged_spec,
                   cache_spec, cache_spec, cache_spec, cache_spec),
        input_output_aliases=aliases,
        compiler_params=_params(32),
        name="attn_prompt",
    )(*args)
    return out[0], out[1], out[2:]


def _attn_latent_kernel(nq_ref, nk_ref, nv_ref, dq_ref, dk_ref, dv_ref,
                        cnk_ref, cnv_ref, cdk_ref, cdv_ref, tab_ref, cos_ref, sin_ref,
                        lam_ref, sg_ref, on_ref, od_ref,
                        kb, vb, ckb, cvb, q1b, q2b, *, layer):
    kb[...] = nk_ref[...].astype(BF16)
    vb[...] = nv_ref[...].astype(BF16)
    ckb[...] = cnk_ref[...].astype(BF16)
    cvb[...] = cnv_ref[...].astype(BF16)
    masked = jnp.full((GRID_W, 2 * GRID_W), MASK_VALUE, F32)

    def with_ones(v):
        return jnp.concatenate([v, jnp.ones(v.shape, BF16)], axis=1)

    left = lax.broadcasted_iota(jnp.int32, (GRID_W, 2 * GRID_W), 1) < GRID_W

    def bias_tile(r, kr):
        lo = min(max(r - WIN_H // 2, 0), GRID_H - WIN_H)
        ok0 = lo <= kr < lo + WIN_H
        ok1 = lo <= kr + 1 < lo + WIN_H
        if not (ok0 or ok1):
            return masked
        tile = tab_ref[kr - r + WIN_H]
        if ok0 and ok1:
            return tile
        return jnp.where(left if ok0 else jnp.logical_not(left), tile, masked)

    for blk in range(GRID_H // NAT_QROWS):
        r0 = blk * NAT_QROWS
        k0 = min(max(r0 - WIN_H // 2, 0), GRID_H - NAT_KROWS)
        rows = slice(r0 * GRID_W, (r0 + NAT_QROWS) * GRID_W)
        keys = slice(k0 * GRID_W, (k0 + NAT_KROWS) * GRID_W)
        q = (nq_ref[rows, :] * (HEAD_DIM ** -0.5 * LOG2E)).astype(BF16)
        bias = jnp.concatenate(
            [jnp.concatenate([bias_tile(r0 + a, k0 + 2 * i) for i in range(NAT_KROWS // 2)], axis=1)
             for a in range(NAT_QROWS)], axis=0)
        s_nb = lax.dot_general(q, kb[keys, :], NT_DIMS, preferred_element_type=F32) + bias
        s_cx = lax.dot_general(q, ckb[...], NT_DIMS, preferred_element_type=F32)
        m = jnp.maximum(jnp.max(s_nb, axis=-1, keepdims=True),
                        jnp.max(s_cx, axis=-1, keepdims=True))
        e_nb = jnp.exp2(s_nb - m).astype(BF16)
        e_cx = jnp.exp2(s_cx - m).astype(BF16)
        oz = (jnp.dot(e_nb, with_ones(vb[keys, :]), preferred_element_type=F32)
              + jnp.dot(e_cx, with_ones(cvb[...]), preferred_element_type=F32))
        on_ref[rows, :] = (oz[:, :HEAD_DIM] / oz[:, HEAD_DIM:]).astype(BF16)

    lane = lax.broadcasted_iota(jnp.int32, (DEC_SEQ, HEAD_DIM), 1)
    first_half = (lane & 16) == 0

    def rope(x):
        rot = jnp.where(first_half, pltpu.roll(x, HEAD_DIM - 16, 1), pltpu.roll(x, 16, 1))
        return x * cos_ref[...] + rot * sin_ref[...]

    q1, q2 = _split_q(rope(dq_ref[...]) * (DIFF_QK_DIM ** -0.5 * LOG2E))
    q1b[...] = q1
    q2b[...] = q2
    kb[...] = rope(dk_ref[...]).astype(BF16)
    vb[...] = dv_ref[...].astype(BF16)
    ckb[...] = cdk_ref[...].astype(BF16)
    cvb[...] = cdv_ref[...].astype(BF16)
    lam = _lambda(lam_ref, layer)

    def diff_block(j, carry):
        r0 = pl.multiple_of(j * Q_BLOCK, Q_BLOCK)

        def softmax_pv(q):
            s_l = lax.dot_general(q, kb[...], NT_DIMS, preferred_element_type=F32)
            s_c = lax.dot_general(q, ckb[...], NT_DIMS, preferred_element_type=F32)
            m = jnp.maximum(jnp.max(s_l, axis=-1, keepdims=True),
                            jnp.max(s_c, axis=-1, keepdims=True))
            e_l = jnp.exp2(s_l - m).astype(BF16)
            e_c = jnp.exp2(s_c - m).astype(BF16)
            oz = (jnp.dot(e_l, with_ones(vb[...]), preferred_element_type=F32)
                  + jnp.dot(e_c, with_ones(cvb[...]), preferred_element_type=F32))
            return oz[:, :HEAD_DIM] / oz[:, HEAD_DIM:]

        o = (softmax_pv(q1b[pl.ds(r0, Q_BLOCK), :])
             - lam * softmax_pv(q2b[pl.ds(r0, Q_BLOCK), :]))
        od_ref[pl.ds(r0, Q_BLOCK), :] = _sub_ln(o, sg_ref[...], layer).astype(BF16)
        return carry

    lax.fori_loop(0, DEC_SEQ // Q_BLOCK, diff_block, 0, unroll=True)


def _attn_latent(qkv, caches, tab, cos, sin, lam_p, subln, layer):
    h8 = N_HEADS
    row0 = N_PROMPT // DEC_SEQ
    blk = lambda off: pl.BlockSpec((DEC_SEQ, HEAD_DIM), lambda b, h: (row0 + b, off + h))
    cblk = pl.BlockSpec((None, None, None, PAST_LEN, HEAD_DIM), lambda b, h: (b, layer, h, 0, 0))
    fixed = pl.BlockSpec((DEC_SEQ, HEAD_DIM), lambda b, h: (0, 0))
    out_spec = pl.BlockSpec((DEC_SEQ, HEAD_DIM), lambda b, h: (b, h))
    merged_shape = jax.ShapeDtypeStruct((N_TOK - N_PROMPT, h8 * HEAD_DIM), BF16)
    return pl.pallas_call(
        functools.partial(_attn_latent_kernel, layer=layer),
        out_shape=(merged_shape, merged_shape),
        grid=(DEC_BATCH, h8),
        in_specs=[blk(0), blk(h8), blk(2 * h8), blk(3 * h8), blk(4 * h8), blk(5 * h8),
                  cblk, cblk, cblk, cblk,
                  pl.BlockSpec((None, 2 * WIN_H, GRID_W, 2 * GRID_W), lambda b, h: (h, 0, 0, 0)),
                  fixed, fixed,
                  pl.BlockSpec((None, 4, DIFF_QK_DIM), lambda b, h: (layer, 0, 0)),
                  pl.BlockSpec((None, 1, HEAD_DIM), lambda b, h: (layer, 0, 0))],
        out_specs=(out_spec, out_spec),
        scratch_shapes=[pltpu.VMEM((DEC_SEQ, HEAD_DIM), BF16), pltpu.VMEM((DEC_SEQ, HEAD_DIM), BF16),
                        pltpu.VMEM((PAST_LEN, HEAD_DIM), BF16), pltpu.VMEM((PAST_LEN, HEAD_DIM), BF16),
                        pltpu.VMEM((DEC_SEQ, HEAD_DIM), BF16), pltpu.VMEM((DEC_SEQ, HEAD_DIM), BF16)],
        compiler_params=_params(48),
        name="attn_latent",
    )(qkv, qkv, qkv, qkv, qkv, qkv, *caches, tab, cos, sin, lam_p, subln)


def _route(logits, bias):
    ex = jnp.exp(logits - jnp.max(logits, axis=0, keepdims=True))
    probs = ex / jnp.sum(ex, axis=0, keepdims=True)
    sel = probs + bias
    srow = [sel[e:e + 1, :] for e in range(N_EXPERTS)]
    prow = [probs[e:e + 1, :] for e in range(N_EXPERTS)]
    gscore = []
    for g in range(N_GROUPS):
        v = srow[g * EXPERTS_PER_GROUP:(g + 1) * EXPERTS_PER_GROUP]
        best = None
        for a in range(EXPERTS_PER_GROUP):
            for b in range(a + 1, EXPERTS_PER_GROUP):
                pair = v[a] + v[b]
                best = pair if best is None else jnp.maximum(best, pair)
        gscore.append(best)
    gbest = gscore[0]
    gidx = jnp.zeros(gbest.shape, jnp.int32)
    for g in range(1, N_GROUPS):
        better = gscore[g] > gbest
        gidx = jnp.where(better, g, gidx)
        gbest = jnp.where(better, gscore[g], gbest)
    neg = jnp.full(gbest.shape, -jnp.inf, F32)
    picks = []
    taken = None
    for _ in range(2):
        best = neg
        idx = jnp.zeros(gbest.shape, jnp.int32)
        wgt = jnp.zeros(gbest.shape, F32)
        for e in range(N_EXPERTS):
            ok = gidx == (e // EXPERTS_PER_GROUP)
            if taken is not None:
                ok = jnp.logical_and(ok, taken != e)
            cand = jnp.where(ok, srow[e], neg)
            better = cand > best
            idx = jnp.where(better, e, idx)
            wgt = jnp.where(better, prow[e], wgt)
            best = jnp.where(better, cand, best)
        picks.append((idx, wgt))
        taken = idx
    (i0, w0), (i1, w1) = picks
    wsum = w0 + w1
    return i0, i1, w0 / wsum, w1 / wsum


def _split_bf16(v):
    hi = v.astype(BF16)
    lo = (v - hi.astype(F32)).astype(BF16)
    return hi, lo


def _load_resident_weight(w_hbm, wbf, stage, sem, n_chunks):
    ck = wbf.shape[0] // n_chunks
    copies = [pltpu.make_async_copy(w_hbm.at[pl.ds(c * ck, ck), :], stage.at[c % 2], sem.at[c % 2])
              for c in range(n_chunks)]
    copies[0].start()
    for c in range(n_chunks):
        if c + 1 < n_chunks:
            copies[c + 1].start()
        copies[c].wait()
        wbf[c * ck:(c + 1) * ck, :] = stage[c % 2].astype(BF16)


def _out_proj_kernel(pn_ref, pd_ref, sn_ref, sd_ref, w_hbm, *rest, layer, split_x):
    if split_x:
        x_in = _select_stream(rest[0], rest[1], OUTPROJ_TM)
        rest = rest[2:]
    else:
        x_in = rest[0][...]
        rest = rest[1:]
    (mod_ref, g_ref, wr_ref, rb_ref, xo_ref, h_ref, idx_ref, cw_ref, wbf, stage, mbuf, sem) = rest
    i = pl.program_id(0)
    half = D_MODEL // 2

    @pl.when(i == 0)
    def _():
        _load_resident_weight(w_hbm.at[layer], wbf, stage, sem, OUTPROJ_WCHUNKS)

    prompt = i < N_PROMPT // OUTPROJ_TM

    @pl.when(prompt)
    def _():
        mbuf[:, :half] = pn_ref[...]
        mbuf[:, half:] = pd_ref[...]

    @pl.when(jnp.logical_not(prompt))
    def _():
        mbuf[:, :half] = sn_ref[...]
        mbuf[:, half:] = sd_ref[...]

    a = jnp.dot(mbuf[...], wbf[...], preferred_element_type=F32)
    x = x_in + mod_ref[2:3, :] * a
    xo_ref[...] = x
    h = _rms(x, g_ref[...]) * (1.0 + mod_ref[4:5, :]) + mod_ref[3:4, :]
    h_ref[...] = h
    w_hi, w_lo = _split_bf16(wr_ref[...])
    h_hi, h_lo = _split_bf16(h)
    part = lax.dot_general(jnp.concatenate([w_hi, w_lo], axis=0), h_hi, NT_DIMS,
                           preferred_element_type=F32)
    logits = (part[:N_EXPERTS] + part[N_EXPERTS:]
              + lax.dot_general(w_hi, h_lo, NT_DIMS, preferred_element_type=F32))
    i0, i1, w0, w1 = _route(logits, rb_ref[...])
    idx_ref[...] = jnp.concatenate([i0, i1], axis=0)
    cw_ref[...] = jnp.concatenate([w0, w1], axis=0)


def _out_proj(merged, w_out, x, mods, layer, gain, w_router_t, router_bias):
    tm = OUTPROJ_TM
    split_x = isinstance(x, tuple)
    x_args = list(x) if split_x else [x]
    x_specs = _stream_specs(tm) if split_x else [pl.BlockSpec((tm, D_MODEL), lambda i: (i, 0))]
    half = D_MODEL // 2
    npt = N_PROMPT // tm
    row = lambda i: (i, 0)
    fixed = lambda i: (0, 0)
    p_row = lambda i: (jnp.minimum(i, npt - 1), 0)
    s_row = lambda i: (jnp.maximum(i - npt, 0), 0)
    return pl.pallas_call(
        functools.partial(_out_proj_kernel, layer=layer, split_x=split_x),
        out_shape=(jax.ShapeDtypeStruct((N_TOK, D_MODEL), F32),
                   jax.ShapeDtypeStruct((N_TOK, D_MODEL), F32),
                   jax.ShapeDtypeStruct((2, N_TOK), jnp.int32),
                   jax.ShapeDtypeStruct((2, N_TOK), F32)),
        grid=(N_TOK // tm,),
        in_specs=[
            pl.BlockSpec((tm, half), p_row),
            pl.BlockSpec((tm, half), p_row),
            pl.BlockSpec((tm, half), s_row),
            pl.BlockSpec((tm, half), s_row),
            pl.BlockSpec(memory_space=pl.ANY),
        ] + x_specs + [
            pl.BlockSpec((None, None, 6, D_MODEL), lambda i: (layer, _mod_row(i, tm), 0, 0)),
            pl.BlockSpec((1, D_MODEL), fixed),
            pl.BlockSpec((N_EXPERTS, D_MODEL), fixed),
            pl.BlockSpec((N_EXPERTS, 1), fixed),
        ],
        out_specs=(pl.BlockSpec((tm, D_MODEL), row), pl.BlockSpec((tm, D_MODEL), row),
                   pl.BlockSpec((2, tm), lambda i: (0, i)),
                   pl.BlockSpec((2, tm), lambda i: (0, i))),
        scratch_shapes=[pltpu.VMEM((D_MODEL, D_MODEL), BF16),
                        pltpu.VMEM((2, D_MODEL // OUTPROJ_WCHUNKS, D_MODEL), F32),
                        pltpu.VMEM((tm, D_MODEL), BF16),
                        pltpu.SemaphoreType.DMA((2,))],
        compiler_params=_params(48),
        name="out_proj_router",
    )(*merged, w_out, *x_args, mods, gain, w_router_t, router_bias)


def _dispatch_kernel(pos_ref, h_ref, xs_in, xs_out, sem):
    del xs_in
    tm = DISPATCH_TM
    base = pl.program_id(0) * tm

    def issue(r, carry):
        for k in range(2):
            p = pos_ref[k * N_TOK + base + r]
            pltpu.make_async_copy(h_ref.at[pl.ds(r, 1), :], xs_out.at[pl.ds(p, 1), :], sem).start()
        return carry

    lax.fori_loop(0, tm, issue, 0, unroll=8)
    for _ in range(2):
        pltpu.make_async_copy(h_ref, xs_out.at[pl.ds(0, tm), :], sem).wait()


def _dispatch(pos, h2, xs_buf):
    tm = DISPATCH_TM
    return pl.pallas_call(
        _dispatch_kernel,
        out_shape=jax.ShapeDtypeStruct((EXPERT_CAP, D_MODEL), F32),
        grid_spec=pltpu.PrefetchScalarGridSpec(
            num_scalar_prefetch=1,
            grid=(N_TOK // tm,),
            in_specs=[pl.BlockSpec((tm, D_MODEL), lambda i, pos_ref: (i, 0)),
                      pl.BlockSpec(memory_space=pl.ANY)],
            out_specs=pl.BlockSpec(memory_space=pl.ANY),
            scratch_shapes=[pltpu.SemaphoreType.DMA(())],
        ),
        input_output_aliases={2: 0},
        compiler_params=_params(32),
        name="expert_dispatch",
    )(pos, h2, xs_buf)


def _experts_kernel(we_ref, wc_ref, wdo_ref, wsl_ref, ti_ref, tdo_ref, tsl_ref,
                    xs_ref, wg_ref, wu_ref, wd_ref, o_ref, wg_bf, wu_bf, wd_bf):
    del we_ref, ti_ref
    s = pl.program_id(0)

    @pl.when(wdo_ref[s] == 1)
    def _():
        slot = wsl_ref[s]
        c = wc_ref[s]
        ck = D_MODEL // EXPERT_WCHUNKS
        r0 = pl.multiple_of(c * ck, ck)
        wg_bf[slot, pl.ds(r0, ck), :] = wg_ref[...].astype(BF16)
        wu_bf[slot, pl.ds(r0, ck), :] = wu_ref[...].astype(BF16)
        cf = D_FF // EXPERT_WCHUNKS
        r1 = pl.multiple_of(c * cf, cf)
        wd_bf[slot, pl.ds(r1, cf), :] = wd_ref[...].astype(BF16)

    @pl.when(tdo_ref[s] == 1)
    def _():
        slot = tsl_ref[s]
        xb = xs_ref[...].astype(BF16)
        g = jnp.dot(xb, wg_bf[slot], preferred_element_type=F32)
        u = jnp.dot(xb, wu_bf[slot], preferred_element_type=F32)
        hid = (g * jax.nn.sigmoid(g) * u).astype(BF16)
        o_ref[...] = jnp.dot(hid, wd_bf[slot], preferred_element_type=F32)

    @pl.when(tdo_ref[s] == 2)
    def _():
        o_ref[...] = jnp.zeros_like(o_ref)


def _experts(plan, xs, w_gate, w_up, w_down, layer):
    tm = EXPERT_TM
    ck = D_MODEL // EXPERT_WCHUNKS
    cf = D_FF // EXPERT_WCHUNKS
    tile = lambda s, we, wc, wdo, wsl, ti, tdo, tsl: (ti[s], 0)
    wblk = lambda s, we, wc, wdo, wsl, ti, tdo, tsl: (layer, we[s], wc[s], 0)
    return pl.pallas_call(
        _experts_kernel,
        out_shape=jax.ShapeDtypeStruct((EXPERT_CAP, D_MODEL), F32),
        grid_spec=pltpu.PrefetchScalarGridSpec(
            num_scalar_prefetch=7,
            grid=(EXPERT_STEPS,),
            in_specs=[
                pl.BlockSpec((tm, D_MODEL), tile),
                pl.BlockSpec((None, None, ck, D_FF), wblk),
                pl.BlockSpec((None, None, ck, D_FF), wblk),
                pl.BlockSpec((None, None, cf, D_MODEL), wblk),
            ],
            out_specs=pl.BlockSpec((tm, D_MODEL), tile),
            scratch_shapes=[pltpu.VMEM((2, D_MODEL, D_FF), BF16),
                            pltpu.VMEM((2, D_MODEL, D_FF), BF16),
                            pltpu.VMEM((2, D_FF, D_MODEL), BF16)],
        ),
        compiler_params=_params(56),
        name="expert_mlp",
    )(*plan, xs, w_gate, w_up, w_down)


def _dispatch_plan(idx):
    e = idx.reshape(-1)
    onehot = (e[:, None] == jnp.arange(N_EXPERTS, dtype=jnp.int32)[None, :]).astype(jnp.int32)
    csum = jnp.cumsum(onehot, axis=0)
    counts = csum[-1]
    rank = jnp.sum(onehot * csum, axis=1) - 1
    nt = (counts + EXPERT_TM - 1) // EXPERT_TM
    tile_end = jnp.cumsum(nt)
    tile_start = tile_end - nt
    pos = jnp.sum(onehot * (tile_start * EXPERT_TM)[None, :], axis=1) + rank

    nw = EXPERT_WCHUNKS
    length = jnp.maximum(nt, nw)
    phase_end = nw + jnp.cumsum(length)
    phase_start = phase_end - length
    zero = jnp.zeros((1,), jnp.int32)
    s = jnp.arange(EXPERT_STEPS, dtype=jnp.int32)
    ph = jnp.sum((s[:, None] >= phase_start[None, :]).astype(jnp.int32), axis=1)
    ex = ph - 1
    k = s - jnp.concatenate([zero, phase_start])[ph]
    nt_s = jnp.concatenate([zero, nt])[ph]
    ts_s = jnp.concatenate([zero, tile_start])[ph]
    live = s < phase_end[-1]
    last = N_EXPERTS - 1
    w_do = (live & (k < nw) & (ex < last)).astype(jnp.int32)
    w_e = jnp.minimum(ex + 1, last)
    w_c = jnp.where(ex < last, jnp.minimum(k, nw - 1), nw - 1)
    w_slot = (ex + 1) % 2
    t_do = (live & (k < nt_s)).astype(jnp.int32)
    t_idx = jnp.maximum(ts_s + jnp.minimum(k + 1, nt_s) - 1, 0)
    z = tile_end[-1] + (s - phase_end[-1])
    t_do = jnp.where(jnp.logical_not(live) & (z < EXPERT_TILES), 2, t_do)
    t_idx = jnp.where(live, t_idx, jnp.minimum(z, EXPERT_TILES - 1))
    t_slot = jnp.maximum(ex, 0) % 2
    plan = tuple(a.astype(jnp.int32) for a in (w_e, w_c, w_do, w_slot, t_idx, t_do, t_slot))
    return pos.astype(jnp.int32), plan


def _bias_tables(nat_rpb):
    w = GRID_W
    c = jnp.arange(w)[:, None]
    kc = jnp.arange(w)[None, :]
    qstart = jnp.clip(c - WIN_W // 2, 0, w - WIN_W)
    valid = (kc >= qstart) & (kc < qstart + WIN_W)
    lead = nat_rpb.shape[:-1]
    pad_l = w - WIN_W
    row = jnp.pad(nat_rpb, [(0, 0)] * len(lead) + [(pad_l, 2 * w - pad_l - nat_rpb.shape[-1])])
    skew = jnp.tile(row, w)[..., :w * (2 * w - 1)].reshape(lead + (w, 2 * w - 1))
    tab = jnp.where(valid, skew[..., w - 1:2 * w - 1] * LOG2E, MASK_VALUE)
    tab = jnp.pad(tab, [(0, 0), (0, 0), (1, 1), (0, 0), (0, 0)], constant_values=MASK_VALUE)
    return jnp.concatenate([tab[:, :, :-1], tab[:, :, 1:]], axis=-1)


def _rope_tables():
    t = jnp.arange(DEC_SEQ)
    row = (t // GRID_W).astype(F32)
    col = (t % GRID_W).astype(F32)
    nf = DIFF_QK_DIM // 4
    inv = ROPE_BASE ** (-jnp.arange(nf, dtype=F32) / nf)
    ar = row[:, None] * inv[None, :]
    ac = col[:, None] * inv[None, :]
    cos = jnp.concatenate([jnp.cos(ar), jnp.cos(ar), jnp.cos(ac), jnp.cos(ac)], axis=-1)
    sin = jnp.concatenate([-jnp.sin(ar), jnp.sin(ar), -jnp.sin(ac), jnp.sin(ac)], axis=-1)
    return jnp.tile(cos, (1, 2)), jnp.tile(sin, (1, 2))


def kernel(x_prompt, x_sample, cache_nat_k, cache_nat_v, cache_diff_k, cache_diff_v, c, c_ctx,
           w_ada, b_ada, norm1, norm2, norm_final, w_in, w_out, nat_rpb, diff_lambda, diff_subln,
           w_router, router_bias, w_gate, w_up, w_down):
    x = (x_prompt.reshape(N_PROMPT, D_MODEL), x_sample.reshape(DEC_BATCH * DEC_SEQ, D_MODEL))
    cond =jnp.concatenate([c_ctx[None, :], c, jnp.zeros((8 - 1 - DEC_BATCH, D_MODEL), F32)], axis=0)
    mods = _modulation(cond, w_ada, b_ada)
    tabs = _bias_tables(nat_rpb)
    cos, sin = _rope_tables()
    subln = diff_subln.reshape(DEPTH, 1, HEAD_DIM)
    w_router_t = w_router.T
    rbias = router_bias.reshape(N_EXPERTS, 1)
    lat_caches = (cache_nat_k, cache_nat_v, cache_diff_k, cache_diff_v)

    new_caches = tuple(jnp.zeros((BATCH, DEPTH, N_HEADS, SEQ, HEAD_DIM), F32) for _ in range(4))
    xs_buf = jnp.zeros((EXPERT_CAP, D_MODEL), F32)
    ys = pos = cw = None
    for layer in range(DEPTH):
        gain1 = norm1[layer].reshape(1, D_MODEL)
        if layer == 0:
            h = _pre_first(*x, mods, layer, gain1)
        else:
            x, h = _pre_combine(x, ys, pos, cw, mods, layer - 1, gain1, final=False)
        qkv = _in_proj(h, w_in, layer)
        pn, pd, new_caches = _attn_prompt(qkv, diff_lambda, subln, new_caches, layer)
        sn, sd = _attn_latent(qkv, lat_caches, tabs[layer], cos, sin, diff_lambda, subln, layer)
        x, h2, idx, cwt = _out_proj((pn, pd, sn, sd), w_out, x, mods, layer,
                                    norm2[layer].reshape(1, D_MODEL), w_router_t, rbias)
        pos, plan = _dispatch_plan(idx)
        cw = cwt.T
        xs_buf = _dispatch(pos, h2, xs_buf)
        ys = _experts(plan, xs_buf, w_gate, w_up, w_down, layer)
    yp, ysm = _pre_combine(x, ys, pos, cw, mods, DEPTH - 1, norm_final.reshape(1, D_MODEL), final=True)
    return (yp.reshape(BATCH, SEQ, D_MODEL), ysm.reshape(DEC_BATCH, DEC_SEQ, D_MODEL)) + tuple(new_caches)
```

```python
import functools
import math

import jax
import jax.numpy as jnp
from jax import lax
from jax.experimental import pallas as pl
from jax.experimental.pallas import tpu as pltpu

D_MODEL = 2048
BATCH = 16
SEQ = 256
DEPTH = 4
DEC_BATCH = 4
DEC_SEQ = 1024
PAST_LEN = 512
GRID_W = 64
GRID_H = DEC_SEQ // GRID_W
N_HEADS = 8
HEAD_DIM = 128
DIFF_QK_DIM = 64
WIN_H = 8
WIN_W = 16
N_EXPERTS = 16
N_GROUPS = 4
EXPERTS_PER_GROUP = N_EXPERTS // N_GROUPS
D_FF = 1024
ROPE_BASE = 10000.0
EPS = 1e-6
IN_COLS = 6 * N_HEADS * HEAD_DIM
N_PROMPT = BATCH * SEQ
N_TOK = N_PROMPT + DEC_BATCH * DEC_SEQ
N_PAIRS = 2 * N_TOK

PRE_TM = 256
INPROJ_TM = 2048
INPROJ_TN = 512
OUTPROJ_TM = 256
OUTPROJ_WCHUNKS = 4
DISPATCH_TM = 256
EXPERT_TM = 256
EXPERT_CAP = N_PAIRS + N_EXPERTS * EXPERT_TM
EXPERT_TILES = EXPERT_CAP // EXPERT_TM
EXPERT_WCHUNKS = 4
EXPERT_STEPS = EXPERT_WCHUNKS * (N_EXPERTS + 1) + EXPERT_TILES
PLAN_BLOCK = 128
MOD_TN = 1024
Q_BLOCK = 256
PROMPT_HEADS = 4
NAT_QROWS = 4
NAT_KROWS = 12
LOG2E = 1.4426950408889634
MASK_VALUE = -1e30

F32 = jnp.float32
BF16 = jnp.bfloat16
MIB = 1024 * 1024
NT_DIMS = (((1,), (1,)), ((), ()))


def _params(vmem_mib):
    return pltpu.CompilerParams(vmem_limit_bytes=vmem_mib * MIB)


def _lam_init(layer):
    return 0.8 - 0.6 * math.exp(-0.3 * layer)


def _rms(x, gain):
    return x * lax.rsqrt(jnp.mean(x * x, axis=-1, keepdims=True) + EPS) * gain


def _mod_row(i, tm):
    first = i * tm
    return jnp.where(first < N_PROMPT, 0, 1 + (first - N_PROMPT) // DEC_SEQ)


def _mod_kernel(cond_ref, w_ref, b_ref, o_ref):
    c = cond_ref[...]
    s = c * jax.nn.sigmoid(c)
    o_ref[...] = jnp.dot(s.astype(BF16), w_ref[...].astype(BF16),
                         preferred_element_type=F32) + b_ref[...]


def _modulation(cond, w_ada, b_ada):
    n = 6 * D_MODEL
    out = pl.pallas_call(
        _mod_kernel,
        out_shape=jax.ShapeDtypeStruct((DEPTH, 8, n), F32),
        grid=(DEPTH, n // MOD_TN),
        in_specs=[
            pl.BlockSpec((8, D_MODEL), lambda l, j: (0, 0)),
            pl.BlockSpec((None, D_MODEL, MOD_TN), lambda l, j: (l, 0, j)),
            pl.BlockSpec((None, 1, MOD_TN), lambda l, j: (l, 0, j)),
        ],
        out_specs=pl.BlockSpec((None, 8, MOD_TN), lambda l, j: (l, 0, j)),
        compiler_params=_params(40),
        name="adaln_modulation",
    )(cond, w_ada, b_ada.reshape(DEPTH, 1, n))
    return out.reshape(DEPTH, 8, 6, D_MODEL)


def _gather_expert_rows(pos_ref, ys_hbm, ybuf, sem, tm):
    i = pl.program_id(0)
    n = pl.num_programs(0)

    def issue_tile(tile, slot):
        for r in range(tm):
            for k in range(2):
                p = pos_ref[k * N_TOK + tile * tm + r]
                pltpu.make_async_copy(ys_hbm.at[pl.ds(p, 1), :],
                                      ybuf.at[slot, k, pl.ds(r, 1), :], sem.at[slot, k]).start()

    @pl.when(i == 0)
    def _():
        issue_tile(0, 0)

    @pl.when(i + 1 < n)
    def _():
        issue_tile(i + 1, (i + 1) % 2)

    slot = i % 2
    for k in range(2):
        pltpu.make_async_copy(ys_hbm.at[pl.ds(0, tm), :], ybuf.at[slot, k], sem.at[slot, k]).wait()
    return slot


def _select_stream(xp_ref, xs_ref, tm):
    prompt = pl.program_id(0) < N_PROMPT // tm
    return jnp.where(prompt, xp_ref[...], xs_ref[...])


def _pre_first_kernel(xp_ref, xs_ref, mod_ref, g_ref, h_ref):
    x = _select_stream(xp_ref, xs_ref, PRE_TM)
    h = _rms(x, g_ref[...]) * (1.0 + mod_ref[1:2, :]) + mod_ref[0:1, :]
    h_ref[...] = h.astype(BF16)


def _pre_mid_kernel(pos_ref, x_ref, ys_hbm, cw_ref, modp_ref, mod_ref, g_ref,
                    xo_ref, h_ref, ybuf, sem):
    slot = _gather_expert_rows(pos_ref, ys_hbm, ybuf, sem, PRE_TM)
    cw = cw_ref[...]
    y = cw[:, 0:1] * ybuf[slot, 0] + cw[:, 1:2] * ybuf[slot, 1]
    x = x_ref[...] + modp_ref[5:6, :] * y
    xo_ref[...] = x
    h = _rms(x, g_ref[...]) * (1.0 + mod_ref[1:2, :]) + mod_ref[0:1, :]
    h_ref[...] = h.astype(BF16)


def _pre_final_kernel(pos_ref, x_ref, ys_hbm, cw_ref, modp_ref, g_ref, yp_ref, ys_ref, ybuf, sem):
    slot = _gather_expert_rows(pos_ref, ys_hbm, ybuf, sem, PRE_TM)
    cw = cw_ref[...]
    y = cw[:, 0:1] * ybuf[slot, 0] + cw[:, 1:2] * ybuf[slot, 1]
    x = x_ref[...] + modp_ref[5:6, :] * y
    out = _rms(x, g_ref[...])
    prompt = pl.program_id(0) < N_PROMPT // PRE_TM

    @pl.when(prompt)
    def _():
        yp_ref[...] = out

    @pl.when(jnp.logical_not(prompt))
    def _():
        ys_ref[...] = out


def _stream_specs(tm):
    npt = N_PROMPT // tm
    return [pl.BlockSpec((tm, D_MODEL), lambda i, *_: (jnp.minimum(i, npt - 1), 0)),
            pl.BlockSpec((tm, D_MODEL), lambda i, *_: (jnp.maximum(i - npt, 0), 0))]


def _pre_first(xp, xs, mods, layer, gain):
    tm = PRE_TM
    return pl.pallas_call(
        _pre_first_kernel,
        out_shape=jax.ShapeDtypeStruct((N_TOK, D_MODEL), BF16),
        grid=(N_TOK // tm,),
        in_specs=_stream_specs(tm) + [
            pl.BlockSpec((None, None, 6, D_MODEL), lambda i: (layer, _mod_row(i, tm), 0, 0)),
            pl.BlockSpec((1, D_MODEL), lambda i: (0, 0)),
        ],
        out_specs=pl.BlockSpec((tm, D_MODEL), lambda i: (i, 0)),
        compiler_params=_params(32),
        name="pre_first",
    )(xp, xs, mods, gain)


def _pre_combine(x, ys, pos, cw, mods, prev_layer, gain, final):
    tm = PRE_TM
    row = lambda i, pos_ref: (i, 0)
    fixed = lambda i, pos_ref: (0, 0)
    in_specs = [
        pl.BlockSpec((tm, D_MODEL), row),
        pl.BlockSpec(memory_space=pl.ANY),
        pl.BlockSpec((tm, 2), row),
        pl.BlockSpec((None, None, 6, D_MODEL),
                     lambda i, pos_ref: (prev_layer, _mod_row(i, tm), 0, 0)),
    ]
    args = [x, ys, cw, mods]
    if final:
        kern = _pre_final_kernel
        out_shape = (jax.ShapeDtypeStruct((N_PROMPT, D_MODEL), F32),
                     jax.ShapeDtypeStruct((N_TOK - N_PROMPT, D_MODEL), F32))
        out_specs = tuple(_stream_specs(tm))
    else:
        kern = _pre_mid_kernel
        in_specs.append(pl.BlockSpec((None, None, 6, D_MODEL),
                                     lambda i, pos_ref: (prev_layer + 1, _mod_row(i, tm), 0, 0)))
        args.append(mods)
        out_shape = (jax.ShapeDtypeStruct((N_TOK, D_MODEL), F32),
                     jax.ShapeDtypeStruct((N_TOK, D_MODEL), BF16))
        out_specs = (pl.BlockSpec((tm, D_MODEL), row), pl.BlockSpec((tm, D_MODEL), row))
    in_specs.append(pl.BlockSpec((1, D_MODEL), fixed))
    args.append(gain)
    return pl.pallas_call(
        kern,
        out_shape=out_shape,
        grid_spec=pltpu.PrefetchScalarGridSpec(
            num_scalar_prefetch=1,
            grid=(N_TOK // tm,),
            in_specs=in_specs,
            out_specs=out_specs,
            scratch_shapes=[pltpu.VMEM((2, 2, tm, D_MODEL), F32), pltpu.SemaphoreType.DMA((2, 2))],
        ),
        compiler_params=_params(40),
        name="pre_final" if final else "pre_combine",
    )(pos, *args)


def _in_proj_kernel(h_ref, w_ref, o_ref):
    o_ref[...] = jnp.dot(h_ref[...], w_ref[...].astype(BF16), preferred_element_type=F32)


def _in_proj(h, w_in, layer):
    tm, tn = INPROJ_TM, INPROJ_TN
    return pl.pallas_call(
        _in_proj_kernel,
        out_shape=jax.ShapeDtypeStruct((N_TOK, IN_COLS), F32),
        grid=(N_TOK // tm, IN_COLS // tn),
        in_specs=[
            pl.BlockSpec((tm, D_MODEL), lambda i, j: (i, 0)),
            pl.BlockSpec((None, D_MODEL, tn), lambda i, j: (layer, 0, j)),
        ],
        out_specs=pl.BlockSpec((tm, tn), lambda i, j: (i, j)),
        compiler_params=_params(48),
        name="in_proj",
    )(h, w_in)


def _lambda(lam_ref, layer):
    lp = lam_ref[...]
    a = jnp.sum(lp[0:1, :] * lp[1:2, :], axis=-1, keepdims=True)
    b = jnp.sum(lp[2:3, :] * lp[3:4, :], axis=-1, keepdims=True)
    return jnp.exp(a) - jnp.exp(b) + _lam_init(layer)


def _split_q(q):
    lane = lax.broadcasted_iota(jnp.int32, q.shape, 1)
    q1 = jnp.where(lane < DIFF_QK_DIM, q, 0.0).astype(BF16)
    q2 = jnp.where(lane >= DIFF_QK_DIM, q, 0.0).astype(BF16)
    return q1, q2


def _sub_ln(o, gain, layer):
    return _rms(o, gain) * (1.0 - _lam_init(layer))


def _attn_prompt_kernel(nq_ref, nk_ref, nv_ref, dq_ref, dk_ref, dv_ref, lam_ref, sg_ref,
                        c0_in, c1_in, c2_in, c3_in,
                        on_ref, od_ref, onk_ref, onv_ref, odk_ref, odv_ref, *, layer):
    del c0_in, c1_in, c2_in, c3_in
    ones = jnp.ones((SEQ, HEAD_DIM), BF16)
    lam = _lambda(lam_ref, layer)

    def softmax_pv(q, k, v):
        s = lax.dot_general(q, k, NT_DIMS, preferred_element_type=F32)
        e = jnp.exp2(s - jnp.max(s, axis=-1, keepdims=True)).astype(BF16)
        oz = jnp.dot(e, jnp.concatenate([v, ones], axis=1), preferred_element_type=F32)
        return oz[:, :HEAD_DIM] / oz[:, HEAD_DIM:]

    for hh in range(PROMPT_HEADS):
        cols = slice(hh * HEAD_DIM, (hh + 1) * HEAD_DIM)
        nk = nk_ref[:, cols]
        nv = nv_ref[:, cols]
        dk = dk_ref[:, cols]
        dv = dv_ref[:, cols]
        onk_ref[hh] = nk
        onv_ref[hh] = nv
        odk_ref[hh] = dk
        odv_ref[hh] = dv
        q = (nq_ref[:, cols] * (HEAD_DIM ** -0.5 * LOG2E)).astype(BF16)
        on_ref[:, cols] = softmax_pv(q, nk.astype(BF16), nv.astype(BF16)).astype(BF16)
        q1, q2 = _split_q(dq_ref[:, cols] * (DIFF_QK_DIM ** -0.5 * LOG2E))
        kb = dk.astype(BF16)
        vb = dv.astype(BF16)
        o = softmax_pv(q1, kb, vb) - lam * softmax_pv(q2, kb, vb)
        od_ref[:, cols] = _sub_ln(o, sg_ref[...], layer).astype(BF16)


def _attn_prompt(qkv, lam_p, subln, caches, layer):
    hp = PROMPT_HEADS
    h8 = N_HEADS // hp
    blk = lambda off: pl.BlockSpec((SEQ, hp * HEAD_DIM), lambda b, h: (b, off + h))
    in_specs = [blk(0), blk(h8), blk(2 * h8), blk(3 * h8), blk(4 * h8), blk(5 * h8),
                pl.BlockSpec((None, 4, DIFF_QK_DIM), lambda b, h: (layer, 0, 0)),
                pl.BlockSpec((None, 1, HEAD_DIM), lambda b, h: (layer, 0, 0))]
    args = [qkv] * 6 + [lam_p, subln]
    aliases = {}
    for n, cache in enumerate(caches):
        aliases[len(args)] = 2 + n
        in_specs.append(pl.BlockSpec(memory_space=pl.ANY))
        args.append(cache)
    cache_shape = jax.ShapeDtypeStruct((BATCH, DEPTH, N_HEADS, SEQ, HEAD_DIM), F32)
    cache_spec = pl.BlockSpec((None, None, hp, SEQ, HEAD_DIM), lambda b, h: (b, layer, h, 0, 0))
    merged_shape = jax.ShapeDtypeStruct((N_PROMPT, N_HEADS * HEAD_DIM), BF16)
    merged_spec = pl.BlockSpec((SEQ, hp * HEAD_DIM), lambda b, h: (b, h))
    out = pl.pallas_call(
        functools.partial(_attn_prompt_kernel, layer=layer),
        out_shape=(merged_shape, merged_shape,
                   cache_shape, cache_shape, cache_shape, cache_shape),
        grid=(BATCH, h8),
        in_specs=in_specs,
        out_specs=(merged_spec, merged_spec,
                   cache_spec, cache_spec, cache_spec, cache_spec),
        input_output_aliases=aliases,
        compiler_params=_params(32),
        name="attn_prompt",
    )(*args)
    return out[0], out[1], out[2:]


def _attn_latent_kernel(nq_ref, nk_ref, nv_ref, dq_ref, dk_ref, dv_ref,
                        cnk_ref, cnv_ref, cdk_ref, cdv_ref, tab_ref, cos_ref, sin_ref,
                        lam_ref, sg_ref, on_ref, od_ref,
                        kb, vb, ckb, cvb, q1b, q2b, *, layer):
    kb[...] = nk_ref[...].astype(BF16)
    vb[...] = nv_ref[...].astype(BF16)
    ckb[...] = cnk_ref[...].astype(BF16)
    cvb[...] = cnv_ref[...].astype(BF16)
    masked = jnp.full((GRID_W, 2 * GRID_W), MASK_VALUE, F32)

    def with_ones(v):
        return jnp.concatenate([v, jnp.ones(v.shape, BF16)], axis=1)

    left = lax.broadcasted_iota(jnp.int32, (GRID_W, 2 * GRID_W), 1) < GRID_W

    def bias_tile(r, kr):
        lo = min(max(r - WIN_H // 2, 0), GRID_H - WIN_H)
        ok0 = lo <= kr < lo + WIN_H
        ok1 = lo <= kr + 1 < lo + WIN_H
        if not (ok0 or ok1):
            return masked
        tile = tab_ref[kr - r + WIN_H]
        if ok0 and ok1:
            return tile
        return jnp.where(left if ok0 else jnp.logical_not(left), tile, masked)

    for blk in range(GRID_H // NAT_QROWS):
        r0 = blk * NAT_QROWS
        k0 = min(max(r0 - WIN_H // 2, 0), GRID_H - NAT_KROWS)
        rows = slice(r0 * GRID_W, (r0 + NAT_QROWS) * GRID_W)
        keys = slice(k0 * GRID_W, (k0 + NAT_KROWS) * GRID_W)
        q = (nq_ref[rows, :] * (HEAD_DIM ** -0.5 * LOG2E)).astype(BF16)
        bias = jnp.concatenate(
            [jnp.concatenate([bias_tile(r0 + a, k0 + 2 * i) for i in range(NAT_KROWS // 2)], axis=1)
             for a in range(NAT_QROWS)], axis=0)
        s_nb = lax.dot_general(q, kb[keys, :], NT_DIMS, preferred_element_type=F32) + bias
        s_cx = lax.dot_general(q, ckb[...], NT_DIMS, preferred_element_type=F32)
        m = jnp.maximum(jnp.max(s_nb, axis=-1, keepdims=True),
                        jnp.max(s_cx, axis=-1, keepdims=True))
        e_nb = jnp.exp2(s_nb - m).astype(BF16)
        e_cx = jnp.exp2(s_cx - m).astype(BF16)
        oz = (jnp.dot(e_nb, with_ones(vb[keys, :]), preferred_element_type=F32)
              + jnp.dot(e_cx, with_ones(cvb[...]), preferred_element_type=F32))
        on_ref[rows, :] = (oz[:, :HEAD_DIM] / oz[:, HEAD_DIM:]).astype(BF16)

    lane = lax.broadcasted_iota(jnp.int32, (DEC_SEQ, HEAD_DIM), 1)
    first_half = (lane & 16) == 0

    def rope(x):
        rot = jnp.where(first_half, pltpu.roll(x, HEAD_DIM - 16, 1), pltpu.roll(x, 16, 1))
        return x * cos_ref[...] + rot * sin_ref[...]

    q1, q2 = _split_q(rope(dq_ref[...]) * (DIFF_QK_DIM ** -0.5 * LOG2E))
    q1b[...] = q1
    q2b[...] = q2
    kb[...] = rope(dk_ref[...]).astype(BF16)
    vb[...] = dv_ref[...].astype(BF16)
    ckb[...] = cdk_ref[...].astype(BF16)
    cvb[...] = cdv_ref[...].astype(BF16)
    lam = _lambda(lam_ref, layer)

    def diff_block(j, carry):
        r0 = pl.multiple_of(j * Q_BLOCK, Q_BLOCK)

        def softmax_pv(q):
            s_l = lax.dot_general(q, kb[...], NT_DIMS, preferred_element_type=F32)
            s_c = lax.dot_general(q, ckb[...], NT_DIMS, preferred_element_type=F32)
            m = jnp.maximum(jnp.max(s_l, axis=-1, keepdims=True),
                            jnp.max(s_c, axis=-1, keepdims=True))
            e_l = jnp.exp2(s_l - m).astype(BF16)
            e_c = jnp.exp2(s_c - m).astype(BF16)
            oz = (jnp.dot(e_l, with_ones(vb[...]), preferred_element_type=F32)
                  + jnp.dot(e_c, with_ones(cvb[...]), preferred_element_type=F32))
            return oz[:, :HEAD_DIM] / oz[:, HEAD_DIM:]

        o = (softmax_pv(q1b[pl.ds(r0, Q_BLOCK), :])
             - lam * softmax_pv(q2b[pl.ds(r0, Q_BLOCK), :]))
        od_ref[pl.ds(r0, Q_BLOCK), :] = _sub_ln(o, sg_ref[...], layer).astype(BF16)
        return carry

    lax.fori_loop(0, DEC_SEQ // Q_BLOCK, diff_block, 0, unroll=True)


def _attn_latent(qkv, caches, tab, cos, sin, lam_p, subln, layer):
    h8 = N_HEADS
    row0 = N_PROMPT // DEC_SEQ
    blk = lambda off: pl.BlockSpec((DEC_SEQ, HEAD_DIM), lambda b, h: (row0 + b, off + h))
    cblk = pl.BlockSpec((None, None, None, PAST_LEN, HEAD_DIM), lambda b, h: (b, layer, h, 0, 0))
    fixed = pl.BlockSpec((DEC_SEQ, HEAD_DIM), lambda b, h: (0, 0))
    out_spec = pl.BlockSpec((DEC_SEQ, HEAD_DIM), lambda b, h: (b, h))
    merged_shape = jax.ShapeDtypeStruct((N_TOK - N_PROMPT, h8 * HEAD_DIM), BF16)
    return pl.pallas_call(
        functools.partial(_attn_latent_kernel, layer=layer),
        out_shape=(merged_shape, merged_shape),
        grid=(DEC_BATCH, h8),
        in_specs=[blk(0), blk(h8), blk(2 * h8), blk(3 * h8), blk(4 * h8), blk(5 * h8),
                  cblk, cblk, cblk, cblk,
                  pl.BlockSpec((None, 2 * WIN_H, GRID_W, 2 * GRID_W), lambda b, h: (h, 0, 0, 0)),
                  fixed, fixed,
                  pl.BlockSpec((None, 4, DIFF_QK_DIM), lambda b, h: (layer, 0, 0)),
                  pl.BlockSpec((None, 1, HEAD_DIM), lambda b, h: (layer, 0, 0))],
        out_specs=(out_spec, out_spec),
        scratch_shapes=[pltpu.VMEM((DEC_SEQ, HEAD_DIM), BF16), pltpu.VMEM((DEC_SEQ, HEAD_DIM), BF16),
                        pltpu.VMEM((PAST_LEN, HEAD_DIM), BF16), pltpu.VMEM((PAST_LEN, HEAD_DIM), BF16),
                        pltpu.VMEM((DEC_SEQ, HEAD_DIM), BF16), pltpu.VMEM((DEC_SEQ, HEAD_DIM), BF16)],
        compiler_params=_params(48),
        name="attn_latent",
    )(qkv, qkv, qkv, qkv, qkv, qkv, *caches, tab, cos, sin, lam_p, subln)


def _route(logits, bias):
    ex = jnp.exp(logits - jnp.max(logits, axis=0, keepdims=True))
    probs = ex / jnp.sum(ex, axis=0, keepdims=True)
    sel = probs + bias
    srow = [sel[e:e + 1, :] for e in range(N_EXPERTS)]
    prow = [probs[e:e + 1, :] for e in range(N_EXPERTS)]
    gscore = []
    for g in range(N_GROUPS):
        v = srow[g * EXPERTS_PER_GROUP:(g + 1) * EXPERTS_PER_GROUP]
        best = None
        for a in range(EXPERTS_PER_GROUP):
            for b in range(a + 1, EXPERTS_PER_GROUP):
                pair = v[a] + v[b]
                best = pair if best is None else jnp.maximum(best, pair)
        gscore.append(best)
    gbest = gscore[0]
    gidx = jnp.zeros(gbest.shape, jnp.int32)
    for g in range(1, N_GROUPS):
        better = gscore[g] > gbest
        gidx = jnp.where(better, g, gidx)
        gbest = jnp.where(better, gscore[g], gbest)
    neg = jnp.full(gbest.shape, -jnp.inf, F32)
    picks = []
    taken = None
    for _ in range(2):
        best = neg
        idx = jnp.zeros(gbest.shape, jnp.int32)
        wgt = jnp.zeros(gbest.shape, F32)
        for e in range(N_EXPERTS):
            ok = gidx == (e // EXPERTS_PER_GROUP)
            if taken is not None:
                ok = jnp.logical_and(ok, taken != e)
            cand = jnp.where(ok, srow[e], neg)
            better = cand > best
            idx = jnp.where(better, e, idx)
            wgt = jnp.where(better, prow[e], wgt)
            best = jnp.where(better, cand, best)
        picks.append((idx, wgt))
        taken = idx
    (i0, w0), (i1, w1) = picks
    wsum = w0 + w1
    return i0, i1, w0 / wsum, w1 / wsum


def _split_bf16(v):
    hi = v.astype(BF16)
    lo = (v - hi.astype(F32)).astype(BF16)
    return hi, lo


def _load_resident_weight(w_hbm, wbf, stage, sem, n_chunks):
    ck = wbf.shape[0] // n_chunks
    copies = [pltpu.make_async_copy(w_hbm.at[pl.ds(c * ck, ck), :], stage.at[c % 2], sem.at[c % 2])
              for c in range(n_chunks)]
    copies[0].start()
    for c in range(n_chunks):
        if c + 1 < n_chunks:
            copies[c + 1].start()
        copies[c].wait()
        wbf[c * ck:(c + 1) * ck, :] = stage[c % 2].astype(BF16)


def _out_proj_kernel(pn_ref, pd_ref, sn_ref, sd_ref, w_hbm, *rest, layer, split_x):
    if split_x:
        x_in = _select_stream(rest[0], rest[1], OUTPROJ_TM)
        rest = rest[2:]
    else:
        x_in = rest[0][...]
        rest = rest[1:]
    (mod_ref, g_ref, wr_ref, rb_ref, xo_ref, h_ref, idx_ref, cw_ref, wbf, stage, mbuf, sem) = rest
    i = pl.program_id(0)
    half = D_MODEL // 2

    @pl.when(i == 0)
    def _():
        _load_resident_weight(w_hbm.at[layer], wbf, stage, sem, OUTPROJ_WCHUNKS)

    prompt = i < N_PROMPT // OUTPROJ_TM

    @pl.when(prompt)
    def _():
        mbuf[:, :half] = pn_ref[...]
        mbuf[:, half:] = pd_ref[...]

    @pl.when(jnp.logical_not(prompt))
    def _():
        mbuf[:, :half] = sn_ref[...]
        mbuf[:, half:] = sd_ref[...]

    a = jnp.dot(mbuf[...], wbf[...], preferred_element_type=F32)
    x = x_in + mod_ref[2:3, :] * a
    xo_ref[...] = x
    h = _rms(x, g_ref[...]) * (1.0 + mod_ref[4:5, :]) + mod_ref[3:4, :]
    h_ref[...] = h
    w_hi, w_lo = _split_bf16(wr_ref[...])
    h_hi, h_lo = _split_bf16(h)
    part = lax.dot_general(jnp.concatenate([w_hi, w_lo], axis=0), h_hi, NT_DIMS,
                           preferred_element_type=F32)
    logits = (part[:N_EXPERTS] + part[N_EXPERTS:]
              + lax.dot_general(w_hi, h_lo, NT_DIMS, preferred_element_type=F32))
    i0, i1, w0, w1 = _route(logits, rb_ref[...])
    idx_ref[...] = jnp.concatenate([i0, i1], axis=0)
    cw_ref[...] = jnp.concatenate([w0, w1], axis=0)


def _out_proj(merged, w_out, x, mods, layer, gain, w_router_t, router_bias):
    tm = OUTPROJ_TM
    split_x = isinstance(x, tuple)
    x_args = list(x) if split_x else [x]
    x_specs = _stream_specs(tm) if split_x else [pl.BlockSpec((tm, D_MODEL), lambda i: (i, 0))]
    half = D_MODEL // 2
    npt = N_PROMPT // tm
    row = lambda i: (i, 0)
    fixed = lambda i: (0, 0)
    p_row = lambda i: (jnp.minimum(i, npt - 1), 0)
    s_row = lambda i: (jnp.maximum(i - npt, 0), 0)
    return pl.pallas_call(
        functools.partial(_out_proj_kernel, layer=layer, split_x=split_x),
        out_shape=(jax.ShapeDtypeStruct((N_TOK, D_MODEL), F32),
                   jax.ShapeDtypeStruct((N_TOK, D_MODEL), F32),
                   jax.ShapeDtypeStruct((2, N_TOK), jnp.int32),
                   jax.ShapeDtypeStruct((2, N_TOK), F32)),
        grid=(N_TOK // tm,),
        in_specs=[
            pl.BlockSpec((tm, half), p_row),
            pl.BlockSpec((tm, half), p_row),
            pl.BlockSpec((tm, half), s_row),
            pl.BlockSpec((tm, half), s_row),
            pl.BlockSpec(memory_space=pl.ANY),
        ] + x_specs + [
            pl.BlockSpec((None, None, 6, D_MODEL), lambda i: (layer, _mod_row(i, tm), 0, 0)),
            pl.BlockSpec((1, D_MODEL), fixed),
            pl.BlockSpec((N_EXPERTS, D_MODEL), fixed),
            pl.BlockSpec((N_EXPERTS, 1), fixed),
        ],
        out_specs=(pl.BlockSpec((tm, D_MODEL), row), pl.BlockSpec((tm, D_MODEL), row),
                   pl.BlockSpec((2, tm), lambda i: (0, i)),
                   pl.BlockSpec((2, tm), lambda i: (0, i))),
        scratch_shapes=[pltpu.VMEM((D_MODEL, D_MODEL), BF16),
                        pltpu.VMEM((2, D_MODEL // OUTPROJ_WCHUNKS, D_MODEL), F32),
                        pltpu.VMEM((tm, D_MODEL), BF16),
                        pltpu.SemaphoreType.DMA((2,))],
        compiler_params=_params(48),
        name="out_proj_router",
    )(*merged, w_out, *x_args, mods, gain, w_router_t, router_bias)


def _dispatch_kernel(pos_ref, h_ref, xs_in, xs_out, sem):
    del xs_in
    tm = DISPATCH_TM
    base = pl.program_id(0) * tm

    for r in range(tm):
        for k in range(2):
            p = pos_ref[k * N_TOK + base + r]
            pltpu.make_async_copy(h_ref.at[pl.ds(r, 1), :], xs_out.at[pl.ds(p, 1), :], sem).start()
    for _ in range(2):
        pltpu.make_async_copy(h_ref, xs_out.at[pl.ds(0, tm), :], sem).wait()


def _dispatch(pos, h2, xs_buf):
    tm = DISPATCH_TM
    return pl.pallas_call(
        _dispatch_kernel,
        out_shape=jax.ShapeDtypeStruct((EXPERT_CAP, D_MODEL), F32),
        grid_spec=pltpu.PrefetchScalarGridSpec(
            num_scalar_prefetch=1,
            grid=(N_TOK // tm,),
            in_specs=[pl.BlockSpec((tm, D_MODEL), lambda i, pos_ref: (i, 0)),
                      pl.BlockSpec(memory_space=pl.ANY)],
            out_specs=pl.BlockSpec(memory_space=pl.ANY),
            scratch_shapes=[pltpu.SemaphoreType.DMA(())],
        ),
        input_output_aliases={2: 0},
        compiler_params=_params(32),
        name="expert_dispatch",
    )(pos, h2, xs_buf)


def _experts_kernel(we_ref, wc_ref, wdo_ref, wsl_ref, ti_ref, tdo_ref, tsl_ref,
                    xs_ref, wg_ref, wu_ref, wd_ref, o_ref, wg_bf, wu_bf, wd_bf):
    del we_ref, ti_ref
    s = pl.program_id(0)

    @pl.when(wdo_ref[s] == 1)
    def _():
        slot = wsl_ref[s]
        c = wc_ref[s]
        ck = D_MODEL // EXPERT_WCHUNKS
        r0 = pl.multiple_of(c * ck, ck)
        wg_bf[slot, pl.ds(r0, ck), :] = wg_ref[...].astype(BF16)
        wu_bf[slot, pl.ds(r0, ck), :] = wu_ref[...].astype(BF16)
        cf = D_FF // EXPERT_WCHUNKS
        r1 = pl.multiple_of(c * cf, cf)
        wd_bf[slot, pl.ds(r1, cf), :] = wd_ref[...].astype(BF16)

    @pl.when(tdo_ref[s] == 1)
    def _():
        slot = tsl_ref[s]
        xb = xs_ref[...].astype(BF16)
        g = jnp.dot(xb, wg_bf[slot], preferred_element_type=F32)
        u = jnp.dot(xb, wu_bf[slot], preferred_element_type=F32)
        hid = (g * jax.nn.sigmoid(g) * u).astype(BF16)
        o_ref[...] = jnp.dot(hid, wd_bf[slot], preferred_element_type=F32)

    @pl.when(tdo_ref[s] == 2)
    def _():
        o_ref[...] = jnp.zeros_like(o_ref)


def _experts(plan, xs, w_gate, w_up, w_down, layer):
    tm = EXPERT_TM
    ck = D_MODEL // EXPERT_WCHUNKS
    cf = D_FF // EXPERT_WCHUNKS
    tile = lambda s, we, wc, wdo, wsl, ti, tdo, tsl: (ti[s], 0)
    wblk = lambda s, we, wc, wdo, wsl, ti, tdo, tsl: (layer, we[s], wc[s], 0)
    return pl.pallas_call(
        _experts_kernel,
        out_shape=jax.ShapeDtypeStruct((EXPERT_CAP, D_MODEL), F32),
        grid_spec=pltpu.PrefetchScalarGridSpec(
            num_scalar_prefetch=7,
            grid=(EXPERT_STEPS,),
            in_specs=[
                pl.BlockSpec((tm, D_MODEL), tile),
                pl.BlockSpec((None, None, ck, D_FF), wblk),
                pl.BlockSpec((None, None, ck, D_FF), wblk),
                pl.BlockSpec((None, None, cf, D_MODEL), wblk),
            ],
            out_specs=pl.BlockSpec((tm, D_MODEL), tile),
            scratch_shapes=[pltpu.VMEM((2, D_MODEL, D_FF), BF16),
                            pltpu.VMEM((2, D_MODEL, D_FF), BF16),
                            pltpu.VMEM((2, D_FF, D_MODEL), BF16)],
        ),
        compiler_params=_params(56),
        name="expert_mlp",
    )(*plan, xs, w_gate, w_up, w_down)


def _dispatch_plan(idx):
    e = idx.reshape(-1)
    onehot = (e[:, None] == jnp.arange(N_EXPERTS, dtype=jnp.int32)[None, :]).astype(jnp.int32)
    nb = N_PAIRS // PLAN_BLOCK
    tri = jnp.tril(jnp.ones((PLAN_BLOCK, PLAN_BLOCK), F32))
    within = jnp.einsum('ij,bjk->bik', tri, onehot.reshape(nb, PLAN_BLOCK, N_EXPERTS).astype(F32),
                        preferred_element_type=F32).astype(jnp.int32)
    totals = within[:, -1, :]
    before = jnp.cumsum(totals, axis=0) - totals
    csum = (within + before[:, None, :]).reshape(N_PAIRS, N_EXPERTS)
    counts = csum[-1]
    rank = jnp.sum(onehot * csum, axis=1) - 1
    nt = (counts + EXPERT_TM - 1) // EXPERT_TM
    tile_end = jnp.cumsum(nt)
    tile_start = tile_end - nt
    pos = jnp.sum(onehot * (tile_start * EXPERT_TM)[None, :], axis=1) + rank

    nw = EXPERT_WCHUNKS
    length = jnp.maximum(nt, nw)
    phase_end = nw + jnp.cumsum(length)
    phase_start = phase_end - length
    zero = jnp.zeros((1,), jnp.int32)
    s = jnp.arange(EXPERT_STEPS, dtype=jnp.int32)
    ph = jnp.sum((s[:, None] >= phase_start[None, :]).astype(jnp.int32), axis=1)
    ex = ph - 1
    k = s - jnp.concatenate([zero, phase_start])[ph]
    nt_s = jnp.concatenate([zero, nt])[ph]
    ts_s = jnp.concatenate([zero, tile_start])[ph]
    live = s < phase_end[-1]
    last = N_EXPERTS - 1
    w_do = (live & (k < nw) & (ex < last)).astype(jnp.int32)
    w_e = jnp.minimum(ex + 1, last)
    w_c = jnp.where(ex < last, jnp.minimum(k, nw - 1), nw - 1)
    w_slot = (ex + 1) % 2
    t_do = (live & (k < nt_s)).astype(jnp.int32)
    t_idx = jnp.maximum(ts_s + jnp.minimum(k + 1, nt_s) - 1, 0)
    z = tile_end[-1] + (s - phase_end[-1])
    t_do = jnp.where(jnp.logical_not(live) & (z < EXPERT_TILES), 2, t_do)
    t_idx = jnp.where(live, t_idx, jnp.minimum(z, EXPERT_TILES - 1))
    t_slot = jnp.maximum(ex, 0) % 2
    plan = tuple(a.astype(jnp.int32) for a in (w_e, w_c, w_do, w_slot, t_idx, t_do, t_slot))
    return pos.astype(jnp.int32), plan


def _bias_tables(nat_rpb):
    w = GRID_W
    c = jnp.arange(w)[:, None]
    kc = jnp.arange(w)[None, :]
    qstart = jnp.clip(c - WIN_W // 2, 0, w - WIN_W)
    valid = (kc >= qstart) & (kc < qstart + WIN_W)
    lead = nat_rpb.shape[:-1]
    pad_l = w - WIN_W
    row = jnp.pad(nat_rpb, [(0, 0)] * len(lead) + [(pad_l, 2 * w - pad_l - nat_rpb.shape[-1])])
    skew = jnp.tile(row, w)[..., :w * (2 * w - 1)].reshape(lead + (w, 2 * w - 1))
    tab = jnp.where(valid, skew[..., w - 1:2 * w - 1] * LOG2E, MASK_VALUE)
    tab = jnp.pad(tab, [(0, 0), (0, 0), (1, 1), (0, 0), (0, 0)], constant_values=MASK_VALUE)
    return jnp.concatenate([tab[:, :, :-1], tab[:, :, 1:]], axis=-1)


def _rope_tables():
    t = jnp.arange(DEC_SEQ)
    row = (t // GRID_W).astype(F32)
    col = (t % GRID_W).astype(F32)
    nf = DIFF_QK_DIM // 4
    inv = ROPE_BASE ** (-jnp.arange(nf, dtype=F32) / nf)
    ar = row[:, None] * inv[None, :]
    ac = col[:, None] * inv[None, :]
    cos = jnp.concatenate([jnp.cos(ar), jnp.cos(ar), jnp.cos(ac), jnp.cos(ac)], axis=-1)
    sin = jnp.concatenate([-jnp.sin(ar), jnp.sin(ar), -jnp.sin(ac), jnp.sin(ac)], axis=-1)
    return jnp.tile(cos, (1, 2)), jnp.tile(sin, (1, 2))


def kernel(x_prompt, x_sample, cache_nat_k, cache_nat_v, cache_diff_k, cache_diff_v, c, c_ctx,
           w_ada, b_ada, norm1, norm2, norm_final, w_in, w_out, nat_rpb, diff_lambda, diff_subln,
           w_router, router_bias, w_gate, w_up, w_down):
    x = (x_prompt.reshape(N_PROMPT, D_MODEL), x_sample.reshape(DEC_BATCH * DEC_SEQ, D_MODEL))
    cond =jnp.concatenate([c_ctx[None, :], c, jnp.zeros((8 - 1 - DEC_BATCH, D_MODEL), F32)], axis=0)
    mods = _modulation(cond, w_ada, b_ada)
    tabs = _bias_tables(nat_rpb)
    cos, sin = _rope_tables()
    subln = diff_subln.reshape(DEPTH, 1, HEAD_DIM)
    w_router_t = w_router.T
    rbias = router_bias.reshape(N_EXPERTS, 1)
    lat_caches = (cache_nat_k, cache_nat_v, cache_diff_k, cache_diff_v)

    new_caches = tuple(jnp.zeros((BATCH, DEPTH, N_HEADS, SEQ, HEAD_DIM), F32) for _ in range(4))
    xs_buf = jnp.zeros((EXPERT_CAP, D_MODEL), F32)
    ys = pos = cw = None
    for layer in range(DEPTH):
        gain1 = norm1[layer].reshape(1, D_MODEL)
        if layer == 0:
            h = _pre_first(*x, mods, layer, gain1)
        else:
            x, h = _pre_combine(x, ys, pos, cw, mods, layer - 1, gain1, final=False)
        qkv = _in_proj(h, w_in, layer)
        pn, pd, new_caches = _attn_prompt(qkv, diff_lambda, subln, new_caches, layer)
        sn, sd = _attn_latent(qkv, lat_caches, tabs[layer], cos, sin, diff_lambda, subln, layer)
        x, h2, idx, cwt = _out_proj((pn, pd, sn, sd), w_out, x, mods, layer,
                                    norm2[layer].reshape(1, D_MODEL), w_router_t, rbias)
        pos, plan = _dispatch_plan(idx)
        cw = cwt.T
        xs_buf = _dispatch(pos, h2, xs_buf)
        ys = _experts(plan, xs_buf, w_gate, w_up, w_down, layer)
    yp, ysm = _pre_combine(x, ys, pos, cw, mods, DEPTH - 1, norm_final.reshape(1, D_MODEL), final=True)
    return (yp.reshape(BATCH, SEQ, D_MODEL), ysm.reshape(DEC_BATCH, DEC_SEQ, D_MODEL)) + tuple(new_caches)
```

```python
import functools
import math

import jax
import jax.numpy as jnp
from jax import lax
from jax.experimental import pallas as pl
from jax.experimental.pallas import tpu as pltpu

D_MODEL = 2048
BATCH = 16
SEQ = 256
DEPTH = 4
DEC_BATCH = 4
DEC_SEQ = 1024
PAST_LEN = 512
GRID_W = 64
GRID_H = DEC_SEQ // GRID_W
N_HEADS = 8
HEAD_DIM = 128
DIFF_QK_DIM = 64
WIN_H = 8
WIN_W = 16
N_EXPERTS = 16
N_GROUPS = 4
EXPERTS_PER_GROUP = N_EXPERTS // N_GROUPS
D_FF = 1024
ROPE_BASE = 10000.0
EPS = 1e-6
IN_COLS = 6 * N_HEADS * HEAD_DIM
N_PROMPT = BATCH * SEQ
N_TOK = N_PROMPT + DEC_BATCH * DEC_SEQ
N_PAIRS = 2 * N_TOK

PRE_TM = 256
INPROJ_TM = 2048
INPROJ_TN = 512
OUTPROJ_TM = 256
OUTPROJ_WCHUNKS = 4
DISPATCH_TM = 256
EXPERT_TM = 256
EXPERT_CAP = N_PAIRS + N_EXPERTS * EXPERT_TM
EXPERT_TILES = EXPERT_CAP // EXPERT_TM
EXPERT_WCHUNKS = 4
EXPERT_STEPS = EXPERT_WCHUNKS * (N_EXPERTS + 1) + EXPERT_TILES
PLAN_BLOCK = 128
MOD_TN = 1024
Q_BLOCK = 256
PROMPT_HEADS = 4
NAT_QROWS = 4
NAT_KROWS = 12
LOG2E = 1.4426950408889634
MASK_VALUE = -1e30

F32 = jnp.float32
BF16 = jnp.bfloat16
MIB = 1024 * 1024
NT_DIMS = (((1,), (1,)), ((), ()))


def _params(vmem_mib):
    return pltpu.CompilerParams(vmem_limit_bytes=vmem_mib * MIB)


def _lam_init(layer):
    return 0.8 - 0.6 * math.exp(-0.3 * layer)


def _rms(x, gain):
    return x * lax.rsqrt(jnp.mean(x * x, axis=-1, keepdims=True) + EPS) * gain


def _mod_row(i, tm):
    first = i * tm
    return jnp.where(first < N_PROMPT, 0, 1 + (first - N_PROMPT) // DEC_SEQ)


def _mod_kernel(cond_ref, w_ref, b_ref, o_ref):
    c = cond_ref[...]
    s = c * jax.nn.sigmoid(c)
    o_ref[...] = jnp.dot(s.astype(BF16), w_ref[...].astype(BF16),
                         preferred_element_type=F32) + b_ref[...]


def _modulation(cond, w_ada, b_ada):
    n = 6 * D_MODEL
    out = pl.pallas_call(
        _mod_kernel,
        out_shape=jax.ShapeDtypeStruct((DEPTH, 8, n), F32),
        grid=(DEPTH, n // MOD_TN),
        in_specs=[
            pl.BlockSpec((8, D_MODEL), lambda l, j: (0, 0)),
            pl.BlockSpec((None, D_MODEL, MOD_TN), lambda l, j: (l, 0, j)),
            pl.BlockSpec((None, 1, MOD_TN), lambda l, j: (l, 0, j)),
        ],
        out_specs=pl.BlockSpec((None, 8, MOD_TN), lambda l, j: (l, 0, j)),
        compiler_params=_params(40),
        name="adaln_modulation",
    )(cond, w_ada, b_ada.reshape(DEPTH, 1, n))
    return out.reshape(DEPTH, 8, 6, D_MODEL)


def _gather_expert_rows(pos_ref, ys_hbm, ybuf, sem, tm):
    i = pl.program_id(0)
    n = pl.num_programs(0)

    def issue_tile(tile, slot):
        for r in range(tm):
            for k in range(2):
                p = pos_ref[k * N_TOK + tile * tm + r]
                pltpu.make_async_copy(ys_hbm.at[pl.ds(p, 1), :],
                                      ybuf.at[slot, k, pl.ds(r, 1), :], sem.at[slot, k]).start()

    @pl.when(i == 0)
    def _():
        issue_tile(0, 0)

    @pl.when(i + 1 < n)
    def _():
        issue_tile(i + 1, (i + 1) % 2)

    slot = i % 2
    for k in range(2):
        pltpu.make_async_copy(ys_hbm.at[pl.ds(0, tm), :], ybuf.at[slot, k], sem.at[slot, k]).wait()
    return slot


def _select_stream(xp_ref, xs_ref, tm):
    prompt = pl.program_id(0) < N_PROMPT // tm
    return jnp.where(prompt, xp_ref[...], xs_ref[...])


def _pre_first_kernel(xp_ref, xs_ref, mod_ref, g_ref, h_ref):
    x = _select_stream(xp_ref, xs_ref, PRE_TM)
    h = _rms(x, g_ref[...]) * (1.0 + mod_ref[1:2, :]) + mod_ref[0:1, :]
    h_ref[...] = h.astype(BF16)


def _pre_mid_kernel(pos_ref, x_ref, ys_hbm, cw_ref, modp_ref, mod_ref, g_ref,
                    xo_ref, h_ref, ybuf, sem):
    slot = _gather_expert_rows(pos_ref, ys_hbm, ybuf, sem, PRE_TM)
    cw = cw_ref[...]
    y = cw[:, 0:1] * ybuf[slot, 0] + cw[:, 1:2] * ybuf[slot, 1]
    x = x_ref[...] + modp_ref[5:6, :] * y
    xo_ref[...] = x
    h = _rms(x, g_ref[...]) * (1.0 + mod_ref[1:2, :]) + mod_ref[0:1, :]
    h_ref[...] = h.astype(BF16)


def _pre_final_kernel(pos_ref, x_ref, ys_hbm, cw_ref, modp_ref, g_ref, yp_ref, ys_ref, ybuf, sem):
    slot = _gather_expert_rows(pos_ref, ys_hbm, ybuf, sem, PRE_TM)
    cw = cw_ref[...]
    y = cw[:, 0:1] * ybuf[slot, 0] + cw[:, 1:2] * ybuf[slot, 1]
    x = x_ref[...] + modp_ref[5:6, :] * y
    out = _rms(x, g_ref[...])
    prompt = pl.program_id(0) < N_PROMPT // PRE_TM

    @pl.when(prompt)
    def _():
        yp_ref[...] = out

    @pl.when(jnp.logical_not(prompt))
    def _():
        ys_ref[...] = out


def _stream_specs(tm):
    npt = N_PROMPT // tm
    return [pl.BlockSpec((tm, D_MODEL), lambda i, *_: (jnp.minimum(i, npt - 1), 0)),
            pl.BlockSpec((tm, D_MODEL), lambda i, *_: (jnp.maximum(i - npt, 0), 0))]


def _pre_first(xp, xs, mods, layer, gain):
    tm = PRE_TM
    return pl.pallas_call(
        _pre_first_kernel,
        out_shape=jax.ShapeDtypeStruct((N_TOK, D_MODEL), BF16),
        grid=(N_TOK // tm,),
        in_specs=_stream_specs(tm) + [
            pl.BlockSpec((None, None, 6, D_MODEL), lambda i: (layer, _mod_row(i, tm), 0, 0)),
            pl.BlockSpec((1, D_MODEL), lambda i: (0, 0)),
        ],
        out_specs=pl.BlockSpec((tm, D_MODEL), lambda i: (i, 0)),
        compiler_params=_params(32),
        name="pre_first",
    )(xp, xs, mods, gain)


def _pre_combine(x, ys, pos, cw, mods, prev_layer, gain, final):
    tm = PRE_TM
    row = lambda i, pos_ref: (i, 0)
    fixed = lambda i, pos_ref: (0, 0)
    in_specs = [
        pl.BlockSpec((tm, D_MODEL), row),
        pl.BlockSpec(memory_space=pl.ANY),
        pl.BlockSpec((tm, 2), row),
        pl.BlockSpec((None, None, 6, D_MODEL),
                     lambda i, pos_ref: (prev_layer, _mod_row(i, tm), 0, 0)),
    ]
    args = [x, ys, cw, mods]
    if final:
        kern = _pre_final_kernel
        out_shape = (jax.ShapeDtypeStruct((N_PROMPT, D_MODEL), F32),
                     jax.ShapeDtypeStruct((N_TOK - N_PROMPT, D_MODEL), F32))
        out_specs = tuple(_stream_specs(tm))
    else:
        kern = _pre_mid_kernel
        in_specs.append(pl.BlockSpec((None, None, 6, D_MODEL),
                                     lambda i, pos_ref: (prev_layer + 1, _mod_row(i, tm), 0, 0)))
        args.append(mods)
        out_shape = (jax.ShapeDtypeStruct((N_TOK, D_MODEL), F32),
                     jax.ShapeDtypeStruct((N_TOK, D_MODEL), BF16))
        out_specs = (pl.BlockSpec((tm, D_MODEL), row), pl.BlockSpec((tm, D_MODEL), row))
    in_specs.append(pl.BlockSpec((1, D_MODEL), fixed))
    args.append(gain)
    return pl.pallas_call(
        kern,
        out_shape=out_shape,
        grid_spec=pltpu.PrefetchScalarGridSpec(
            num_scalar_prefetch=1,
            grid=(N_TOK // tm,),
            in_specs=in_specs,
            out_specs=out_specs,
            scratch_shapes=[pltpu.VMEM((2, 2, tm, D_MODEL), F32), pltpu.SemaphoreType.DMA((2, 2))],
        ),
        compiler_params=_params(40),
        name="pre_final" if final else "pre_combine",
    )(pos, *args)


def _in_proj_kernel(h_ref, w_ref, o_ref):
    o_ref[...] = jnp.dot(h_ref[...], w_ref[...].astype(BF16), preferred_element_type=F32)


def _in_proj(h, w_in, layer):
    tm, tn = INPROJ_TM, INPROJ_TN
    return pl.pallas_call(
        _in_proj_kernel,
        out_shape=jax.ShapeDtypeStruct((N_TOK, IN_COLS), F32),
        grid=(N_TOK // tm, IN_COLS // tn),
        in_specs=[
            pl.BlockSpec((tm, D_MODEL), lambda i, j: (i, 0)),
            pl.BlockSpec((None, D_MODEL, tn), lambda i, j: (layer, 0, j)),
        ],
        out_specs=pl.BlockSpec((tm, tn), lambda i, j: (i, j)),
        compiler_params=_params(48),
        name="in_proj",
    )(h, w_in)


def _lambda(lam_ref, layer):
    lp = lam_ref[...]
    a = jnp.sum(lp[0:1, :] * lp[1:2, :], axis=-1, keepdims=True)
    b = jnp.sum(lp[2:3, :] * lp[3:4, :], axis=-1, keepdims=True)
    return jnp.exp(a) - jnp.exp(b) + _lam_init(layer)


def _split_q(q):
    lane = lax.broadcasted_iota(jnp.int32, q.shape, 1)
    q1 = jnp.where(lane < DIFF_QK_DIM, q, 0.0).astype(BF16)
    q2 = jnp.where(lane >= DIFF_QK_DIM, q, 0.0).astype(BF16)
    return q1, q2


def _sub_ln(o, gain, layer):
    return _rms(o, gain) * (1.0 - _lam_init(layer))


def _attn_prompt_kernel(nq_ref, nk_ref, nv_ref, dq_ref, dk_ref, dv_ref, lam_ref, sg_ref,
                        c0_in, c1_in, c2_in, c3_in,
                        on_ref, od_ref, onk_ref, onv_ref, odk_ref, odv_ref, *, layer):
    del c0_in, c1_in, c2_in, c3_in
    ones = jnp.ones((SEQ, HEAD_DIM), BF16)
    lam = _lambda(lam_ref, layer)

    def softmax_pv(q, k, v):
        s = lax.dot_general(q, k, NT_DIMS, preferred_element_type=F32)
        e = jnp.exp2(s - jnp.max(s, axis=-1, keepdims=True)).astype(BF16)
        oz = jnp.dot(e, jnp.concatenate([v, ones], axis=1), preferred_element_type=F32)
        return oz[:, :HEAD_DIM] / oz[:, HEAD_DIM:]

    for hh in range(PROMPT_HEADS):
        cols = slice(hh * HEAD_DIM, (hh + 1) * HEAD_DIM)
        nk = nk_ref[:, cols]
        nv = nv_ref[:, cols]
        dk = dk_ref[:, cols]
        dv = dv_ref[:, cols]
        onk_ref[hh] = nk
        onv_ref[hh] = nv
        odk_ref[hh] = dk
        odv_ref[hh] = dv
        q = (nq_ref[:, cols] * (HEAD_DIM ** -0.5 * LOG2E)).astype(BF16)
        on_ref[:, cols] = softmax_pv(q, nk.astype(BF16), nv.astype(BF16)).astype(BF16)
        q1, q2 = _split_q(dq_ref[:, cols] * (DIFF_QK_DIM ** -0.5 * LOG2E))
        kb = dk.astype(BF16)
        vb = dv.astype(BF16)
        o = softmax_pv(q1, kb, vb) - lam * softmax_pv(q2, kb, vb)
        od_ref[:, cols] = _sub_ln(o, sg_ref[...], layer).astype(BF16)


def _attn_prompt(qkv, lam_p, subln, caches, layer):
    hp = PROMPT_HEADS
    h8 = N_HEADS // hp
    blk = lambda off: pl.BlockSpec((SEQ, hp * HEAD_DIM), lambda b, h: (b, off + h))
    in_specs = [blk(0), blk(h8), blk(2 * h8), blk(3 * h8), blk(4 * h8), blk(5 * h8),
                pl.BlockSpec((None, 4, DIFF_QK_DIM), lambda b, h: (layer, 0, 0)),
                pl.BlockSpec((None, 1, HEAD_DIM), lambda b, h: (layer, 0, 0))]
    args = [qkv] * 6 + [lam_p, subln]
    aliases = {}
    for n, cache in enumerate(caches):
        aliases[len(args)] = 2 + n
        in_specs.append(pl.BlockSpec(memory_space=pl.ANY))
        args.append(cache)
    cache_shape = jax.ShapeDtypeStruct((BATCH, DEPTH, N_HEADS, SEQ, HEAD_DIM), F32)
    cache_spec = pl.BlockSpec((None, None, hp, SEQ, HEAD_DIM), lambda b, h: (b, layer, h, 0, 0))
    merged_shape = jax.ShapeDtypeStruct((N_PROMPT, N_HEADS * HEAD_DIM), BF16)
    merged_spec = pl.BlockSpec((SEQ, hp * HEAD_DIM), lambda b, h: (b, h))
    out = pl.pallas_call(
        functools.partial(_attn_prompt_kernel, layer=layer),
        out_shape=(merged_shape, merged_shape,
                   cache_shape, cache_shape, cache_shape, cache_shape),
        grid=(BATCH, h8),
        in_specs=in_specs,
        out_specs=(merged_spec, merged_spec,
                   cache_spec, cache_spec, cache_spec, cache_spec),
        input_output_aliases=aliases,
        compiler_params=_params(32),
        name="attn_prompt",
    )(*args)
    return out[0], out[1], out[2:]


def _attn_latent_kernel(nq_ref, nk_ref, nv_ref, dq_ref, dk_ref, dv_ref,
                        cnk_ref, cnv_ref, cdk_ref, cdv_ref, tab_ref, cos_ref, sin_ref,
                        lam_ref, sg_ref, on_ref, od_ref,
                        kb, vb, ckb, cvb, q1b, q2b, *, layer):
    kb[...] = nk_ref[...].astype(BF16)
    vb[...] = nv_ref[...].astype(BF16)
    ckb[...] = cnk_ref[...].astype(BF16)
    cvb[...] = cnv_ref[...].astype(BF16)
    masked = jnp.full((GRID_W, 2 * GRID_W), MASK_VALUE, F32)

    def with_ones(v):
        return jnp.concatenate([v, jnp.ones(v.shape, BF16)], axis=1)

    left = lax.broadcasted_iota(jnp.int32, (GRID_W, 2 * GRID_W), 1) < GRID_W

    def bias_tile(r, kr):
        lo = min(max(r - WIN_H // 2, 0), GRID_H - WIN_H)
        ok0 = lo <= kr < lo + WIN_H
        ok1 = lo <= kr + 1 < lo + WIN_H
        if not (ok0 or ok1):
            return masked
        tile = tab_ref[kr - r + WIN_H]
        if ok0 and ok1:
            return tile
        return jnp.where(left if ok0 else jnp.logical_not(left), tile, masked)

    for blk in range(GRID_H // NAT_QROWS):
        r0 = blk * NAT_QROWS
        k0 = min(max(r0 - WIN_H // 2, 0), GRID_H - NAT_KROWS)
        rows = slice(r0 * GRID_W, (r0 + NAT_QROWS) * GRID_W)
        keys = slice(k0 * GRID_W, (k0 + NAT_KROWS) * GRID_W)
        q = (nq_ref[rows, :] * (HEAD_DIM ** -0.5 * LOG2E)).astype(BF16)
        bias = jnp.concatenate(
            [jnp.concatenate([bias_tile(r0 + a, k0 + 2 * i) for i in range(NAT_KROWS // 2)], axis=1)
             for a in range(NAT_QROWS)], axis=0)
        s_nb = lax.dot_general(q, kb[keys, :], NT_DIMS, preferred_element_type=F32) + bias
        s_cx = lax.dot_general(q, ckb[...], NT_DIMS, preferred_element_type=F32)
        m = jnp.maximum(jnp.max(s_nb, axis=-1, keepdims=True),
                        jnp.max(s_cx, axis=-1, keepdims=True))
        e_nb = jnp.exp2(s_nb - m).astype(BF16)
        e_cx = jnp.exp2(s_cx - m).astype(BF16)
        oz = (jnp.dot(e_nb, with_ones(vb[keys, :]), preferred_element_type=F32)
              + jnp.dot(e_cx, with_ones(cvb[...]), preferred_element_type=F32))
        on_ref[rows, :] = (oz[:, :HEAD_DIM] / oz[:, HEAD_DIM:]).astype(BF16)

    lane = lax.broadcasted_iota(jnp.int32, (DEC_SEQ, HEAD_DIM), 1)
    first_half = (lane & 16) == 0

    def rope(x):
        rot = jnp.where(first_half, pltpu.roll(x, HEAD_DIM - 16, 1), pltpu.roll(x, 16, 1))
        return x * cos_ref[...] + rot * sin_ref[...]

    q1, q2 = _split_q(rope(dq_ref[...]) * (DIFF_QK_DIM ** -0.5 * LOG2E))
    q1b[...] = q1
    q2b[...] = q2
    kb[...] = rope(dk_ref[...]).astype(BF16)
    vb[...] = dv_ref[...].astype(BF16)
    ckb[...] = cdk_ref[...].astype(BF16)
    cvb[...] = cdv_ref[...].astype(BF16)
    lam = _lambda(lam_ref, layer)

    def diff_block(j, carry):
        r0 = pl.multiple_of(j * Q_BLOCK, Q_BLOCK)

        def softmax_pv(q):
            s_l = lax.dot_general(q, kb[...], NT_DIMS, preferred_element_type=F32)
            s_c = lax.dot_general(q, ckb[...], NT_DIMS, preferred_element_type=F32)
            m = jnp.maximum(jnp.max(s_l, axis=-1, keepdims=True),
                            jnp.max(s_c, axis=-1, keepdims=True))
            e_l = jnp.exp2(s_l - m).astype(BF16)
            e_c = jnp.exp2(s_c - m).astype(BF16)
            oz = (jnp.dot(e_l, with_ones(vb[...]), preferred_element_type=F32)
                  + jnp.dot(e_c, with_ones(cvb[...]), preferred_element_type=F32))
            return oz[:, :HEAD_DIM] / oz[:, HEAD_DIM:]

        o = (softmax_pv(q1b[pl.ds(r0, Q_BLOCK), :])
             - lam * softmax_pv(q2b[pl.ds(r0, Q_BLOCK), :]))
        od_ref[pl.ds(r0, Q_BLOCK), :] = _sub_ln(o, sg_ref[...], layer).astype(BF16)
        return carry

    lax.fori_loop(0, DEC_SEQ // Q_BLOCK, diff_block, 0, unroll=True)


def _attn_latent(qkv, caches, tab, cos, sin, lam_p, subln, layer):
    h8 = N_HEADS
    row0 = N_PROMPT // DEC_SEQ
    blk = lambda off: pl.BlockSpec((DEC_SEQ, HEAD_DIM), lambda b, h: (row0 + b, off + h))
    cblk = pl.BlockSpec((None, None, None, PAST_LEN, HEAD_DIM), lambda b, h: (b, layer, h, 0, 0))
    fixed = pl.BlockSpec((DEC_SEQ, HEAD_DIM), lambda b, h: (0, 0))
    out_spec = pl.BlockSpec((DEC_SEQ, HEAD_DIM), lambda b, h: (b, h))
    merged_shape = jax.ShapeDtypeStruct((N_TOK - N_PROMPT, h8 * HEAD_DIM), BF16)
    return pl.pallas_call(
        functools.partial(_attn_latent_kernel, layer=layer),
        out_shape=(merged_shape, merged_shape),
        grid=(DEC_BATCH, h8),
        in_specs=[blk(0), blk(h8), blk(2 * h8), blk(3 * h8), blk(4 * h8), blk(5 * h8),
                  cblk, cblk, cblk, cblk,
                  pl.BlockSpec((None, 2 * WIN_H, GRID_W, 2 * GRID_W), lambda b, h: (h, 0, 0, 0)),
                  fixed, fixed,
                  pl.BlockSpec((None, 4, DIFF_QK_DIM), lambda b, h: (layer, 0, 0)),
                  pl.BlockSpec((None, 1, HEAD_DIM), lambda b, h: (layer, 0, 0))],
        out_specs=(out_spec, out_spec),
        scratch_shapes=[pltpu.VMEM((DEC_SEQ, HEAD_DIM), BF16), pltpu.VMEM((DEC_SEQ, HEAD_DIM), BF16),
                        pltpu.VMEM((PAST_LEN, HEAD_DIM), BF16), pltpu.VMEM((PAST_LEN, HEAD_DIM), BF16),
                        pltpu.VMEM((DEC_SEQ, HEAD_DIM), BF16), pltpu.VMEM((DEC_SEQ, HEAD_DIM), BF16)],
        compiler_params=_params(48),
        name="attn_latent",
    )(qkv, qkv, qkv, qkv, qkv, qkv, *caches, tab, cos, sin, lam_p, subln)


def _route(logits, bias):
    ex = jnp.exp(logits - jnp.max(logits, axis=0, keepdims=True))
    probs = ex / jnp.sum(ex, axis=0, keepdims=True)
    sel = probs + bias
    srow = [sel[e:e + 1, :] for e in range(N_EXPERTS)]
    prow = [probs[e:e + 1, :] for e in range(N_EXPERTS)]
    gscore = []
    for g in range(N_GROUPS):
        v = srow[g * EXPERTS_PER_GROUP:(g + 1) * EXPERTS_PER_GROUP]
        best = None
        for a in range(EXPERTS_PER_GROUP):
            for b in range(a + 1, EXPERTS_PER_GROUP):
                pair = v[a] + v[b]
                best = pair if best is None else jnp.maximum(best, pair)
        gscore.append(best)
    gbest = gscore[0]
    gidx = jnp.zeros(gbest.shape, jnp.int32)
    for g in range(1, N_GROUPS):
        better = gscore[g] > gbest
        gidx = jnp.where(better, g, gidx)
        gbest = jnp.where(better, gscore[g], gbest)
    neg = jnp.full(gbest.shape, -jnp.inf, F32)
    picks = []
    taken = None
    for _ in range(2):
        best = neg
        idx = jnp.zeros(gbest.shape, jnp.int32)
        wgt = jnp.zeros(gbest.shape, F32)
        for e in range(N_EXPERTS):
            ok = gidx == (e // EXPERTS_PER_GROUP)
            if taken is not None:
                ok = jnp.logical_and(ok, taken != e)
            cand = jnp.where(ok, srow[e], neg)
            better = cand > best
            idx = jnp.where(better, e, idx)
            wgt = jnp.where(better, prow[e], wgt)
            best = jnp.where(better, cand, best)
        picks.append((idx, wgt))
        taken = idx
    (i0, w0), (i1, w1) = picks
    wsum = w0 + w1
    return i0, i1, w0 / wsum, w1 / wsum


def _split_bf16(v):
    hi = v.astype(BF16)
    lo = (v - hi.astype(F32)).astype(BF16)
    return hi, lo


def _load_resident_weight(w_hbm, wbf, stage, sem, n_chunks):
    ck = wbf.shape[0] // n_chunks
    copies = [pltpu.make_async_copy(w_hbm.at[pl.ds(c * ck, ck), :], stage.at[c % 2], sem.at[c % 2])
              for c in range(n_chunks)]
    copies[0].start()
    for c in range(n_chunks):
        if c + 1 < n_chunks:
            copies[c + 1].start()
        copies[c].wait()
        wbf[c * ck:(c + 1) * ck, :] = stage[c % 2].astype(BF16)


def _out_proj_kernel(pn_ref, pd_ref, sn_ref, sd_ref, w_hbm, *rest, layer, split_x):
    if split_x:
        x_in = _select_stream(rest[0], rest[1], OUTPROJ_TM)
        rest = rest[2:]
    else:
        x_in = rest[0][...]
        rest = rest[1:]
    (mod_ref, g_ref, wr_ref, rb_ref, xo_ref, h_ref, idx_ref, cw_ref, wbf, stage, mbuf, sem) = rest
    i = pl.program_id(0)
    half = D_MODEL // 2

    @pl.when(i == 0)
    def _():
        _load_resident_weight(w_hbm.at[layer], wbf, stage, sem, OUTPROJ_WCHUNKS)

    prompt = i < N_PROMPT // OUTPROJ_TM

    @pl.when(prompt)
    def _():
        mbuf[:, :half] = pn_ref[...]
        mbuf[:, half:] = pd_ref[...]

    @pl.when(jnp.logical_not(prompt))
    def _():
        mbuf[:, :half] = sn_ref[...]
        mbuf[:, half:] = sd_ref[...]

    a = jnp.dot(mbuf[...], wbf[...], preferred_element_type=F32)
    x = x_in + mod_ref[2:3, :] * a
    xo_ref[...] = x
    h = _rms(x, g_ref[...]) * (1.0 + mod_ref[4:5, :]) + mod_ref[3:4, :]
    h_ref[...] = h
    w_hi, w_lo = _split_bf16(wr_ref[...])
    h_hi, h_lo = _split_bf16(h)
    part = lax.dot_general(jnp.concatenate([w_hi, w_lo], axis=0), h_hi, NT_DIMS,
                           preferred_element_type=F32)
    logits = (part[:N_EXPERTS] + part[N_EXPERTS:]
              + lax.dot_general(w_hi, h_lo, NT_DIMS, preferred_element_type=F32))
    i0, i1, w0, w1 = _route(logits, rb_ref[...])
    idx_ref[...] = jnp.concatenate([i0, i1], axis=0)
    cw_ref[...] = jnp.concatenate([w0, w1], axis=0)


def _out_proj(merged, w_out, x, mods, layer, gain, w_router_t, router_bias):
    tm = OUTPROJ_TM
    split_x = isinstance(x, tuple)
    x_args = list(x) if split_x else [x]
    x_specs = _stream_specs(tm) if split_x else [pl.BlockSpec((tm, D_MODEL), lambda i: (i, 0))]
    half = D_MODEL // 2
    npt = N_PROMPT // tm
    row = lambda i: (i, 0)
    fixed = lambda i: (0, 0)
    p_row = lambda i: (jnp.minimum(i, npt - 1), 0)
    s_row = lambda i: (jnp.maximum(i - npt, 0), 0)
    return pl.pallas_call(
        functools.partial(_out_proj_kernel, layer=layer, split_x=split_x),
        out_shape=(jax.ShapeDtypeStruct((N_TOK, D_MODEL), F32),
                   jax.ShapeDtypeStruct((N_TOK, D_MODEL), F32),
                   jax.ShapeDtypeStruct((2, N_TOK), jnp.int32),
                   jax.ShapeDtypeStruct((2, N_TOK), F32)),
        grid=(N_TOK // tm,),
        in_specs=[
            pl.BlockSpec((tm, half), p_row),
            pl.BlockSpec((tm, half), p_row),
            pl.BlockSpec((tm, half), s_row),
            pl.BlockSpec((tm, half), s_row),
            pl.BlockSpec(memory_space=pl.ANY),
        ] + x_specs + [
            pl.BlockSpec((None, None, 6, D_MODEL), lambda i: (layer, _mod_row(i, tm), 0, 0)),
            pl.BlockSpec((1, D_MODEL), fixed),
            pl.BlockSpec((N_EXPERTS, D_MODEL), fixed),
            pl.BlockSpec((N_EXPERTS, 1), fixed),
        ],
        out_specs=(pl.BlockSpec((tm, D_MODEL), row), pl.BlockSpec((tm, D_MODEL), row),
                   pl.BlockSpec((2, tm), lambda i: (0, i)),
                   pl.BlockSpec((2, tm), lambda i: (0, i))),
        scratch_shapes=[pltpu.VMEM((D_MODEL, D_MODEL), BF16),
                        pltpu.VMEM((2, D_MODEL // OUTPROJ_WCHUNKS, D_MODEL), F32),
                        pltpu.VMEM((tm, D_MODEL), BF16),
                        pltpu.SemaphoreType.DMA((2,))],
        compiler_params=_params(48),
        name="out_proj_router",
    )(*merged, w_out, *x_args, mods, gain, w_router_t, router_bias)


def _dispatch_kernel(pos_ref, h_ref, xs_in, xs_out, sem):
    del xs_in
    tm = DISPATCH_TM
    base = pl.program_id(0) * tm

    for r in range(tm):
        for k in range(2):
            p = pos_ref[k * N_TOK + base + r]
            pltpu.make_async_copy(h_ref.at[pl.ds(r, 1), :], xs_out.at[pl.ds(p, 1), :], sem).start()
    for _ in range(2):
        pltpu.make_async_copy(h_ref, xs_out.at[pl.ds(0, tm), :], sem).wait()


def _dispatch(pos, h2, xs_buf):
    tm = DISPATCH_TM
    return pl.pallas_call(
        _dispatch_kernel,
        out_shape=jax.ShapeDtypeStruct((EXPERT_CAP, D_MODEL), F32),
        grid_spec=pltpu.PrefetchScalarGridSpec(
            num_scalar_prefetch=1,
            grid=(N_TOK // tm,),
            in_specs=[pl.BlockSpec((tm, D_MODEL), lambda i, pos_ref: (i, 0)),
                      pl.BlockSpec(memory_space=pl.ANY)],
            out_specs=pl.BlockSpec(memory_space=pl.ANY),
            scratch_shapes=[pltpu.SemaphoreType.DMA(())],
        ),
        input_output_aliases={2: 0},
        compiler_params=_params(32),
        name="expert_dispatch",
    )(pos, h2, xs_buf)


def _experts_kernel(we_ref, wc_ref, wdo_ref, wsl_ref, ti_ref, tdo_ref, tsl_ref,
                    xs_ref, wg_ref, wu_ref, wd_ref, o_ref, wg_bf, wu_bf, wd_bf):
    del we_ref, ti_ref
    s = pl.program_id(0)

    def cast_chunk():
        slot = wsl_ref[s]
        c = wc_ref[s]
        ck = D_MODEL // EXPERT_WCHUNKS
        r0 = pl.multiple_of(c * ck, ck)
        wg_bf[slot, pl.ds(r0, ck), :] = wg_ref[...].astype(BF16)
        wu_bf[slot, pl.ds(r0, ck), :] = wu_ref[...].astype(BF16)
        cf = D_FF // EXPERT_WCHUNKS
        r1 = pl.multiple_of(c * cf, cf)
        wd_bf[slot, pl.ds(r1, cf), :] = wd_ref[...].astype(BF16)

    def run_tile():
        slot = tsl_ref[s]
        xb = xs_ref[...].astype(BF16)
        g = jnp.dot(xb, wg_bf[slot], preferred_element_type=F32)
        u = jnp.dot(xb, wu_bf[slot], preferred_element_type=F32)
        hid = (g * jax.nn.sigmoid(g) * u).astype(BF16)
        o_ref[...] = jnp.dot(hid, wd_bf[slot], preferred_element_type=F32)

    load = wdo_ref[s] == 1
    tile = tdo_ref[s] == 1

    @pl.when(jnp.logical_and(load, tile))
    def _():
        cast_chunk()
        run_tile()

    @pl.when(jnp.logical_and(load, jnp.logical_not(tile)))
    def _():
        cast_chunk()

    @pl.when(jnp.logical_and(tile, jnp.logical_not(load)))
    def _():
        run_tile()

    @pl.when(tdo_ref[s] == 2)
    def _():
        o_ref[...] = jnp.zeros_like(o_ref)


def _experts(plan, xs, w_gate, w_up, w_down, layer):
    tm = EXPERT_TM
    ck = D_MODEL // EXPERT_WCHUNKS
    cf = D_FF // EXPERT_WCHUNKS
    tile = lambda s, we, wc, wdo, wsl, ti, tdo, tsl: (ti[s], 0)
    wblk = lambda s, we, wc, wdo, wsl, ti, tdo, tsl: (layer, we[s], wc[s], 0)
    return pl.pallas_call(
        _experts_kernel,
        out_shape=jax.ShapeDtypeStruct((EXPERT_CAP, D_MODEL), F32),
        grid_spec=pltpu.PrefetchScalarGridSpec(
            num_scalar_prefetch=7,
            grid=(EXPERT_STEPS,),
            in_specs=[
                pl.BlockSpec((tm, D_MODEL), tile),
                pl.BlockSpec((None, None, ck, D_FF), wblk),
                pl.BlockSpec((None, None, ck, D_FF), wblk),
                pl.BlockSpec((None, None, cf, D_MODEL), wblk),
            ],
            out_specs=pl.BlockSpec((tm, D_MODEL), tile),
            scratch_shapes=[pltpu.VMEM((2, D_MODEL, D_FF), BF16),
                            pltpu.VMEM((2, D_MODEL, D_FF), BF16),
                            pltpu.VMEM((2, D_FF, D_MODEL), BF16)],
        ),
        compiler_params=_params(56),
        name="expert_mlp",
    )(*plan, xs, w_gate, w_up, w_down)


def _dispatch_plan(idx):
    e = idx.reshape(-1)
    onehot = jnp.arange(N_EXPERTS, dtype=jnp.int32)[:, None] == e[None, :]
    nb = N_PAIRS // PLAN_BLOCK
    upper = jnp.triu(jnp.ones((PLAN_BLOCK, PLAN_BLOCK), F32))
    within = jnp.einsum('ebj,ji->ebi', onehot.astype(F32).reshape(N_EXPERTS, nb, PLAN_BLOCK), upper,
                        preferred_element_type=F32)
    totals = within[:, :, -1]
    before = jnp.cumsum(totals, axis=1) - totals
    csum = (within + before[:, :, None]).reshape(N_EXPERTS, N_PAIRS)
    counts = (before[:, -1] + totals[:, -1]).astype(jnp.int32)
    nt = (counts + EXPERT_TM - 1) // EXPERT_TM
    tile_end = jnp.cumsum(nt)
    tile_start = tile_end - nt
    first_row = (tile_start * EXPERT_TM).astype(F32)[:, None]
    pos = jnp.sum(jnp.where(onehot, csum - 1.0 + first_row, 0.0), axis=0)

    nw = EXPERT_WCHUNKS
    length = jnp.maximum(nt, nw)
    phase_end = nw + jnp.cumsum(length)
    phase_start = phase_end - length
    zero = jnp.zeros((1,), jnp.int32)
    s = jnp.arange(EXPERT_STEPS, dtype=jnp.int32)
    ph = jnp.sum((s[:, None] >= phase_start[None, :]).astype(jnp.int32), axis=1)
    ex = ph - 1
    k = s - jnp.concatenate([zero, phase_start])[ph]
    nt_s = jnp.concatenate([zero, nt])[ph]
    ts_s = jnp.concatenate([zero, tile_start])[ph]
    live = s < phase_end[-1]
    last = N_EXPERTS - 1
    w_do = (live & (k < nw) & (ex < last)).astype(jnp.int32)
    w_e = jnp.minimum(ex + 1, last)
    w_c = jnp.where(ex < last, jnp.minimum(k, nw - 1), nw - 1)
    w_slot = (ex + 1) % 2
    t_do = (live & (k < nt_s)).astype(jnp.int32)
    t_idx = jnp.maximum(ts_s + jnp.minimum(k + 1, nt_s) - 1, 0)
    z = tile_end[-1] + (s - phase_end[-1])
    t_do = jnp.where(jnp.logical_not(live) & (z < EXPERT_TILES), 2, t_do)
    t_idx = jnp.where(live, t_idx, jnp.minimum(z, EXPERT_TILES - 1))
    t_slot = jnp.maximum(ex, 0) % 2
    plan = tuple(a.astype(jnp.int32) for a in (w_e, w_c, w_do, w_slot, t_idx, t_do, t_slot))
    return pos.astype(jnp.int32), plan


def _bias_tables(nat_rpb):
    w = GRID_W
    c = jnp.arange(w)[:, None]
    kc = jnp.arange(w)[None, :]
    qstart = jnp.clip(c - WIN_W // 2, 0, w - WIN_W)
    valid = (kc >= qstart) & (kc < qstart + WIN_W)
    n_off = nat_rpb.shape[-1]
    select = ((kc - c + WIN_W - 1)[None] == jnp.arange(n_off)[:, None, None]).astype(F32)
    tab = jnp.einsum('lhdj,jck->lhdck', nat_rpb, select, precision=lax.Precision.HIGHEST)
    tab = jnp.where(valid, tab * LOG2E, MASK_VALUE)
    tab = jnp.pad(tab, [(0, 0), (0, 0), (1, 1), (0, 0), (0, 0)], constant_values=MASK_VALUE)
    return jnp.concatenate([tab[:, :, :-1], tab[:, :, 1:]], axis=-1)


def _rope_tables():
    t = jnp.arange(DEC_SEQ)
    row = (t // GRID_W).astype(F32)
    col = (t % GRID_W).astype(F32)
    nf = DIFF_QK_DIM // 4
    inv = ROPE_BASE ** (-jnp.arange(nf, dtype=F32) / nf)
    ar = row[:, None] * inv[None, :]
    ac = col[:, None] * inv[None, :]
    cos = jnp.concatenate([jnp.cos(ar), jnp.cos(ar), jnp.cos(ac), jnp.cos(ac)], axis=-1)
    sin = jnp.concatenate([-jnp.sin(ar), jnp.sin(ar), -jnp.sin(ac), jnp.sin(ac)], axis=-1)
    return jnp.tile(cos, (1, 2)), jnp.tile(sin, (1, 2))


def kernel(x_prompt, x_sample, cache_nat_k, cache_nat_v, cache_diff_k, cache_diff_v, c, c_ctx,
           w_ada, b_ada, norm1, norm2, norm_final, w_in, w_out, nat_rpb, diff_lambda, diff_subln,
           w_router, router_bias, w_gate, w_up, w_down):
    x = (x_prompt.reshape(N_PROMPT, D_MODEL), x_sample.reshape(DEC_BATCH * DEC_SEQ, D_MODEL))
    cond =jnp.concatenate([c_ctx[None, :], c, jnp.zeros((8 - 1 - DEC_BATCH, D_MODEL), F32)], axis=0)
    mods = _modulation(cond, w_ada, b_ada)
    tabs = _bias_tables(nat_rpb)
    cos, sin = _rope_tables()
    subln = diff_subln.reshape(DEPTH, 1, HEAD_DIM)
    w_router_t = w_router.T
    rbias = router_bias.reshape(N_EXPERTS, 1)
    lat_caches = (cache_nat_k, cache_nat_v, cache_diff_k, cache_diff_v)

    new_caches = tuple(jnp.zeros((BATCH, DEPTH, N_HEADS, SEQ, HEAD_DIM), F32) for _ in range(4))
    xs_buf = jnp.zeros((EXPERT_CAP, D_MODEL), F32)
    ys = pos = cw = None
    for layer in range(DEPTH):
        gain1 = norm1[layer].reshape(1, D_MODEL)
        if layer == 0:
            h = _pre_first(*x, mods, layer, gain1)
        else:
            x, h = _pre_combine(x, ys, pos, cw, mods, layer - 1, gain1, final=False)
        qkv = _in_proj(h, w_in, layer)
        pn, pd, new_caches = _attn_prompt(qkv, diff_lambda, subln, new_caches, layer)
        sn, sd = _attn_latent(qkv, lat_caches, tabs[layer], cos, sin, diff_lambda, subln, layer)
        x, h2, idx, cwt = _out_proj((pn, pd, sn, sd), w_out, x, mods, layer,
                                    norm2[layer].reshape(1, D_MODEL), w_router_t, rbias)
        pos, plan = _dispatch_plan(idx)
        cw = cwt.T
        xs_buf = _dispatch(pos, h2, xs_buf)
        ys = _experts(plan, xs_buf, w_gate, w_up, w_down, layer)
    yp, ysm = _pre_combine(x, ys, pos, cw, mods, DEPTH - 1, norm_final.reshape(1, D_MODEL), final=True)
    return (yp.reshape(BATCH, SEQ, D_MODEL), ysm.reshape(DEC_BATCH, DEC_SEQ, D_MODEL)) + tuple(new_caches)
```

```python
import functools
import math

import jax
import jax.numpy as jnp
from jax import lax
from jax.experimental import pallas as pl
from jax.experimental.pallas import tpu as pltpu

D_MODEL = 2048
BATCH = 16
SEQ = 256
DEPTH = 4
DEC_BATCH = 4
DEC_SEQ = 1024
PAST_LEN = 512
GRID_W = 64
GRID_H = DEC_SEQ // GRID_W
N_HEADS = 8
HEAD_DIM = 128
DIFF_QK_DIM = 64
WIN_H = 8
WIN_W = 16
N_EXPERTS = 16
N_GROUPS = 4
EXPERTS_PER_GROUP = N_EXPERTS // N_GROUPS
D_FF = 1024
ROPE_BASE = 10000.0
EPS = 1e-6
IN_COLS = 6 * N_HEADS * HEAD_DIM
N_PROMPT = BATCH * SEQ
N_TOK = N_PROMPT + DEC_BATCH * DEC_SEQ
N_PAIRS = 2 * N_TOK

PRE_TM = 256
INPROJ_TM = 2048
INPROJ_TN = 512
OUTPROJ_TM = 256
OUTPROJ_WCHUNKS = 4
DISPATCH_TM = 256
EXPERT_TM = 256
EXPERT_CAP = N_PAIRS + N_EXPERTS * EXPERT_TM
EXPERT_TILES = EXPERT_CAP // EXPERT_TM
EXPERT_WCHUNKS = 4
EXPERT_STEPS = EXPERT_WCHUNKS * (N_EXPERTS + 1) + EXPERT_TILES
PLAN_BLOCK = 128
MOD_TN = 1024
Q_BLOCK = 256
PROMPT_HEADS = 4
NAT_QROWS = 4
NAT_KROWS = 12
LOG2E = 1.4426950408889634
MASK_VALUE = -1e30

F32 = jnp.float32
BF16 = jnp.bfloat16
MIB = 1024 * 1024
NT_DIMS = (((1,), (1,)), ((), ()))


def _params(vmem_mib):
    return pltpu.CompilerParams(vmem_limit_bytes=vmem_mib * MIB)


def _lam_init(layer):
    return 0.8 - 0.6 * math.exp(-0.3 * layer)


def _rms(x, gain):
    return x * lax.rsqrt(jnp.mean(x * x, axis=-1, keepdims=True) + EPS) * gain


def _mod_row(i, tm):
    first = i * tm
    return jnp.where(first < N_PROMPT, 0, 1 + (first - N_PROMPT) // DEC_SEQ)


def _mod_kernel(cond_ref, w_ref, b_ref, o_ref):
    c = cond_ref[...]
    s = c * jax.nn.sigmoid(c)
    o_ref[...] = jnp.dot(s.astype(BF16), w_ref[...].astype(BF16),
                         preferred_element_type=F32) + b_ref[...]


def _modulation(cond, w_ada, b_ada):
    n = 6 * D_MODEL
    out = pl.pallas_call(
        _mod_kernel,
        out_shape=jax.ShapeDtypeStruct((DEPTH, 8, n), F32),
        grid=(DEPTH, n // MOD_TN),
        in_specs=[
            pl.BlockSpec((8, D_MODEL), lambda l, j: (0, 0)),
            pl.BlockSpec((None, D_MODEL, MOD_TN), lambda l, j: (l, 0, j)),
            pl.BlockSpec((None, 1, MOD_TN), lambda l, j: (l, 0, j)),
        ],
        out_specs=pl.BlockSpec((None, 8, MOD_TN), lambda l, j: (l, 0, j)),
        compiler_params=_params(40),
        name="adaln_modulation",
    )(cond, w_ada, b_ada.reshape(DEPTH, 1, n))
    return out.reshape(DEPTH, 8, 6, D_MODEL)


def _gather_expert_rows(pos_ref, ys_hbm, ybuf, sem, tm):
    i = pl.program_id(0)
    n = pl.num_programs(0)

    def issue_tile(tile, slot):
        for r in range(tm):
            for k in range(2):
                p = pos_ref[k * N_TOK + tile * tm + r]
                pltpu.make_async_copy(ys_hbm.at[pl.ds(p, 1), :],
                                      ybuf.at[slot, k, pl.ds(r, 1), :], sem.at[slot, k]).start()

    @pl.when(i == 0)
    def _():
        issue_tile(0, 0)

    @pl.when(i + 1 < n)
    def _():
        issue_tile(i + 1, (i + 1) % 2)

    slot = i % 2
    for k in range(2):
        pltpu.make_async_copy(ys_hbm.at[pl.ds(0, tm), :], ybuf.at[slot, k], sem.at[slot, k]).wait()
    return slot


def _select_stream(xp_ref, xs_ref, tm):
    prompt = pl.program_id(0) < N_PROMPT // tm
    return jnp.where(prompt, xp_ref[...], xs_ref[...])


def _pre_first_kernel(xp_ref, xs_ref, mod_ref, g_ref, h_ref):
    x = _select_stream(xp_ref, xs_ref, PRE_TM)
    h = _rms(x, g_ref[...]) * (1.0 + mod_ref[1:2, :]) + mod_ref[0:1, :]
    h_ref[...] = h.astype(BF16)


def _pre_mid_kernel(pos_ref, x_ref, ys_hbm, cw_ref, modp_ref, mod_ref, g_ref,
                    xo_ref, h_ref, ybuf, sem):
    slot = _gather_expert_rows(pos_ref, ys_hbm, ybuf, sem, PRE_TM)
    cw = cw_ref[...]
    y = cw[:, 0:1] * ybuf[slot, 0] + cw[:, 1:2] * ybuf[slot, 1]
    x = x_ref[...] + modp_ref[5:6, :] * y
    xo_ref[...] = x
    h = _rms(x, g_ref[...]) * (1.0 + mod_ref[1:2, :]) + mod_ref[0:1, :]
    h_ref[...] = h.astype(BF16)


def _pre_final_kernel(pos_ref, x_ref, ys_hbm, cw_ref, modp_ref, g_ref, yp_ref, ys_ref, ybuf, sem):
    slot = _gather_expert_rows(pos_ref, ys_hbm, ybuf, sem, PRE_TM)
    cw = cw_ref[...]
    y = cw[:, 0:1] * ybuf[slot, 0] + cw[:, 1:2] * ybuf[slot, 1]
    x = x_ref[...] + modp_ref[5:6, :] * y
    out = _rms(x, g_ref[...])
    prompt = pl.program_id(0) < N_PROMPT // PRE_TM

    @pl.when(prompt)
    def _():
        yp_ref[...] = out

    @pl.when(jnp.logical_not(prompt))
    def _():
        ys_ref[...] = out


def _stream_specs(tm):
    npt = N_PROMPT // tm
    return [pl.BlockSpec((tm, D_MODEL), lambda i, *_: (jnp.minimum(i, npt - 1), 0)),
            pl.BlockSpec((tm, D_MODEL), lambda i, *_: (jnp.maximum(i - npt, 0), 0))]


def _pre_first(xp, xs, mods, layer, gain):
    tm = PRE_TM
    return pl.pallas_call(
        _pre_first_kernel,
        out_shape=jax.ShapeDtypeStruct((N_TOK, D_MODEL), BF16),
        grid=(N_TOK // tm,),
        in_specs=_stream_specs(tm) + [
            pl.BlockSpec((None, None, 6, D_MODEL), lambda i: (layer, _mod_row(i, tm), 0, 0)),
            pl.BlockSpec((1, D_MODEL), lambda i: (0, 0)),
        ],
        out_specs=pl.BlockSpec((tm, D_MODEL), lambda i: (i, 0)),
        compiler_params=_params(32),
        name="pre_first",
    )(xp, xs, mods, gain)


def _pre_combine(x, ys, pos, cw, mods, prev_layer, gain, final):
    tm = PRE_TM
    row = lambda i, pos_ref: (i, 0)
    fixed = lambda i, pos_ref: (0, 0)
    in_specs = [
        pl.BlockSpec((tm, D_MODEL), row),
        pl.BlockSpec(memory_space=pl.ANY),
        pl.BlockSpec((tm, 2), row),
        pl.BlockSpec((None, None, 6, D_MODEL),
                     lambda i, pos_ref: (prev_layer, _mod_row(i, tm), 0, 0)),
    ]
    args = [x, ys, cw, mods]
    if final:
        kern = _pre_final_kernel
        out_shape = (jax.ShapeDtypeStruct((N_PROMPT, D_MODEL), F32),
                     jax.ShapeDtypeStruct((N_TOK - N_PROMPT, D_MODEL), F32))
        out_specs = tuple(_stream_specs(tm))
    else:
        kern = _pre_mid_kernel
        in_specs.append(pl.BlockSpec((None, None, 6, D_MODEL),
                                     lambda i, pos_ref: (prev_layer + 1, _mod_row(i, tm), 0, 0)))
        args.append(mods)
        out_shape = (jax.ShapeDtypeStruct((N_TOK, D_MODEL), F32),
                     jax.ShapeDtypeStruct((N_TOK, D_MODEL), BF16))
        out_specs = (pl.BlockSpec((tm, D_MODEL), row), pl.BlockSpec((tm, D_MODEL), row))
    in_specs.append(pl.BlockSpec((1, D_MODEL), fixed))
    args.append(gain)
    return pl.pallas_call(
        kern,
        out_shape=out_shape,
        grid_spec=pltpu.PrefetchScalarGridSpec(
            num_scalar_prefetch=1,
            grid=(N_TOK // tm,),
            in_specs=in_specs,
            out_specs=out_specs,
            scratch_shapes=[pltpu.VMEM((2, 2, tm, D_MODEL), F32), pltpu.SemaphoreType.DMA((2, 2))],
        ),
        compiler_params=_params(40),
        name="pre_final" if final else "pre_combine",
    )(pos, *args)


def _in_proj_kernel(h_ref, w_ref, o_ref):
    o_ref[...] = jnp.dot(h_ref[...], w_ref[...].astype(BF16), preferred_element_type=F32)


def _in_proj(h, w_in, layer):
    tm, tn = INPROJ_TM, INPROJ_TN
    return pl.pallas_call(
        _in_proj_kernel,
        out_shape=jax.ShapeDtypeStruct((N_TOK, IN_COLS), F32),
        grid=(N_TOK // tm, IN_COLS // tn),
        in_specs=[
            pl.BlockSpec((tm, D_MODEL), lambda i, j: (i, 0)),
            pl.BlockSpec((None, D_MODEL, tn), lambda i, j: (layer, 0, j)),
        ],
        out_specs=pl.BlockSpec((tm, tn), lambda i, j: (i, j)),
        compiler_params=_params(48),
        name="in_proj",
    )(h, w_in)


def _lambda(lam_ref, layer):
    lp = lam_ref[...]
    a = jnp.sum(lp[0:1, :] * lp[1:2, :], axis=-1, keepdims=True)
    b = jnp.sum(lp[2:3, :] * lp[3:4, :], axis=-1, keepdims=True)
    return jnp.exp(a) - jnp.exp(b) + _lam_init(layer)


def _split_q(q):
    lane = lax.broadcasted_iota(jnp.int32, q.shape, 1)
    q1 = jnp.where(lane < DIFF_QK_DIM, q, 0.0).astype(BF16)
    q2 = jnp.where(lane >= DIFF_QK_DIM, q, 0.0).astype(BF16)
    return q1, q2


def _sub_ln(o, gain, layer):
    return _rms(o, gain) * (1.0 - _lam_init(layer))


def _attn_prompt_kernel(nq_ref, nk_ref, nv_ref, dq_ref, dk_ref, dv_ref, lam_ref, sg_ref,
                        c0_in, c1_in, c2_in, c3_in,
                        on_ref, od_ref, onk_ref, onv_ref, odk_ref, odv_ref, *, layer):
    del c0_in, c1_in, c2_in, c3_in
    ones = jnp.ones((SEQ, HEAD_DIM), BF16)
    lam = _lambda(lam_ref, layer)

    def softmax_pv(q, k, v):
        s = lax.dot_general(q, k, NT_DIMS, preferred_element_type=F32)
        e = jnp.exp2(s - jnp.max(s, axis=-1, keepdims=True)).astype(BF16)
        oz = jnp.dot(e, jnp.concatenate([v, ones], axis=1), preferred_element_type=F32)
        return oz[:, :HEAD_DIM] / oz[:, HEAD_DIM:]

    for hh in range(PROMPT_HEADS):
        cols = slice(hh * HEAD_DIM, (hh + 1) * HEAD_DIM)
        nk = nk_ref[:, cols]
        nv = nv_ref[:, cols]
        dk = dk_ref[:, cols]
        dv = dv_ref[:, cols]
        onk_ref[hh] = nk
        onv_ref[hh] = nv
        odk_ref[hh] = dk
        odv_ref[hh] = dv
        q = (nq_ref[:, cols] * (HEAD_DIM ** -0.5 * LOG2E)).astype(BF16)
        on_ref[:, cols] = softmax_pv(q, nk.astype(BF16), nv.astype(BF16)).astype(BF16)
        q1, q2 = _split_q(dq_ref[:, cols] * (DIFF_QK_DIM ** -0.5 * LOG2E))
        kb = dk.astype(BF16)
        vb = dv.astype(BF16)
        o = softmax_pv(q1, kb, vb) - lam * softmax_pv(q2, kb, vb)
        od_ref[:, cols] = _sub_ln(o, sg_ref[...], layer).astype(BF16)


def _attn_prompt(qkv, lam_p, subln, caches, layer):
    hp = PROMPT_HEADS
    h8 = N_HEADS // hp
    blk = lambda off: pl.BlockSpec((SEQ, hp * HEAD_DIM), lambda b, h: (b, off + h))
    in_specs = [blk(0), blk(h8), blk(2 * h8), blk(3 * h8), blk(4 * h8), blk(5 * h8),
                pl.BlockSpec((None, 4, DIFF_QK_DIM), lambda b, h: (layer, 0, 0)),
                pl.BlockSpec((None, 1, HEAD_DIM), lambda b, h: (layer, 0, 0))]
    args = [qkv] * 6 + [lam_p, subln]
    aliases = {}
    for n, cache in enumerate(caches):
        aliases[len(args)] = 2 + n
        in_specs.append(pl.BlockSpec(memory_space=pl.ANY))
        args.append(cache)
    cache_shape = jax.ShapeDtypeStruct((BATCH, DEPTH, N_HEADS, SEQ, HEAD_DIM), F32)
    cache_spec = pl.BlockSpec((None, None, hp, SEQ, HEAD_DIM), lambda b, h: (b, layer, h, 0, 0))
    merged_shape = jax.ShapeDtypeStruct((N_PROMPT, N_HEADS * HEAD_DIM), BF16)
    merged_spec = pl.BlockSpec((SEQ, hp * HEAD_DIM), lambda b, h: (b, h))
    out = pl.pallas_call(
        functools.partial(_attn_prompt_kernel, layer=layer),
        out_shape=(merged_shape, merged_shape,
                   cache_shape, cache_shape, cache_shape, cache_shape),
        grid=(BATCH, h8),
        in_specs=in_specs,
        out_specs=(merged_spec, merged_spec,
                   cache_spec, cache_spec, cache_spec, cache_spec),
        input_output_aliases=aliases,
        compiler_params=_params(32),
        name="attn_prompt",
    )(*args)
    return out[0], out[1], out[2:]


def _attn_latent_kernel(nq_ref, nk_ref, nv_ref, dq_ref, dk_ref, dv_ref,
                        cnk_ref, cnv_ref, cdk_ref, cdv_ref, tab_ref, cos_ref, sin_ref,
                        lam_ref, sg_ref, on_ref, od_ref,
                        kb, vb, ckb, cvb, q1b, q2b, *, layer):
    kb[...] = nk_ref[...].astype(BF16)
    vb[...] = nv_ref[...].astype(BF16)
    ckb[...] = cnk_ref[...].astype(BF16)
    cvb[...] = cnv_ref[...].astype(BF16)
    masked = jnp.full((GRID_W, 2 * GRID_W), MASK_VALUE, F32)

    def with_ones(v):
        return jnp.concatenate([v, jnp.ones(v.shape, BF16)], axis=1)

    left = lax.broadcasted_iota(jnp.int32, (GRID_W, 2 * GRID_W), 1) < GRID_W

    def bias_tile(r, kr):
        lo = min(max(r - WIN_H // 2, 0), GRID_H - WIN_H)
        ok0 = lo <= kr < lo + WIN_H
        ok1 = lo <= kr + 1 < lo + WIN_H
        if not (ok0 or ok1):
            return masked
        tile = tab_ref[kr - r + WIN_H]
        if ok0 and ok1:
            return tile
        return jnp.where(left if ok0 else jnp.logical_not(left), tile, masked)

    for blk in range(GRID_H // NAT_QROWS):
        r0 = blk * NAT_QROWS
        k0 = min(max(r0 - WIN_H // 2, 0), GRID_H - NAT_KROWS)
        rows = slice(r0 * GRID_W, (r0 + NAT_QROWS) * GRID_W)
        keys = slice(k0 * GRID_W, (k0 + NAT_KROWS) * GRID_W)
        q = (nq_ref[rows, :] * (HEAD_DIM ** -0.5 * LOG2E)).astype(BF16)
        bias = jnp.concatenate(
            [jnp.concatenate([bias_tile(r0 + a, k0 + 2 * i) for i in range(NAT_KROWS // 2)], axis=1)
             for a in range(NAT_QROWS)], axis=0)
        s_nb = lax.dot_general(q, kb[keys, :], NT_DIMS, preferred_element_type=F32) + bias
        s_cx = lax.dot_general(q, ckb[...], NT_DIMS, preferred_element_type=F32)
        m = jnp.maximum(jnp.max(s_nb, axis=-1, keepdims=True),
                        jnp.max(s_cx, axis=-1, keepdims=True))
        e_nb = jnp.exp2(s_nb - m).astype(BF16)
        e_cx = jnp.exp2(s_cx - m).astype(BF16)
        oz = (jnp.dot(e_nb, with_ones(vb[keys, :]), preferred_element_type=F32)
              + jnp.dot(e_cx, with_ones(cvb[...]), preferred_element_type=F32))
        on_ref[rows, :] = (oz[:, :HEAD_DIM] / oz[:, HEAD_DIM:]).astype(BF16)

    lane = lax.broadcasted_iota(jnp.int32, (DEC_SEQ, HEAD_DIM), 1)
    first_half = (lane & 16) == 0

    def rope(x):
        rot = jnp.where(first_half, pltpu.roll(x, HEAD_DIM - 16, 1), pltpu.roll(x, 16, 1))
        return x * cos_ref[...] + rot * sin_ref[...]

    q1, q2 = _split_q(rope(dq_ref[...]) * (DIFF_QK_DIM ** -0.5 * LOG2E))
    q1b[...] = q1
    q2b[...] = q2
    kb[...] = rope(dk_ref[...]).astype(BF16)
    vb[...] = dv_ref[...].astype(BF16)
    ckb[...] = cdk_ref[...].astype(BF16)
    cvb[...] = cdv_ref[...].astype(BF16)
    lam = _lambda(lam_ref, layer)

    def diff_block(j, carry):
        r0 = pl.multiple_of(j * Q_BLOCK, Q_BLOCK)

        def softmax_pv(q):
            s_l = lax.dot_general(q, kb[...], NT_DIMS, preferred_element_type=F32)
            s_c = lax.dot_general(q, ckb[...], NT_DIMS, preferred_element_type=F32)
            m = jnp.maximum(jnp.max(s_l, axis=-1, keepdims=True),
                            jnp.max(s_c, axis=-1, keepdims=True))
            e_l = jnp.exp2(s_l - m).astype(BF16)
            e_c = jnp.exp2(s_c - m).astype(BF16)
            oz = (jnp.dot(e_l, with_ones(vb[...]), preferred_element_type=F32)
                  + jnp.dot(e_c, with_ones(cvb[...]), preferred_element_type=F32))
            return oz[:, :HEAD_DIM] / oz[:, HEAD_DIM:]

        o = (softmax_pv(q1b[pl.ds(r0, Q_BLOCK), :])
             - lam * softmax_pv(q2b[pl.ds(r0, Q_BLOCK), :]))
        od_ref[pl.ds(r0, Q_BLOCK), :] = _sub_ln(o, sg_ref[...], layer).astype(BF16)
        return carry

    lax.fori_loop(0, DEC_SEQ // Q_BLOCK, diff_block, 0, unroll=True)


def _attn_latent(qkv, caches, tab, cos, sin, lam_p, subln, layer):
    h8 = N_HEADS
    row0 = N_PROMPT // DEC_SEQ
    blk = lambda off: pl.BlockSpec((DEC_SEQ, HEAD_DIM), lambda b, h: (row0 + b, off + h))
    cblk = pl.BlockSpec((None, None, None, PAST_LEN, HEAD_DIM), lambda b, h: (b, layer, h, 0, 0))
    fixed = pl.BlockSpec((DEC_SEQ, HEAD_DIM), lambda b, h: (0, 0))
    out_spec = pl.BlockSpec((DEC_SEQ, HEAD_DIM), lambda b, h: (b, h))
    merged_shape = jax.ShapeDtypeStruct((N_TOK - N_PROMPT, h8 * HEAD_DIM), BF16)
    return pl.pallas_call(
        functools.partial(_attn_latent_kernel, layer=layer),
        out_shape=(merged_shape, merged_shape),
        grid=(DEC_BATCH, h8),
        in_specs=[blk(0), blk(h8), blk(2 * h8), blk(3 * h8), blk(4 * h8), blk(5 * h8),
                  cblk, cblk, cblk, cblk,
                  pl.BlockSpec((None, 2 * WIN_H, GRID_W, 2 * GRID_W), lambda b, h: (h, 0, 0, 0)),
                  fixed, fixed,
                  pl.BlockSpec((None, 4, DIFF_QK_DIM), lambda b, h: (layer, 0, 0)),
                  pl.BlockSpec((None, 1, HEAD_DIM), lambda b, h: (layer, 0, 0))],
        out_specs=(out_spec, out_spec),
        scratch_shapes=[pltpu.VMEM((DEC_SEQ, HEAD_DIM), BF16), pltpu.VMEM((DEC_SEQ, HEAD_DIM), BF16),
                        pltpu.VMEM((PAST_LEN, HEAD_DIM), BF16), pltpu.VMEM((PAST_LEN, HEAD_DIM), BF16),
                        pltpu.VMEM((DEC_SEQ, HEAD_DIM), BF16), pltpu.VMEM((DEC_SEQ, HEAD_DIM), BF16)],
        compiler_params=_params(48),
        name="attn_latent",
    )(qkv, qkv, qkv, qkv, qkv, qkv, *caches, tab, cos, sin, lam_p, subln)


def _route(logits, bias):
    ex = jnp.exp(logits - jnp.max(logits, axis=0, keepdims=True))
    probs = ex / jnp.sum(ex, axis=0, keepdims=True)
    sel = probs + bias
    srow = [sel[e:e + 1, :] for e in range(N_EXPERTS)]
    prow = [probs[e:e + 1, :] for e in range(N_EXPERTS)]
    gscore = []
    for g in range(N_GROUPS):
        v = srow[g * EXPERTS_PER_GROUP:(g + 1) * EXPERTS_PER_GROUP]
        best = None
        for a in range(EXPERTS_PER_GROUP):
            for b in range(a + 1, EXPERTS_PER_GROUP):
                pair = v[a] + v[b]
                best = pair if best is None else jnp.maximum(best, pair)
        gscore.append(best)
    gbest = gscore[0]
    gidx = jnp.zeros(gbest.shape, jnp.int32)
    for g in range(1, N_GROUPS):
        better = gscore[g] > gbest
        gidx = jnp.where(better, g, gidx)
        gbest = jnp.where(better, gscore[g], gbest)
    neg = jnp.full(gbest.shape, -jnp.inf, F32)
    picks = []
    taken = None
    for _ in range(2):
        best = neg
        idx = jnp.zeros(gbest.shape, jnp.int32)
        wgt = jnp.zeros(gbest.shape, F32)
        for e in range(N_EXPERTS):
            ok = gidx == (e // EXPERTS_PER_GROUP)
            if taken is not None:
                ok = jnp.logical_and(ok, taken != e)
            cand = jnp.where(ok, srow[e], neg)
            better = cand > best
            idx = jnp.where(better, e, idx)
            wgt = jnp.where(better, prow[e], wgt)
            best = jnp.where(better, cand, best)
        picks.append((idx, wgt))
        taken = idx
    (i0, w0), (i1, w1) = picks
    wsum = w0 + w1
    return i0, i1, w0 / wsum, w1 / wsum


def _split_bf16(v):
    hi = v.astype(BF16)
    lo = (v - hi.astype(F32)).astype(BF16)
    return hi, lo


def _load_resident_weight(w_hbm, wbf, stage, sem, n_chunks):
    ck = wbf.shape[0] // n_chunks
    copies = [pltpu.make_async_copy(w_hbm.at[pl.ds(c * ck, ck), :], stage.at[c % 2], sem.at[c % 2])
              for c in range(n_chunks)]
    copies[0].start()
    for c in range(n_chunks):
        if c + 1 < n_chunks:
            copies[c + 1].start()
        copies[c].wait()
        wbf[c * ck:(c + 1) * ck, :] = stage[c % 2].astype(BF16)


def _out_proj_kernel(pn_ref, pd_ref, sn_ref, sd_ref, w_hbm, *rest, layer, split_x):
    if split_x:
        x_in = _select_stream(rest[0], rest[1], OUTPROJ_TM)
        rest = rest[2:]
    else:
        x_in = rest[0][...]
        rest = rest[1:]
    (mod_ref, g_ref, wr_ref, rb_ref, xo_ref, h_ref, idx_ref, cw_ref, wbf, stage, mbuf, sem) = rest
    i = pl.program_id(0)
    half = D_MODEL // 2

    @pl.when(i == 0)
    def _():
        _load_resident_weight(w_hbm.at[layer], wbf, stage, sem, OUTPROJ_WCHUNKS)

    prompt = i < N_PROMPT // OUTPROJ_TM

    @pl.when(prompt)
    def _():
        mbuf[:, :half] = pn_ref[...]
        mbuf[:, half:] = pd_ref[...]

    @pl.when(jnp.logical_not(prompt))
    def _():
        mbuf[:, :half] = sn_ref[...]
        mbuf[:, half:] = sd_ref[...]

    a = jnp.dot(mbuf[...], wbf[...], preferred_element_type=F32)
    x = x_in + mod_ref[2:3, :] * a
    xo_ref[...] = x
    h = _rms(x, g_ref[...]) * (1.0 + mod_ref[4:5, :]) + mod_ref[3:4, :]
    h_ref[...] = h
    w_hi, w_lo = _split_bf16(wr_ref[...])
    h_hi, h_lo = _split_bf16(h)
    part = lax.dot_general(jnp.concatenate([w_hi, w_lo], axis=0), h_hi, NT_DIMS,
                           preferred_element_type=F32)
    logits = (part[:N_EXPERTS] + part[N_EXPERTS:]
              + lax.dot_general(w_hi, h_lo, NT_DIMS, preferred_element_type=F32))
    i0, i1, w0, w1 = _route(logits, rb_ref[...])
    idx_ref[...] = jnp.concatenate([i0, i1], axis=0)
    cw_ref[...] = jnp.concatenate([w0, w1], axis=0)


def _out_proj(merged, w_out, x, mods, layer, gain, w_router_t, router_bias):
    tm = OUTPROJ_TM
    split_x = isinstance(x, tuple)
    x_args = list(x) if split_x else [x]
    x_specs = _stream_specs(tm) if split_x else [pl.BlockSpec((tm, D_MODEL), lambda i: (i, 0))]
    half = D_MODEL // 2
    npt = N_PROMPT // tm
    row = lambda i: (i, 0)
    fixed = lambda i: (0, 0)
    p_row = lambda i: (jnp.minimum(i, npt - 1), 0)
    s_row = lambda i: (jnp.maximum(i - npt, 0), 0)
    return pl.pallas_call(
        functools.partial(_out_proj_kernel, layer=layer, split_x=split_x),
        out_shape=(jax.ShapeDtypeStruct((N_TOK, D_MODEL), F32),
                   jax.ShapeDtypeStruct((N_TOK, D_MODEL), F32),
                   jax.ShapeDtypeStruct((2, N_TOK), jnp.int32),
                   jax.ShapeDtypeStruct((2, N_TOK), F32)),
        grid=(N_TOK // tm,),
        in_specs=[
            pl.BlockSpec((tm, half), p_row),
            pl.BlockSpec((tm, half), p_row),
            pl.BlockSpec((tm, half), s_row),
            pl.BlockSpec((tm, half), s_row),
            pl.BlockSpec(memory_space=pl.ANY),
        ] + x_specs + [
            pl.BlockSpec((None, None, 6, D_MODEL), lambda i: (layer, _mod_row(i, tm), 0, 0)),
            pl.BlockSpec((1, D_MODEL), fixed),
            pl.BlockSpec((N_EXPERTS, D_MODEL), fixed),
            pl.BlockSpec((N_EXPERTS, 1), fixed),
        ],
        out_specs=(pl.BlockSpec((tm, D_MODEL), row), pl.BlockSpec((tm, D_MODEL), row),
                   pl.BlockSpec((2, tm), lambda i: (0, i)),
                   pl.BlockSpec((2, tm), lambda i: (0, i))),
        scratch_shapes=[pltpu.VMEM((D_MODEL, D_MODEL), BF16),
                        pltpu.VMEM((2, D_MODEL // OUTPROJ_WCHUNKS, D_MODEL), F32),
                        pltpu.VMEM((tm, D_MODEL), BF16),
                        pltpu.SemaphoreType.DMA((2,))],
        compiler_params=_params(48),
        name="out_proj_router",
    )(*merged, w_out, *x_args, mods, gain, w_router_t, router_bias)


def _dispatch_kernel(pos_ref, h_ref, xs_in, xs_out, sem):
    del xs_in
    tm = DISPATCH_TM
    base = pl.program_id(0) * tm

    for r in range(tm):
        for k in range(2):
            p = pos_ref[k * N_TOK + base + r]
            pltpu.make_async_copy(h_ref.at[pl.ds(r, 1), :], xs_out.at[pl.ds(p, 1), :], sem).start()
    for _ in range(2):
        pltpu.make_async_copy(h_ref, xs_out.at[pl.ds(0, tm), :], sem).wait()


def _dispatch(pos, h2, xs_buf):
    tm = DISPATCH_TM
    return pl.pallas_call(
        _dispatch_kernel,
        out_shape=jax.ShapeDtypeStruct((EXPERT_CAP, D_MODEL), F32),
        grid_spec=pltpu.PrefetchScalarGridSpec(
            num_scalar_prefetch=1,
            grid=(N_TOK // tm,),
            in_specs=[pl.BlockSpec((tm, D_MODEL), lambda i, pos_ref: (i, 0)),
                      pl.BlockSpec(memory_space=pl.ANY)],
            out_specs=pl.BlockSpec(memory_space=pl.ANY),
            scratch_shapes=[pltpu.SemaphoreType.DMA(())],
        ),
        input_output_aliases={2: 0},
        compiler_params=_params(32),
        name="expert_dispatch",
    )(pos, h2, xs_buf)


def _experts_kernel(we_ref, wc_ref, wdo_ref, wsl_ref, ti_ref, tdo_ref, tsl_ref,
                    xs_ref, wg_ref, wu_ref, wd_ref, o_ref, wg_bf, wu_bf, wd_bf):
    del we_ref, ti_ref
    s = pl.program_id(0)

    def cast_chunk():
        slot = wsl_ref[s]
        c = wc_ref[s]
        ck = D_MODEL // EXPERT_WCHUNKS
        r0 = pl.multiple_of(c * ck, ck)
        wg_bf[slot, pl.ds(r0, ck), :] = wg_ref[...].astype(BF16)
        wu_bf[slot, pl.ds(r0, ck), :] = wu_ref[...].astype(BF16)
        cf = D_FF // EXPERT_WCHUNKS
        r1 = pl.multiple_of(c * cf, cf)
        wd_bf[slot, pl.ds(r1, cf), :] = wd_ref[...].astype(BF16)

    def run_tile():
        slot = tsl_ref[s]
        xb = xs_ref[...].astype(BF16)
        g = jnp.dot(xb, wg_bf[slot], preferred_element_type=F32)
        u = jnp.dot(xb, wu_bf[slot], preferred_element_type=F32)
        hid = (g * jax.nn.sigmoid(g) * u).astype(BF16)
        o_ref[...] = jnp.dot(hid, wd_bf[slot], preferred_element_type=F32)

    load = wdo_ref[s] == 1
    tile = tdo_ref[s] == 1

    @pl.when(jnp.logical_and(load, tile))
    def _():
        cast_chunk()
        run_tile()

    @pl.when(jnp.logical_and(load, jnp.logical_not(tile)))
    def _():
        cast_chunk()

    @pl.when(jnp.logical_and(tile, jnp.logical_not(load)))
    def _():
        run_tile()

    @pl.when(tdo_ref[s] == 2)
    def _():
        o_ref[...] = jnp.zeros_like(o_ref)


def _experts(plan, xs, w_gate, w_up, w_down, layer):
    tm = EXPERT_TM
    ck = D_MODEL // EXPERT_WCHUNKS
    cf = D_FF // EXPERT_WCHUNKS
    tile = lambda s, we, wc, wdo, wsl, ti, tdo, tsl: (ti[s], 0)
    wblk = lambda s, we, wc, wdo, wsl, ti, tdo, tsl: (layer, we[s], wc[s], 0)
    return pl.pallas_call(
        _experts_kernel,
        out_shape=jax.ShapeDtypeStruct((EXPERT_CAP, D_MODEL), F32),
        grid_spec=pltpu.PrefetchScalarGridSpec(
            num_scalar_prefetch=7,
            grid=(EXPERT_STEPS,),
            in_specs=[
                pl.BlockSpec((tm, D_MODEL), tile),
                pl.BlockSpec((None, None, ck, D_FF), wblk),
                pl.BlockSpec((None, None, ck, D_FF), wblk),
                pl.BlockSpec((None, None, cf, D_MODEL), wblk),
            ],
            out_specs=pl.BlockSpec((tm, D_MODEL), tile),
            scratch_shapes=[pltpu.VMEM((2, D_MODEL, D_FF), BF16),
                            pltpu.VMEM((2, D_MODEL, D_FF), BF16),
                            pltpu.VMEM((2, D_FF, D_MODEL), BF16)],
        ),
        compiler_params=_params(56),
        name="expert_mlp",
    )(*plan, xs, w_gate, w_up, w_down)


def _dispatch_plan(idx):
    e = idx.reshape(-1)
    onehot = jnp.arange(N_EXPERTS, dtype=jnp.int32)[:, None] == e[None, :]
    nb = N_PAIRS // PLAN_BLOCK
    upper = jnp.triu(jnp.ones((PLAN_BLOCK, PLAN_BLOCK), F32))
    within = jnp.einsum('ebj,ji->ebi', onehot.astype(F32).reshape(N_EXPERTS, nb, PLAN_BLOCK), upper,
                        preferred_element_type=F32)
    totals = within[:, :, -1]
    before = jnp.cumsum(totals, axis=1) - totals
    csum = (within + before[:, :, None]).reshape(N_EXPERTS, N_PAIRS)
    counts = (before[:, -1] + totals[:, -1]).astype(jnp.int32)
    nt = (counts + EXPERT_TM - 1) // EXPERT_TM
    tile_end = jnp.cumsum(nt)
    tile_start = tile_end - nt
    first_row = (tile_start * EXPERT_TM).astype(F32)[:, None]
    pos = jnp.sum(jnp.where(onehot, csum - 1.0 + first_row, 0.0), axis=0)

    nw = EXPERT_WCHUNKS
    length = jnp.maximum(nt, nw)
    phase_end = nw + jnp.cumsum(length)
    phase_start = phase_end - length
    s = jnp.arange(EXPERT_STEPS, dtype=jnp.int32)
    started = s[:, None] >= phase_start[None, :]
    ex = jnp.sum(started.astype(jnp.int32), axis=1) - 1
    running = started & (s[:, None] < phase_end[None, :])

    def of_phase(v):
        return jnp.sum(jnp.where(running, v[None, :], 0), axis=1)

    k = s - of_phase(phase_start)
    nt_s = of_phase(nt)
    ts_s = of_phase(tile_start)
    live = s < phase_end[-1:]
    last = N_EXPERTS - 1
    w_do = (live & (k < nw) & (ex < last)).astype(jnp.int32)
    w_e = jnp.minimum(ex + 1, last)
    w_c = jnp.where(ex < last, jnp.minimum(k, nw - 1), nw - 1)
    w_slot = (ex + 1) % 2
    t_do = (live & (k < nt_s)).astype(jnp.int32)
    t_idx = jnp.maximum(ts_s + jnp.minimum(k + 1, nt_s) - 1, 0)
    z = tile_end[-1:] + (s - phase_end[-1:])
    t_do = jnp.where(jnp.logical_not(live) & (z < EXPERT_TILES), 2, t_do)
    t_idx = jnp.where(live, t_idx, jnp.minimum(z, EXPERT_TILES - 1))
    t_slot = jnp.maximum(ex, 0) % 2
    plan = tuple(a.astype(jnp.int32) for a in (w_e, w_c, w_do, w_slot, t_idx, t_do, t_slot))
    return pos.astype(jnp.int32), plan


def _bias_tables(nat_rpb):
    w = GRID_W
    c = jnp.arange(w)[:, None]
    kc = jnp.arange(w)[None, :]
    qstart = jnp.clip(c - WIN_W // 2, 0, w - WIN_W)
    valid = (kc >= qstart) & (kc < qstart + WIN_W)
    n_off = nat_rpb.shape[-1]
    select = ((kc - c + WIN_W - 1)[None] == jnp.arange(n_off)[:, None, None]).astype(F32)
    tab = jnp.einsum('lhdj,jck->lhdck', nat_rpb, select, precision=lax.Precision.HIGHEST)
    tab = jnp.where(valid, tab * LOG2E, MASK_VALUE)
    tab = jnp.pad(tab, [(0, 0), (0, 0), (1, 1), (0, 0), (0, 0)], constant_values=MASK_VALUE)
    return jnp.concatenate([tab[:, :, :-1], tab[:, :, 1:]], axis=-1)


def _rope_tables():
    t = jnp.arange(DEC_SEQ)
    row = (t // GRID_W).astype(F32)
    col = (t % GRID_W).astype(F32)
    nf = DIFF_QK_DIM // 4
    inv = ROPE_BASE ** (-jnp.arange(nf, dtype=F32) / nf)
    ar = row[:, None] * inv[None, :]
    ac = col[:, None] * inv[None, :]
    cos = jnp.concatenate([jnp.cos(ar), jnp.cos(ar), jnp.cos(ac), jnp.cos(ac)], axis=-1)
    sin = jnp.concatenate([-jnp.sin(ar), jnp.sin(ar), -jnp.sin(ac), jnp.sin(ac)], axis=-1)
    return jnp.tile(cos, (1, 2)), jnp.tile(sin, (1, 2))


def kernel(x_prompt, x_sample, cache_nat_k, cache_nat_v, cache_diff_k, cache_diff_v, c, c_ctx,
           w_ada, b_ada, norm1, norm2, norm_final, w_in, w_out, nat_rpb, diff_lambda, diff_subln,
           w_router, router_bias, w_gate, w_up, w_down):
    x = (x_prompt.reshape(N_PROMPT, D_MODEL), x_sample.reshape(DEC_BATCH * DEC_SEQ, D_MODEL))
    cond =jnp.concatenate([c_ctx[None, :], c, jnp.zeros((8 - 1 - DEC_BATCH, D_MODEL), F32)], axis=0)
    mods = _modulation(cond, w_ada, b_ada)
    tabs = _bias_tables(nat_rpb)
    cos, sin = _rope_tables()
    subln = diff_subln.reshape(DEPTH, 1, HEAD_DIM)
    w_router_t = w_router.T
    rbias = router_bias.reshape(N_EXPERTS, 1)
    lat_caches = (cache_nat_k, cache_nat_v, cache_diff_k, cache_diff_v)

    new_caches = tuple(jnp.zeros((BATCH, DEPTH, N_HEADS, SEQ, HEAD_DIM), F32) for _ in range(4))
    xs_buf = jnp.zeros((EXPERT_CAP, D_MODEL), F32)
    ys = pos = cw = None
    for layer in range(DEPTH):
        gain1 = norm1[layer].reshape(1, D_MODEL)
        if layer == 0:
            h = _pre_first(*x, mods, layer, gain1)
        else:
            x, h = _pre_combine(x, ys, pos, cw, mods, layer - 1, gain1, final=False)
        qkv = _in_proj(h, w_in, layer)
        pn, pd, new_caches = _attn_prompt(qkv, diff_lambda, subln, new_caches, layer)
        sn, sd = _attn_latent(qkv, lat_caches, tabs[layer], cos, sin, diff_lambda, subln, layer)
        x, h2, idx, cwt = _out_proj((pn, pd, sn, sd), w_out, x, mods, layer,
                                    norm2[layer].reshape(1, D_MODEL), w_router_t, rbias)
        pos, plan = _dispatch_plan(idx)
        cw = cwt.T
        xs_buf = _dispatch(pos, h2, xs_buf)
        ys = _experts(plan, xs_buf, w_gate, w_up, w_down, layer)
    yp, ysm = _pre_combine(x, ys, pos, cw, mods, DEPTH - 1, norm_final.reshape(1, D_MODEL), final=True)
    return (yp.reshape(BATCH, SEQ, D_MODEL), ysm.reshape(DEC_BATCH, DEC_SEQ, D_MODEL)) + tuple(new_caches)
```

```python
import functools
import math

import jax
import jax.numpy as jnp
from jax import lax
from jax.experimental import pallas as pl
from jax.experimental.pallas import tpu as pltpu

D_MODEL = 2048
BATCH = 16
SEQ = 256
DEPTH = 4
DEC_BATCH = 4
DEC_SEQ = 1024
PAST_LEN = 512
GRID_W = 64
GRID_H = DEC_SEQ // GRID_W
N_HEADS = 8
HEAD_DIM = 128
DIFF_QK_DIM = 64
WIN_H = 8
WIN_W = 16
N_EXPERTS = 16
N_GROUPS = 4
EXPERTS_PER_GROUP = N_EXPERTS // N_GROUPS
D_FF = 1024
ROPE_BASE = 10000.0
EPS = 1e-6
IN_COLS = 6 * N_HEADS * HEAD_DIM
N_PROMPT = BATCH * SEQ
N_TOK = N_PROMPT + DEC_BATCH * DEC_SEQ
N_PAIRS = 2 * N_TOK

PRE_TM = 256
INPROJ_TM = 2048
INPROJ_TN = 256
OUTPROJ_TM = 256
OUTPROJ_WCHUNKS = 4
DISPATCH_TM = 256
EXPERT_TM = 256
EXPERT_CAP = N_PAIRS + N_EXPERTS * EXPERT_TM
EXPERT_TILES = EXPERT_CAP // EXPERT_TM
EXPERT_WCHUNKS = 4
EXPERT_STEPS = EXPERT_WCHUNKS * (N_EXPERTS + 1) + EXPERT_TILES
PLAN_BLOCK = 128
MOD_TN = 1024
Q_BLOCK = 256
PROMPT_HEADS = 4
NAT_QROWS = 4
NAT_KROWS = 12
LOG2E = 1.4426950408889634
MASK_VALUE = -1e30

F32 = jnp.float32
BF16 = jnp.bfloat16
MIB = 1024 * 1024
NT_DIMS = (((1,), (1,)), ((), ()))


def _params(vmem_mib):
    return pltpu.CompilerParams(vmem_limit_bytes=vmem_mib * MIB)


def _lam_init(layer):
    return 0.8 - 0.6 * math.exp(-0.3 * layer)


def _rms(x, gain):
    return x * lax.rsqrt(jnp.mean(x * x, axis=-1, keepdims=True) + EPS) * gain


def _mod_row(i, tm):
    first = i * tm
    return jnp.where(first < N_PROMPT, 0, 1 + (first - N_PROMPT) // DEC_SEQ)


def _mod_kernel(cond_ref, w_ref, b_ref, o_ref):
    c = cond_ref[...]
    s = c * jax.nn.sigmoid(c)
    o_ref[...] = jnp.dot(s.astype(BF16), w_ref[...].astype(BF16),
                         preferred_element_type=F32) + b_ref[...]


def _modulation(cond, w_ada, b_ada):
    n = 6 * D_MODEL
    out = pl.pallas_call(
        _mod_kernel,
        out_shape=jax.ShapeDtypeStruct((DEPTH, 8, n), F32),
        grid=(DEPTH, n // MOD_TN),
        in_specs=[
            pl.BlockSpec((8, D_MODEL), lambda l, j: (0, 0)),
            pl.BlockSpec((None, D_MODEL, MOD_TN), lambda l, j: (l, 0, j)),
            pl.BlockSpec((None, 1, MOD_TN), lambda l, j: (l, 0, j)),
        ],
        out_specs=pl.BlockSpec((None, 8, MOD_TN), lambda l, j: (l, 0, j)),
        compiler_params=_params(40),
        name="adaln_modulation",
    )(cond, w_ada, b_ada.reshape(DEPTH, 1, n))
    return out.reshape(DEPTH, 8, 6, D_MODEL)


def _gather_expert_rows(pos_ref, ys_hbm, ybuf, sem, tm):
    i = pl.program_id(0)
    n = pl.num_programs(0)

    def issue_tile(tile, slot):
        for r in range(tm):
            for k in range(2):
                p = pos_ref[k * N_TOK + tile * tm + r]
                pltpu.make_async_copy(ys_hbm.at[pl.ds(p, 1), :],
                                      ybuf.at[slot, k, pl.ds(r, 1), :], sem.at[slot, k]).start()

    @pl.when(i == 0)
    def _():
        issue_tile(0, 0)

    @pl.when(i + 1 < n)
    def _():
        issue_tile(i + 1, (i + 1) % 2)

    slot = i % 2
    for k in range(2):
        pltpu.make_async_copy(ys_hbm.at[pl.ds(0, tm), :], ybuf.at[slot, k], sem.at[slot, k]).wait()
    return slot


def _select_stream(xp_ref, xs_ref, tm):
    prompt = pl.program_id(0) < N_PROMPT // tm
    return jnp.where(prompt, xp_ref[...], xs_ref[...])


def _pre_first_kernel(xp_ref, xs_ref, mod_ref, g_ref, h_ref):
    x = _select_stream(xp_ref, xs_ref, PRE_TM)
    h = _rms(x, g_ref[...]) * (1.0 + mod_ref[1:2, :]) + mod_ref[0:1, :]
    h_ref[...] = h.astype(BF16)


def _pre_mid_kernel(pos_ref, x_ref, ys_hbm, cw_ref, modp_ref, mod_ref, g_ref,
                    xo_ref, h_ref, ybuf, sem):
    slot = _gather_expert_rows(pos_ref, ys_hbm, ybuf, sem, PRE_TM)
    cw = cw_ref[...]
    y = cw[:, 0:1] * ybuf[slot, 0] + cw[:, 1:2] * ybuf[slot, 1]
    x = x_ref[...] + modp_ref[5:6, :] * y
    xo_ref[...] = x
    h = _rms(x, g_ref[...]) * (1.0 + mod_ref[1:2, :]) + mod_ref[0:1, :]
    h_ref[...] = h.astype(BF16)


def _pre_final_kernel(pos_ref, x_ref, ys_hbm, cw_ref, modp_ref, g_ref, yp_ref, ys_ref, ybuf, sem):
    slot = _gather_expert_rows(pos_ref, ys_hbm, ybuf, sem, PRE_TM)
    cw = cw_ref[...]
    y = cw[:, 0:1] * ybuf[slot, 0] + cw[:, 1:2] * ybuf[slot, 1]
    x = x_ref[...] + modp_ref[5:6, :] * y
    out = _rms(x, g_ref[...])
    prompt = pl.program_id(0) < N_PROMPT // PRE_TM

    @pl.when(prompt)
    def _():
        yp_ref[...] = out

    @pl.when(jnp.logical_not(prompt))
    def _():
        ys_ref[...] = out


def _stream_specs(tm):
    npt = N_PROMPT // tm
    return [pl.BlockSpec((tm, D_MODEL), lambda i, *_: (jnp.minimum(i, npt - 1), 0)),
            pl.BlockSpec((tm, D_MODEL), lambda i, *_: (jnp.maximum(i - npt, 0), 0))]


def _pre_first(xp, xs, mods, layer, gain):
    tm = PRE_TM
    return pl.pallas_call(
        _pre_first_kernel,
        out_shape=jax.ShapeDtypeStruct((N_TOK, D_MODEL), BF16),
        grid=(N_TOK // tm,),
        in_specs=_stream_specs(tm) + [
            pl.BlockSpec((None, None, 6, D_MODEL), lambda i: (layer, _mod_row(i, tm), 0, 0)),
            pl.BlockSpec((1, D_MODEL), lambda i: (0, 0)),
        ],
        out_specs=pl.BlockSpec((tm, D_MODEL), lambda i: (i, 0)),
        compiler_params=_params(32),
        name="pre_first",
    )(xp, xs, mods, gain)


def _pre_combine(x, ys, pos, cw, mods, prev_layer, gain, final):
    tm = PRE_TM
    row = lambda i, pos_ref: (i, 0)
    fixed = lambda i, pos_ref: (0, 0)
    in_specs = [
        pl.BlockSpec((tm, D_MODEL), row),
        pl.BlockSpec(memory_space=pl.ANY),
        pl.BlockSpec((tm, 2), row),
        pl.BlockSpec((None, None, 6, D_MODEL),
                     lambda i, pos_ref: (prev_layer, _mod_row(i, tm), 0, 0)),
    ]
    args = [x, ys, cw, mods]
    if final:
        kern = _pre_final_kernel
        out_shape = (jax.ShapeDtypeStruct((N_PROMPT, D_MODEL), F32),
                     jax.ShapeDtypeStruct((N_TOK - N_PROMPT, D_MODEL), F32))
        out_specs = tuple(_stream_specs(tm))
    else:
        kern = _pre_mid_kernel
        in_specs.append(pl.BlockSpec((None, None, 6, D_MODEL),
                                     lambda i, pos_ref: (prev_layer + 1, _mod_row(i, tm), 0, 0)))
        args.append(mods)
        out_shape = (jax.ShapeDtypeStruct((N_TOK, D_MODEL), F32),
                     jax.ShapeDtypeStruct((N_TOK, D_MODEL), BF16))
        out_specs = (pl.BlockSpec((tm, D_MODEL), row), pl.BlockSpec((tm, D_MODEL), row))
    in_specs.append(pl.BlockSpec((1, D_MODEL), fixed))
    args.append(gain)
    return pl.pallas_call(
        kern,
        out_shape=out_shape,
        grid_spec=pltpu.PrefetchScalarGridSpec(
            num_scalar_prefetch=1,
            grid=(N_TOK // tm,),
            in_specs=in_specs,
            out_specs=out_specs,
            scratch_shapes=[pltpu.VMEM((2, 2, tm, D_MODEL), F32), pltpu.SemaphoreType.DMA((2, 2))],
        ),
        compiler_params=_params(40),
        name="pre_final" if final else "pre_combine",
    )(pos, *args)


SECTION = N_HEADS * HEAD_DIM
INPROJ_SECTION_BLOCKS = SECTION // INPROJ_TN
CACHE_SECTIONS = (1, 2, 4, 5)


def _in_proj_kernel(h_ref, w_ref, qs_ref, c0_in, c1_in, c2_in, c3_in, o_ref, *cache_refs):
    del c0_in, c1_in, c2_in, c3_in
    i = pl.program_id(0)
    j = pl.program_id(1)
    acc = jnp.dot(h_ref[...], w_ref[...].astype(BF16), preferred_element_type=F32)
    o_ref[...] = (acc * qs_ref[...]).astype(BF16)
    prompt = i < N_PROMPT // INPROJ_TM
    for cache_ref, section in zip(cache_refs, CACHE_SECTIONS):
        first = section * INPROJ_SECTION_BLOCKS

        @pl.when(jnp.logical_and(prompt, jnp.logical_and(j >= first, j < first + INPROJ_SECTION_BLOCKS)))
        def _(cache_ref=cache_ref):
            for b in range(INPROJ_TM // SEQ):
                for hh in range(INPROJ_TN // HEAD_DIM):
                    cache_ref[b, hh] = acc[b * SEQ:(b + 1) * SEQ, hh * HEAD_DIM:(hh + 1) * HEAD_DIM]


def _in_proj(h, w_in, qscale, caches, layer):
    tm, tn = INPROJ_TM, INPROJ_TN
    npi = N_PROMPT // tm
    nb = INPROJ_SECTION_BLOCKS

    def cache_spec(section):
        first = section * nb

        def index(i, j):
            head_blk = jnp.where(i < npi, jnp.clip(j - first, 0, nb - 1), nb - 1)
            return (jnp.minimum(i, npi - 1), layer, head_blk, 0, 0)

        return pl.BlockSpec((tm // SEQ, None, tn // HEAD_DIM, SEQ, HEAD_DIM), index)

    cache_shape = jax.ShapeDtypeStruct((BATCH, DEPTH, N_HEADS, SEQ, HEAD_DIM), F32)
    out = pl.pallas_call(
        _in_proj_kernel,
        out_shape=(jax.ShapeDtypeStruct((N_TOK, IN_COLS), BF16),) + (cache_shape,) * 4,
        grid=(N_TOK // tm, IN_COLS // tn),
        in_specs=[
            pl.BlockSpec((tm, D_MODEL), lambda i, j: (i, 0)),
            pl.BlockSpec((None, D_MODEL, tn), lambda i, j: (layer, 0, j)),
            pl.BlockSpec((1, tn), lambda i, j: (0, j)),
        ] + [pl.BlockSpec(memory_space=pl.ANY)] * 4,
        out_specs=(pl.BlockSpec((tm, tn), lambda i, j: (i, j)),)
        + tuple(cache_spec(section) for section in CACHE_SECTIONS),
        input_output_aliases={3: 1, 4: 2, 5: 3, 6: 4},
        compiler_params=_params(48),
        name="in_proj",
    )(h, w_in, qscale, *caches)
    return out[0], out[1:]


def _lambda(lam_ref, layer):
    lp = lam_ref[...]
    a = jnp.sum(lp[0:1, :] * lp[1:2, :], axis=-1, keepdims=True)
    b = jnp.sum(lp[2:3, :] * lp[3:4, :], axis=-1, keepdims=True)
    return jnp.exp(a) - jnp.exp(b) + _lam_init(layer)


def _split_q(q):
    lane = lax.broadcasted_iota(jnp.int32, q.shape, 1)
    q1 = jnp.where(lane < DIFF_QK_DIM, q, 0.0).astype(BF16)
    q2 = jnp.where(lane >= DIFF_QK_DIM, q, 0.0).astype(BF16)
    return q1, q2


def _sub_ln(o, gain, layer):
    return _rms(o, gain) * (1.0 - _lam_init(layer))


def _attn_prompt_kernel(nq_ref, nk_ref, nv_ref, dq_ref, dk_ref, dv_ref, lam_ref, sg_ref,
                        on_ref, od_ref, *, layer):
    ones = jnp.ones((SEQ, HEAD_DIM), BF16)
    lam = _lambda(lam_ref, layer)

    def softmax_pv(q, k, v):
        s = lax.dot_general(q, k, NT_DIMS, preferred_element_type=F32)
        e = jnp.exp2(s - jnp.max(s, axis=-1, keepdims=True)).astype(BF16)
        oz = jnp.dot(e, jnp.concatenate([v, ones], axis=1), preferred_element_type=F32)
        return oz[:, :HEAD_DIM] / oz[:, HEAD_DIM:]

    for hh in range(PROMPT_HEADS):
        cols = slice(hh * HEAD_DIM, (hh + 1) * HEAD_DIM)
        on_ref[:, cols] = softmax_pv(nq_ref[:, cols], nk_ref[:, cols], nv_ref[:, cols]).astype(BF16)
        q1, q2 = _split_q(dq_ref[:, cols])
        kb = dk_ref[:, cols]
        vb = dv_ref[:, cols]
        o = softmax_pv(q1, kb, vb) - lam * softmax_pv(q2, kb, vb)
        od_ref[:, cols] = _sub_ln(o, sg_ref[...], layer).astype(BF16)


def _attn_prompt(qkv, lam_p, subln, layer):
    hp = PROMPT_HEADS
    h8 = N_HEADS // hp
    blk = lambda off: pl.BlockSpec((SEQ, hp * HEAD_DIM), lambda b, h: (b, off + h))
    in_specs = [blk(0), blk(h8), blk(2 * h8), blk(3 * h8), blk(4 * h8), blk(5 * h8),
                pl.BlockSpec((None, 4, DIFF_QK_DIM), lambda b, h: (layer, 0, 0)),
                pl.BlockSpec((None, 1, HEAD_DIM), lambda b, h: (layer, 0, 0))]
    args = [qkv] * 6 + [lam_p, subln]
    merged_shape = jax.ShapeDtypeStruct((N_PROMPT, N_HEADS * HEAD_DIM), BF16)
    merged_spec = pl.BlockSpec((SEQ, hp * HEAD_DIM), lambda b, h: (b, h))
    return pl.pallas_call(
        functools.partial(_attn_prompt_kernel, layer=layer),
        out_shape=(merged_shape, merged_shape),
        grid=(BATCH, h8),
        in_specs=in_specs,
        out_specs=(merged_spec, merged_spec),
        compiler_params=_params(32),
        name="attn_prompt",
    )(*args)


def _attn_latent_kernel(nq_ref, nk_ref, nv_ref, dq_ref, dk_ref, dv_ref,
                        cnk_ref, cnv_ref, cdk_ref, cdv_ref, tab_ref, cos_ref, sin_ref,
                        lam_ref, sg_ref, on_ref, od_ref,
                        kb, ckb, cvb, q1b, q2b, *, layer):
    ckb[...] = cnk_ref[...].astype(BF16)
    cvb[...] = cnv_ref[...].astype(BF16)
    masked = jnp.full((GRID_W, 2 * GRID_W), MASK_VALUE, F32)

    def with_ones(v):
        return jnp.concatenate([v, jnp.ones(v.shape, BF16)], axis=1)

    left = lax.broadcasted_iota(jnp.int32, (GRID_W, 2 * GRID_W), 1) < GRID_W

    def bias_tile(r, kr):
        lo = min(max(r - WIN_H // 2, 0), GRID_H - WIN_H)
        ok0 = lo <= kr < lo + WIN_H
        ok1 = lo <= kr + 1 < lo + WIN_H
        if not (ok0 or ok1):
            return masked
        tile = tab_ref[kr - r + WIN_H]
        if ok0 and ok1:
            return tile
        return jnp.where(left if ok0 else jnp.logical_not(left), tile, masked)

    for blk in range(GRID_H // NAT_QROWS):
        r0 = blk * NAT_QROWS
        k0 = min(max(r0 - WIN_H // 2, 0), GRID_H - NAT_KROWS)
        rows = slice(r0 * GRID_W, (r0 + NAT_QROWS) * GRID_W)
        keys = slice(k0 * GRID_W, (k0 + NAT_KROWS) * GRID_W)
        q = nq_ref[rows, :]
        bias = jnp.concatenate(
            [jnp.concatenate([bias_tile(r0 + a, k0 + 2 * i) for i in range(NAT_KROWS // 2)], axis=1)
             for a in range(NAT_QROWS)], axis=0)
        s_nb = lax.dot_general(q, nk_ref[keys, :], NT_DIMS, preferred_element_type=F32) + bias
        s_cx = lax.dot_general(q, ckb[...], NT_DIMS, preferred_element_type=F32)
        m = jnp.maximum(jnp.max(s_nb, axis=-1, keepdims=True),
                        jnp.max(s_cx, axis=-1, keepdims=True))
        e_nb = jnp.exp2(s_nb - m).astype(BF16)
        e_cx = jnp.exp2(s_cx - m).astype(BF16)
        oz = (jnp.dot(e_nb, with_ones(nv_ref[keys, :]), preferred_element_type=F32)
              + jnp.dot(e_cx, with_ones(cvb[...]), preferred_element_type=F32))
        on_ref[rows, :] = (oz[:, :HEAD_DIM] / oz[:, HEAD_DIM:]).astype(BF16)

    lane = lax.broadcasted_iota(jnp.int32, (DEC_SEQ, HEAD_DIM), 1)
    first_half = (lane & 16) == 0

    def rope(x):
        rot = jnp.where(first_half, pltpu.roll(x, HEAD_DIM - 16, 1), pltpu.roll(x, 16, 1))
        return x * cos_ref[...] + rot * sin_ref[...]

    q1, q2 = _split_q(rope(dq_ref[...].astype(F32)))
    q1b[...] = q1
    q2b[...] = q2
    kb[...] = rope(dk_ref[...].astype(F32)).astype(BF16)
    ckb[...] = cdk_ref[...].astype(BF16)
    cvb[...] = cdv_ref[...].astype(BF16)
    lam = _lambda(lam_ref, layer)

    def diff_block(j, carry):
        r0 = pl.multiple_of(j * Q_BLOCK, Q_BLOCK)

        def softmax_pv(q):
            s_l = lax.dot_general(q, kb[...], NT_DIMS, preferred_element_type=F32)
            s_c = lax.dot_general(q, ckb[...], NT_DIMS, preferred_element_type=F32)
            m = jnp.maximum(jnp.max(s_l, axis=-1, keepdims=True),
                            jnp.max(s_c, axis=-1, keepdims=True))
            e_l = jnp.exp2(s_l - m).astype(BF16)
            e_c = jnp.exp2(s_c - m).astype(BF16)
            oz = (jnp.dot(e_l, with_ones(dv_ref[...]), preferred_element_type=F32)
                  + jnp.dot(e_c, with_ones(cvb[...]), preferred_element_type=F32))
            return oz[:, :HEAD_DIM] / oz[:, HEAD_DIM:]

        o = (softmax_pv(q1b[pl.ds(r0, Q_BLOCK), :])
             - lam * softmax_pv(q2b[pl.ds(r0, Q_BLOCK), :]))
        od_ref[pl.ds(r0, Q_BLOCK), :] = _sub_ln(o, sg_ref[...], layer).astype(BF16)
        return carry

    lax.fori_loop(0, DEC_SEQ // Q_BLOCK, diff_block, 0, unroll=True)


def _attn_latent(qkv, caches, tab, cos, sin, lam_p, subln, layer):
    h8 = N_HEADS
    row0 = N_PROMPT // DEC_SEQ
    blk = lambda off: pl.BlockSpec((DEC_SEQ, HEAD_DIM), lambda b, h: (row0 + b, off + h))
    cblk = pl.BlockSpec((None, None, None, PAST_LEN, HEAD_DIM), lambda b, h: (b, layer, h, 0, 0))
    fixed = pl.BlockSpec((DEC_SEQ, HEAD_DIM), lambda b, h: (0, 0))
    out_spec = pl.BlockSpec((DEC_SEQ, HEAD_DIM), lambda b, h: (b, h))
    merged_shape = jax.ShapeDtypeStruct((N_TOK - N_PROMPT, h8 * HEAD_DIM), BF16)
    return pl.pallas_call(
        functools.partial(_attn_latent_kernel, layer=layer),
        out_shape=(merged_shape, merged_shape),
        grid=(DEC_BATCH, h8),
        in_specs=[blk(0), blk(h8), blk(2 * h8), blk(3 * h8), blk(4 * h8), blk(5 * h8),
                  cblk, cblk, cblk, cblk,
                  pl.BlockSpec((None, 2 * WIN_H, GRID_W, 2 * GRID_W), lambda b, h: (h, 0, 0, 0)),
                  fixed, fixed,
                  pl.BlockSpec((None, 4, DIFF_QK_DIM), lambda b, h: (layer, 0, 0)),
                  pl.BlockSpec((None, 1, HEAD_DIM), lambda b, h: (layer, 0, 0))],
        out_specs=(out_spec, out_spec),
        scratch_shapes=[pltpu.VMEM((DEC_SEQ, HEAD_DIM), BF16),
                        pltpu.VMEM((PAST_LEN, HEAD_DIM), BF16), pltpu.VMEM((PAST_LEN, HEAD_DIM), BF16),
                        pltpu.VMEM((DEC_SEQ, HEAD_DIM), BF16), pltpu.VMEM((DEC_SEQ, HEAD_DIM), BF16)],
        compiler_params=_params(48),
        name="attn_latent",
    )(qkv, qkv, qkv, qkv, qkv, qkv, *caches, tab, cos, sin, lam_p, subln)


def _route(logits, bias):
    ex = jnp.exp(logits - jnp.max(logits, axis=0, keepdims=True))
    probs = ex / jnp.sum(ex, axis=0, keepdims=True)
    sel = probs + bias
    srow = [sel[e:e + 1, :] for e in range(N_EXPERTS)]
    prow = [probs[e:e + 1, :] for e in range(N_EXPERTS)]
    gscore = []
    for g in range(N_GROUPS):
        v = srow[g * EXPERTS_PER_GROUP:(g + 1) * EXPERTS_PER_GROUP]
        best = None
        for a in range(EXPERTS_PER_GROUP):
            for b in range(a + 1, EXPERTS_PER_GROUP):
                pair = v[a] + v[b]
                best = pair if best is None else jnp.maximum(best, pair)
        gscore.append(best)
    gbest = gscore[0]
    gidx = jnp.zeros(gbest.shape, jnp.int32)
    for g in range(1, N_GROUPS):
        better = gscore[g] > gbest
        gidx = jnp.where(better, g, gidx)
        gbest = jnp.where(better, gscore[g], gbest)
    neg = jnp.full(gbest.shape, -jnp.inf, F32)
    picks = []
    taken = None
    for _ in range(2):
        best = neg
        idx = jnp.zeros(gbest.shape, jnp.int32)
        wgt = jnp.zeros(gbest.shape, F32)
        for e in range(N_EXPERTS):
            ok = gidx == (e // EXPERTS_PER_GROUP)
            if taken is not None:
                ok = jnp.logical_and(ok, taken != e)
            cand = jnp.where(ok, srow[e], neg)
            better = cand > best
            idx = jnp.where(better, e, idx)
            wgt = jnp.where(better, prow[e], wgt)
            best = jnp.where(better, cand, best)
        picks.append((idx, wgt))
        taken = idx
    (i0, w0), (i1, w1) = picks
    wsum = w0 + w1
    return i0, i1, w0 / wsum, w1 / wsum


def _split_bf16(v):
    hi = v.astype(BF16)
    lo = (v - hi.astype(F32)).astype(BF16)
    return hi, lo


def _load_resident_weight(w_hbm, wbf, stage, sem, n_chunks):
    ck = wbf.shape[0] // n_chunks
    copies = [pltpu.make_async_copy(w_hbm.at[pl.ds(c * ck, ck), :], stage.at[c % 2], sem.at[c % 2])
              for c in range(n_chunks)]
    copies[0].start()
    for c in range(n_chunks):
        if c + 1 < n_chunks:
            copies[c + 1].start()
        copies[c].wait()
        wbf[c * ck:(c + 1) * ck, :] = stage[c % 2].astype(BF16)


def _out_proj_kernel(pn_ref, pd_ref, sn_ref, sd_ref, w_hbm, *rest, layer, split_x):
    if split_x:
        x_in = _select_stream(rest[0], rest[1], OUTPROJ_TM)
        rest = rest[2:]
    else:
        x_in = rest[0][...]
        rest = rest[1:]
    (mod_ref, g_ref, wr_ref, rb_ref, xo_ref, h_ref, idx_ref, cw_ref, wbf, stage, mbuf, sem) = rest
    i = pl.program_id(0)
    half = D_MODEL // 2

    @pl.when(i == 0)
    def _():
        _load_resident_weight(w_hbm.at[layer], wbf, stage, sem, OUTPROJ_WCHUNKS)

    prompt = i < N_PROMPT // OUTPROJ_TM

    @pl.when(prompt)
    def _():
        mbuf[:, :half] = pn_ref[...]
        mbuf[:, half:] = pd_ref[...]

    @pl.when(jnp.logical_not(prompt))
    def _():
        mbuf[:, :half] = sn_ref[...]
        mbuf[:, half:] = sd_ref[...]

    a = jnp.dot(mbuf[...], wbf[...], preferred_element_type=F32)
    x = x_in + mod_ref[2:3, :] * a
    xo_ref[...] = x
    h = _rms(x, g_ref[...]) * (1.0 + mod_ref[4:5, :]) + mod_ref[3:4, :]
    h_ref[...] = h
    w_hi, w_lo = _split_bf16(wr_ref[...])
    h_hi, h_lo = _split_bf16(h)
    part = lax.dot_general(jnp.concatenate([w_hi, w_lo], axis=0), h_hi, NT_DIMS,
                           preferred_element_type=F32)
    logits = (part[:N_EXPERTS] + part[N_EXPERTS:]
              + lax.dot_general(w_hi, h_lo, NT_DIMS, preferred_element_type=F32))
    i0, i1, w0, w1 = _route(logits, rb_ref[...])
    idx_ref[...] = jnp.concatenate([i0, i1], axis=0)
    cw_ref[...] = jnp.concatenate([w0, w1], axis=0)


def _out_proj(merged, w_out, x, mods, layer, gain, w_router_t, router_bias):
    tm = OUTPROJ_TM
    split_x = isinstance(x, tuple)
    x_args = list(x) if split_x else [x]
    x_specs = _stream_specs(tm) if split_x else [pl.BlockSpec((tm, D_MODEL), lambda i: (i, 0))]
    half = D_MODEL // 2
    npt = N_PROMPT // tm
    row = lambda i: (i, 0)
    fixed = lambda i: (0, 0)
    p_row = lambda i: (jnp.minimum(i, npt - 1), 0)
    s_row = lambda i: (jnp.maximum(i - npt, 0), 0)
    return pl.pallas_call(
        functools.partial(_out_proj_kernel, layer=layer, split_x=split_x),
        out_shape=(jax.ShapeDtypeStruct((N_TOK, D_MODEL), F32),
                   jax.ShapeDtypeStruct((N_TOK, D_MODEL), F32),
                   jax.ShapeDtypeStruct((2, N_TOK), jnp.int32),
                   jax.ShapeDtypeStruct((2, N_TOK), F32)),
        grid=(N_TOK // tm,),
        in_specs=[
            pl.BlockSpec((tm, half), p_row),
            pl.BlockSpec((tm, half), p_row),
            pl.BlockSpec((tm, half), s_row),
            pl.BlockSpec((tm, half), s_row),
            pl.BlockSpec(memory_space=pl.ANY),
        ] + x_specs + [
            pl.BlockSpec((None, None, 6, D_MODEL), lambda i: (layer, _mod_row(i, tm), 0, 0)),
            pl.BlockSpec((1, D_MODEL), fixed),
            pl.BlockSpec((N_EXPERTS, D_MODEL), fixed),
            pl.BlockSpec((N_EXPERTS, 1), fixed),
        ],
        out_specs=(pl.BlockSpec((tm, D_MODEL), row), pl.BlockSpec((tm, D_MODEL), row),
                   pl.BlockSpec((2, tm), lambda i: (0, i)),
                   pl.BlockSpec((2, tm), lambda i: (0, i))),
        scratch_shapes=[pltpu.VMEM((D_MODEL, D_MODEL), BF16),
                        pltpu.VMEM((2, D_MODEL // OUTPROJ_WCHUNKS, D_MODEL), F32),
                        pltpu.VMEM((tm, D_MODEL), BF16),
                        pltpu.SemaphoreType.DMA((2,))],
        compiler_params=_params(48),
        name="out_proj_router",
    )(*merged, w_out, *x_args, mods, gain, w_router_t, router_bias)


def _dispatch_kernel(pos_ref, h_ref, xs_in, xs_out, sem):
    del xs_in
    tm = DISPATCH_TM
    base = pl.program_id(0) * tm

    for r in range(tm):
        for k in range(2):
            p = pos_ref[k * N_TOK + base + r]
            pltpu.make_async_copy(h_ref.at[pl.ds(r, 1), :], xs_out.at[pl.ds(p, 1), :], sem).start()
    for _ in range(2):
        pltpu.make_async_copy(h_ref, xs_out.at[pl.ds(0, tm), :], sem).wait()


def _dispatch(pos, h2, xs_buf):
    tm = DISPATCH_TM
    return pl.pallas_call(
        _dispatch_kernel,
        out_shape=jax.ShapeDtypeStruct((EXPERT_CAP, D_MODEL), F32),
        grid_spec=pltpu.PrefetchScalarGridSpec(
            num_scalar_prefetch=1,
            grid=(N_TOK // tm,),
            in_specs=[pl.BlockSpec((tm, D_MODEL), lambda i, pos_ref: (i, 0)),
                      pl.BlockSpec(memory_space=pl.ANY)],
            out_specs=pl.BlockSpec(memory_space=pl.ANY),
            scratch_shapes=[pltpu.SemaphoreType.DMA(())],
        ),
        input_output_aliases={2: 0},
        compiler_params=_params(32),
        name="expert_dispatch",
    )(pos, h2, xs_buf)


def _experts_kernel(we_ref, wc_ref, wdo_ref, wsl_ref, ti_ref, tdo_ref, tsl_ref,
                    xs_ref, wg_ref, wu_ref, wd_ref, o_ref, wg_bf, wu_bf, wd_bf):
    del we_ref, ti_ref
    s = pl.program_id(0)

    def cast_chunk():
        slot = wsl_ref[s]
        c = wc_ref[s]
        ck = D_MODEL // EXPERT_WCHUNKS
        r0 = pl.multiple_of(c * ck, ck)
        wg_bf[slot, pl.ds(r0, ck), :] = wg_ref[...].astype(BF16)
        wu_bf[slot, pl.ds(r0, ck), :] = wu_ref[...].astype(BF16)
        cf = D_FF // EXPERT_WCHUNKS
        r1 = pl.multiple_of(c * cf, cf)
        wd_bf[slot, pl.ds(r1, cf), :] = wd_ref[...].astype(BF16)

    def run_tile():
        slot = tsl_ref[s]
        xb = xs_ref[...].astype(BF16)
        g = jnp.dot(xb, wg_bf[slot], preferred_element_type=F32)
        u = jnp.dot(xb, wu_bf[slot], preferred_element_type=F32)
        hid = (g * jax.nn.sigmoid(g) * u).astype(BF16)
        o_ref[...] = jnp.dot(hid, wd_bf[slot], preferred_element_type=F32)

    load = wdo_ref[s] == 1
    tile = tdo_ref[s] == 1

    @pl.when(jnp.logical_and(load, tile))
    def _():
        cast_chunk()
        run_tile()

    @pl.when(jnp.logical_and(load, jnp.logical_not(tile)))
    def _():
        cast_chunk()

    @pl.when(jnp.logical_and(tile, jnp.logical_not(load)))
    def _():
        run_tile()

    @pl.when(tdo_ref[s] == 2)
    def _():
        o_ref[...] = jnp.zeros_like(o_ref)


def _experts(plan, xs, w_gate, w_up, w_down, layer):
    tm = EXPERT_TM
    ck = D_MODEL // EXPERT_WCHUNKS
    cf = D_FF // EXPERT_WCHUNKS
    tile = lambda s, we, wc, wdo, wsl, ti, tdo, tsl: (ti[s], 0)
    wblk = lambda s, we, wc, wdo, wsl, ti, tdo, tsl: (layer, we[s], wc[s], 0)
    return pl.pallas_call(
        _experts_kernel,
        out_shape=jax.ShapeDtypeStruct((EXPERT_CAP, D_MODEL), F32),
        grid_spec=pltpu.PrefetchScalarGridSpec(
            num_scalar_prefetch=7,
            grid=(EXPERT_STEPS,),
            in_specs=[
                pl.BlockSpec((tm, D_MODEL), tile),
                pl.BlockSpec((None, None, ck, D_FF), wblk),
                pl.BlockSpec((None, None, ck, D_FF), wblk),
                pl.BlockSpec((None, None, cf, D_MODEL), wblk),
            ],
            out_specs=pl.BlockSpec((tm, D_MODEL), tile),
            scratch_shapes=[pltpu.VMEM((2, D_MODEL, D_FF), BF16),
                            pltpu.VMEM((2, D_MODEL, D_FF), BF16),
                            pltpu.VMEM((2, D_FF, D_MODEL), BF16)],
        ),
        compiler_params=_params(56),
        name="expert_mlp",
    )(*plan, xs, w_gate, w_up, w_down)


def _dispatch_plan(idx):
    e = idx.reshape(-1)
    onehot = jnp.arange(N_EXPERTS, dtype=jnp.int32)[:, None] == e[None, :]
    nb = N_PAIRS // PLAN_BLOCK
    upper = jnp.triu(jnp.ones((PLAN_BLOCK, PLAN_BLOCK), F32))
    within = jnp.einsum('ebj,ji->ebi', onehot.astype(F32).reshape(N_EXPERTS, nb, PLAN_BLOCK), upper,
                        preferred_element_type=F32)
    totals = within[:, :, -1]
    before = jnp.cumsum(totals, axis=1) - totals
    csum = (within + before[:, :, None]).reshape(N_EXPERTS, N_PAIRS)
    counts = (before[:, -1] + totals[:, -1]).astype(jnp.int32)
    nt = (counts + EXPERT_TM - 1) // EXPERT_TM
    tile_end = jnp.cumsum(nt)
    tile_start = tile_end - nt
    first_row = (tile_start * EXPERT_TM).astype(F32)[:, None]
    pos = jnp.sum(jnp.where(onehot, csum - 1.0 + first_row, 0.0), axis=0)

    nw = EXPERT_WCHUNKS
    length = jnp.maximum(nt, nw)
    phase_end = nw + jnp.cumsum(length)
    phase_start = phase_end - length
    s = jnp.arange(EXPERT_STEPS, dtype=jnp.int32)
    started = s[:, None] >= phase_start[None, :]
    ex = jnp.sum(started.astype(jnp.int32), axis=1) - 1
    running = started & (s[:, None] < phase_end[None, :])

    def of_phase(v):
        return jnp.sum(jnp.where(running, v[None, :], 0), axis=1)

    k = s - of_phase(phase_start)
    nt_s = of_phase(nt)
    ts_s = of_phase(tile_start)
    live = s < phase_end[-1:]
    last = N_EXPERTS - 1
    w_do = (live & (k < nw) & (ex < last)).astype(jnp.int32)
    w_e = jnp.minimum(ex + 1, last)
    w_c = jnp.where(ex < last, jnp.minimum(k, nw - 1), nw - 1)
    w_slot = (ex + 1) % 2
    t_do = (live & (k < nt_s)).astype(jnp.int32)
    t_idx = jnp.maximum(ts_s + jnp.minimum(k + 1, nt_s) - 1, 0)
    z = tile_end[-1:] + (s - phase_end[-1:])
    t_do = jnp.where(jnp.logical_not(live) & (z < EXPERT_TILES), 2, t_do)
    t_idx = jnp.where(live, t_idx, jnp.minimum(z, EXPERT_TILES - 1))
    t_slot = jnp.maximum(ex, 0) % 2
    plan = tuple(a.astype(jnp.int32) for a in (w_e, w_c, w_do, w_slot, t_idx, t_do, t_slot))
    return pos.astype(jnp.int32), plan


def _bias_tables(nat_rpb):
    w = GRID_W
    c = jnp.arange(w)[:, None]
    kc = jnp.arange(w)[None, :]
    qstart = jnp.clip(c - WIN_W // 2, 0, w - WIN_W)
    valid = (kc >= qstart) & (kc < qstart + WIN_W)
    n_off = nat_rpb.shape[-1]
    select = ((kc - c + WIN_W - 1)[None] == jnp.arange(n_off)[:, None, None]).astype(F32)
    tab = jnp.einsum('lhdj,jck->lhdck', nat_rpb, select, precision=lax.Precision.HIGHEST)
    tab = jnp.where(valid, tab * LOG2E, MASK_VALUE)
    tab = jnp.pad(tab, [(0, 0), (0, 0), (1, 1), (0, 0), (0, 0)], constant_values=MASK_VALUE)
    return jnp.concatenate([tab[:, :, :-1], tab[:, :, 1:]], axis=-1)


def _rope_tables():
    t = jnp.arange(DEC_SEQ)
    row = (t // GRID_W).astype(F32)
    col = (t % GRID_W).astype(F32)
    nf = DIFF_QK_DIM // 4
    inv = ROPE_BASE ** (-jnp.arange(nf, dtype=F32) / nf)
    ar = row[:, None] * inv[None, :]
    ac = col[:, None] * inv[None, :]
    cos = jnp.concatenate([jnp.cos(ar), jnp.cos(ar), jnp.cos(ac), jnp.cos(ac)], axis=-1)
    sin = jnp.concatenate([-jnp.sin(ar), jnp.sin(ar), -jnp.sin(ac), jnp.sin(ac)], axis=-1)
    return jnp.tile(cos, (1, 2)), jnp.tile(sin, (1, 2))


def kernel(x_prompt, x_sample, cache_nat_k, cache_nat_v, cache_diff_k, cache_diff_v, c, c_ctx,
           w_ada, b_ada, norm1, norm2, norm_final, w_in, w_out, nat_rpb, diff_lambda, diff_subln,
           w_router, router_bias, w_gate, w_up, w_down):
    x = (x_prompt.reshape(N_PROMPT, D_MODEL), x_sample.reshape(DEC_BATCH * DEC_SEQ, D_MODEL))
    cond =jnp.concatenate([c_ctx[None, :], c, jnp.zeros((8 - 1 - DEC_BATCH, D_MODEL), F32)], axis=0)
    mods = _modulation(cond, w_ada, b_ada)
    tabs = _bias_tables(nat_rpb)
    cos, sin = _rope_tables()
    subln = diff_subln.reshape(DEPTH, 1, HEAD_DIM)
    w_router_t = w_router.T
    rbias = router_bias.reshape(N_EXPERTS, 1)
    lat_caches = (cache_nat_k, cache_nat_v, cache_diff_k, cache_diff_v)
    ones = jnp.ones((SECTION,), F32)
    qscale = jnp.concatenate([ones * (HEAD_DIM ** -0.5 * LOG2E), ones, ones,
                              ones * (DIFF_QK_DIM ** -0.5 * LOG2E), ones, ones]).reshape(1, IN_COLS)

    new_caches = tuple(jnp.zeros((BATCH, DEPTH, N_HEADS, SEQ, HEAD_DIM), F32) for _ in range(4))
    xs_buf = jnp.zeros((EXPERT_CAP, D_MODEL), F32)
    ys = pos = cw = None
    for layer in range(DEPTH):
        gain1 = norm1[layer].reshape(1, D_MODEL)
        if layer == 0:
            h = _pre_first(*x, mods, layer, gain1)
        else:
            x, h = _pre_combine(x, ys, pos, cw, mods, layer - 1, gain1, final=False)
        qkv, new_caches = _in_proj(h, w_in, qscale, new_caches, layer)
        pn, pd = _attn_prompt(qkv, diff_lambda, subln, layer)
        sn, sd = _attn_latent(qkv, lat_caches, tabs[layer], cos, sin, diff_lambda, subln, layer)
        x, h2, idx, cwt = _out_proj((pn, pd, sn, sd), w_out, x, mods, layer,
                                    norm2[layer].reshape(1, D_MODEL), w_router_t, rbias)
        pos, plan = _dispatch_plan(idx)
        cw = cwt.T
        xs_buf = _dispatch(pos, h2, xs_buf)
        ys = _experts(plan, xs_buf, w_gate, w_up, w_down, layer)
    yp, ysm = _pre_combine(x, ys, pos, cw, mods, DEPTH - 1, norm_final.reshape(1, D_MODEL), final=True)
    return (yp.reshape(BATCH, SEQ, D_MODEL), ysm.reshape(DEC_BATCH, DEC_SEQ, D_MODEL)) + tuple(new_caches)
```

```python
import functools
import math

import jax
import jax.numpy as jnp
from jax import lax
from jax.experimental import pallas as pl
from jax.experimental.pallas import tpu as pltpu

D_MODEL = 2048
BATCH = 16
SEQ = 256
DEPTH = 4
DEC_BATCH = 4
DEC_SEQ = 1024
PAST_LEN = 512
GRID_W = 64
GRID_H = DEC_SEQ // GRID_W
N_HEADS = 8
HEAD_DIM = 128
DIFF_QK_DIM = 64
WIN_H = 8
WIN_W = 16
N_EXPERTS = 16
N_GROUPS = 4
EXPERTS_PER_GROUP = N_EXPERTS // N_GROUPS
D_FF = 1024
ROPE_BASE = 10000.0
EPS = 1e-6
IN_COLS = 6 * N_HEADS * HEAD_DIM
N_PROMPT = BATCH * SEQ
N_TOK = N_PROMPT + DEC_BATCH * DEC_SEQ
N_PAIRS = 2 * N_TOK

PRE_TM = 256
INPROJ_TM = 2048
INPROJ_TM_PROMPT = 1024
INPROJ_TN = 512
OUTPROJ_TM = 256
OUTPROJ_WCHUNKS = 4
DISPATCH_TM = 256
EXPERT_TM = 256
EXPERT_CAP = N_PAIRS + N_EXPERTS * EXPERT_TM
EXPERT_TILES = EXPERT_CAP // EXPERT_TM
EXPERT_WCHUNKS = 4
EXPERT_STEPS = EXPERT_WCHUNKS * (N_EXPERTS + 1) + EXPERT_TILES
PLAN_BLOCK = 128
MOD_TN = 1024
Q_BLOCK = 256
PROMPT_HEADS = 4
NAT_QROWS = 4
NAT_KROWS = 12
LOG2E = 1.4426950408889634
MASK_VALUE = -1e30

F32 = jnp.float32
BF16 = jnp.bfloat16
MIB = 1024 * 1024
NT_DIMS = (((1,), (1,)), ((), ()))


def _params(vmem_mib):
    return pltpu.CompilerParams(vmem_limit_bytes=vmem_mib * MIB)


def _lam_init(layer):
    return 0.8 - 0.6 * math.exp(-0.3 * layer)


def _rms(x, gain):
    return x * lax.rsqrt(jnp.mean(x * x, axis=-1, keepdims=True) + EPS) * gain


def _mod_row(i, tm):
    first = i * tm
    return jnp.where(first < N_PROMPT, 0, 1 + (first - N_PROMPT) // DEC_SEQ)


def _mod_kernel(cond_ref, w_ref, b_ref, o_ref):
    c = cond_ref[...]
    s = c * jax.nn.sigmoid(c)
    o_ref[...] = jnp.dot(s.astype(BF16), w_ref[...].astype(BF16),
                         preferred_element_type=F32) + b_ref[...]


def _modulation(cond, w_ada, b_ada):
    n = 6 * D_MODEL
    out = pl.pallas_call(
        _mod_kernel,
        out_shape=jax.ShapeDtypeStruct((DEPTH, 8, n), F32),
        grid=(DEPTH, n // MOD_TN),
        in_specs=[
            pl.BlockSpec((8, D_MODEL), lambda l, j: (0, 0)),
            pl.BlockSpec((None, D_MODEL, MOD_TN), lambda l, j: (l, 0, j)),
            pl.BlockSpec((None, 1, MOD_TN), lambda l, j: (l, 0, j)),
        ],
        out_specs=pl.BlockSpec((None, 8, MOD_TN), lambda l, j: (l, 0, j)),
        compiler_params=_params(40),
        name="adaln_modulation",
    )(cond, w_ada, b_ada.reshape(DEPTH, 1, n))
    return out.reshape(DEPTH, 8, 6, D_MODEL)


def _gather_expert_rows(pos_ref, ys_hbm, ybuf, sem, tm):
    i = pl.program_id(0)
    n = pl.num_programs(0)

    def issue_tile(tile, slot):
        for r in range(tm):
            for k in range(2):
                p = pos_ref[k * N_TOK + tile * tm + r]
                pltpu.make_async_copy(ys_hbm.at[pl.ds(p, 1), :],
                                      ybuf.at[slot, k, pl.ds(r, 1), :], sem.at[slot, k]).start()

    @pl.when(i == 0)
    def _():
        issue_tile(0, 0)

    @pl.when(i + 1 < n)
    def _():
        issue_tile(i + 1, (i + 1) % 2)

    slot = i % 2
    for k in range(2):
        pltpu.make_async_copy(ys_hbm.at[pl.ds(0, tm), :], ybuf.at[slot, k], sem.at[slot, k]).wait()
    return slot


def _select_stream(xp_ref, xs_ref, tm):
    prompt = pl.program_id(0) < N_PROMPT // tm
    return jnp.where(prompt, xp_ref[...], xs_ref[...])


def _pre_first_kernel(xp_ref, xs_ref, mod_ref, g_ref, h_ref):
    x = _select_stream(xp_ref, xs_ref, PRE_TM)
    h = _rms(x, g_ref[...]) * (1.0 + mod_ref[1:2, :]) + mod_ref[0:1, :]
    h_ref[...] = h.astype(BF16)


def _pre_mid_kernel(pos_ref, x_ref, ys_hbm, cw_ref, modp_ref, mod_ref, g_ref,
                    xo_ref, h_ref, ybuf, sem):
    slot = _gather_expert_rows(pos_ref, ys_hbm, ybuf, sem, PRE_TM)
    cw = cw_ref[...]
    y = cw[:, 0:1] * ybuf[slot, 0] + cw[:, 1:2] * ybuf[slot, 1]
    x = x_ref[...] + modp_ref[5:6, :] * y
    xo_ref[...] = x
    h = _rms(x, g_ref[...]) * (1.0 + mod_ref[1:2, :]) + mod_ref[0:1, :]
    h_ref[...] = h.astype(BF16)


def _pre_final_kernel(pos_ref, x_ref, ys_hbm, cw_ref, modp_ref, g_ref, yp_ref, ys_ref, ybuf, sem):
    slot = _gather_expert_rows(pos_ref, ys_hbm, ybuf, sem, PRE_TM)
    cw = cw_ref[...]
    y = cw[:, 0:1] * ybuf[slot, 0] + cw[:, 1:2] * ybuf[slot, 1]
    x = x_ref[...] + modp_ref[5:6, :] * y
    out = _rms(x, g_ref[...])
    prompt = pl.program_id(0) < N_PROMPT // PRE_TM

    @pl.when(prompt)
    def _():
        yp_ref[...] = out

    @pl.when(jnp.logical_not(prompt))
    def _():
        ys_ref[...] = out


def _stream_specs(tm):
    npt = N_PROMPT // tm
    return [pl.BlockSpec((tm, D_MODEL), lambda i, *_: (jnp.minimum(i, npt - 1), 0)),
            pl.BlockSpec((tm, D_MODEL), lambda i, *_: (jnp.maximum(i - npt, 0), 0))]


def _pre_first(xp, xs, mods, layer, gain):
    tm = PRE_TM
    return pl.pallas_call(
        _pre_first_kernel,
        out_shape=jax.ShapeDtypeStruct((N_TOK, D_MODEL), BF16),
        grid=(N_TOK // tm,),
        in_specs=_stream_specs(tm) + [
            pl.BlockSpec((None, None, 6, D_MODEL), lambda i: (layer, _mod_row(i, tm), 0, 0)),
            pl.BlockSpec((1, D_MODEL), lambda i: (0, 0)),
        ],
        out_specs=pl.BlockSpec((tm, D_MODEL), lambda i: (i, 0)),
        compiler_params=_params(32),
        name="pre_first",
    )(xp, xs, mods, gain)


def _pre_combine(x, ys, pos, cw, mods, prev_layer, gain, final):
    tm = PRE_TM
    row = lambda i, pos_ref: (i, 0)
    fixed = lambda i, pos_ref: (0, 0)
    in_specs = [
        pl.BlockSpec((tm, D_MODEL), row),
        pl.BlockSpec(memory_space=pl.ANY),
        pl.BlockSpec((tm, 2), row),
        pl.BlockSpec((None, None, 6, D_MODEL),
                     lambda i, pos_ref: (prev_layer, _mod_row(i, tm), 0, 0)),
    ]
    args = [x, ys, cw, mods]
    if final:
        kern = _pre_final_kernel
        out_shape = (jax.ShapeDtypeStruct((N_PROMPT, D_MODEL), F32),
                     jax.ShapeDtypeStruct((N_TOK - N_PROMPT, D_MODEL), F32))
        out_specs = tuple(_stream_specs(tm))
    else:
        kern = _pre_mid_kernel
        in_specs.append(pl.BlockSpec((None, None, 6, D_MODEL),
                                     lambda i, pos_ref: (prev_layer + 1, _mod_row(i, tm), 0, 0)))
        args.append(mods)
        out_shape = (jax.ShapeDtypeStruct((N_TOK, D_MODEL), F32),
                     jax.ShapeDtypeStruct((N_TOK, D_MODEL), BF16))
        out_specs = (pl.BlockSpec((tm, D_MODEL), row), pl.BlockSpec((tm, D_MODEL), row))
    in_specs.append(pl.BlockSpec((1, D_MODEL), fixed))
    args.append(gain)
    return pl.pallas_call(
        kern,
        out_shape=out_shape,
        grid_spec=pltpu.PrefetchScalarGridSpec(
            num_scalar_prefetch=1,
            grid=(N_TOK // tm,),
            in_specs=in_specs,
            out_specs=out_specs,
            scratch_shapes=[pltpu.VMEM((2, 2, tm, D_MODEL), F32), pltpu.SemaphoreType.DMA((2, 2))],
        ),
        compiler_params=_params(40),
        name="pre_final" if final else "pre_combine",
    )(pos, *args)


SECTION = N_HEADS * HEAD_DIM
INPROJ_SECTION_BLOCKS = SECTION // INPROJ_TN
CACHE_SECTIONS = (1, 2, 4, 5)


def _in_proj_kernel(h_ref, w_ref, qs_ref, *rest, tm, with_caches):
    acc = jnp.dot(h_ref[...], w_ref[...].astype(BF16), preferred_element_type=F32)
    o_ref = rest[4] if with_caches else rest[0]
    o_ref[...] = (acc * qs_ref[...]).astype(BF16)
    if not with_caches:
        return
    j = pl.program_id(1)
    for cache_ref, section in zip(rest[5:], CACHE_SECTIONS):
        first = section * INPROJ_SECTION_BLOCKS

        @pl.when(jnp.logical_and(j >= first, j < first + INPROJ_SECTION_BLOCKS))
        def _(cache_ref=cache_ref):
            for b in range(tm // SEQ):
                for hh in range(INPROJ_TN // HEAD_DIM):
                    cache_ref[b, hh] = acc[b * SEQ:(b + 1) * SEQ, hh * HEAD_DIM:(hh + 1) * HEAD_DIM]


def _in_proj_prompt(h, w_in, qscale, caches, layer):
    tm, tn = INPROJ_TM_PROMPT, INPROJ_TN
    nb = INPROJ_SECTION_BLOCKS

    def cache_spec(section):
        first = section * nb
        return pl.BlockSpec((tm // SEQ, None, tn // HEAD_DIM, SEQ, HEAD_DIM),
                            lambda i, j: (i, layer, jnp.clip(j - first, 0, nb - 1), 0, 0))

    cache_shape = jax.ShapeDtypeStruct((BATCH, DEPTH, N_HEADS, SEQ, HEAD_DIM), F32)
    out = pl.pallas_call(
        functools.partial(_in_proj_kernel, tm=tm, with_caches=True),
        out_shape=(jax.ShapeDtypeStruct((N_PROMPT, IN_COLS), BF16),) + (cache_shape,) * 4,
        grid=(N_PROMPT // tm, IN_COLS // tn),
        in_specs=[
            pl.BlockSpec((tm, D_MODEL), lambda i, j: (i, 0)),
            pl.BlockSpec((None, D_MODEL, tn), lambda i, j: (layer, 0, j)),
            pl.BlockSpec((1, tn), lambda i, j: (0, j)),
        ] + [pl.BlockSpec(memory_space=pl.ANY)] * 4,
        out_specs=(pl.BlockSpec((tm, tn), lambda i, j: (i, j)),)
        + tuple(cache_spec(section) for section in CACHE_SECTIONS),
        input_output_aliases={3: 1, 4: 2, 5: 3, 6: 4},
        compiler_params=_params(48),
        name="in_proj_prompt",
    )(h, w_in, qscale, *caches)
    return out[0], out[1:]


def _in_proj_latent(h, w_in, qscale, layer):
    tm, tn = INPROJ_TM, INPROJ_TN
    first_tile = N_PROMPT // tm
    return pl.pallas_call(
        functools.partial(_in_proj_kernel, tm=tm, with_caches=False),
        out_shape=jax.ShapeDtypeStruct((N_TOK - N_PROMPT, IN_COLS), BF16),
        grid=((N_TOK - N_PROMPT) // tm, IN_COLS // tn),
        in_specs=[
            pl.BlockSpec((tm, D_MODEL), lambda i, j: (first_tile + i, 0)),
            pl.BlockSpec((None, D_MODEL, tn), lambda i, j: (layer, 0, j)),
            pl.BlockSpec((1, tn), lambda i, j: (0, j)),
        ],
        out_specs=pl.BlockSpec((tm, tn), lambda i, j: (i, j)),
        compiler_params=_params(48),
        name="in_proj_latent",
    )(h, w_in, qscale)


def _lambda(lam_ref, layer):
    lp = lam_ref[...]
    a = jnp.sum(lp[0:1, :] * lp[1:2, :], axis=-1, keepdims=True)
    b = jnp.sum(lp[2:3, :] * lp[3:4, :], axis=-1, keepdims=True)
    return jnp.exp(a) - jnp.exp(b) + _lam_init(layer)


def _split_q(q):
    lane = lax.broadcasted_iota(jnp.int32, q.shape, 1)
    q1 = jnp.where(lane < DIFF_QK_DIM, q, 0.0).astype(BF16)
    q2 = jnp.where(lane >= DIFF_QK_DIM, q, 0.0).astype(BF16)
    return q1, q2


def _sub_ln(o, gain, layer):
    return _rms(o, gain) * (1.0 - _lam_init(layer))


def _attn_prompt_kernel(nq_ref, nk_ref, nv_ref, dq_ref, dk_ref, dv_ref, lam_ref, sg_ref,
                        on_ref, od_ref, *, layer):
    ones = jnp.ones((SEQ, HEAD_DIM), BF16)
    lam = _lambda(lam_ref, layer)

    def softmax_pv(q, k, v):
        s = lax.dot_general(q, k, NT_DIMS, preferred_element_type=F32)
        e = jnp.exp2(s - jnp.max(s, axis=-1, keepdims=True)).astype(BF16)
        oz = jnp.dot(e, jnp.concatenate([v, ones], axis=1), preferred_element_type=F32)
        return oz[:, :HEAD_DIM] / oz[:, HEAD_DIM:]

    for hh in range(PROMPT_HEADS):
        cols = slice(hh * HEAD_DIM, (hh + 1) * HEAD_DIM)
        on_ref[:, cols] = softmax_pv(nq_ref[:, cols], nk_ref[:, cols], nv_ref[:, cols]).astype(BF16)
        q1, q2 = _split_q(dq_ref[:, cols])
        kb = dk_ref[:, cols]
        vb = dv_ref[:, cols]
        o = softmax_pv(q1, kb, vb) - lam * softmax_pv(q2, kb, vb)
        od_ref[:, cols] = _sub_ln(o, sg_ref[...], layer).astype(BF16)


def _attn_prompt(qkv, lam_p, subln, layer):
    hp = PROMPT_HEADS
    h8 = N_HEADS // hp
    blk = lambda off: pl.BlockSpec((SEQ, hp * HEAD_DIM), lambda b, h: (b, off + h))
    in_specs = [blk(0), blk(h8), blk(2 * h8), blk(3 * h8), blk(4 * h8), blk(5 * h8),
                pl.BlockSpec((None, 4, DIFF_QK_DIM), lambda b, h: (layer, 0, 0)),
                pl.BlockSpec((None, 1, HEAD_DIM), lambda b, h: (layer, 0, 0))]
    args = [qkv] * 6 + [lam_p, subln]
    merged_shape = jax.ShapeDtypeStruct((N_PROMPT, N_HEADS * HEAD_DIM), BF16)
    merged_spec = pl.BlockSpec((SEQ, hp * HEAD_DIM), lambda b, h: (b, h))
    return pl.pallas_call(
        functools.partial(_attn_prompt_kernel, layer=layer),
        out_shape=(merged_shape, merged_shape),
        grid=(BATCH, h8),
        in_specs=in_specs,
        out_specs=(merged_spec, merged_spec),
        compiler_params=_params(32),
        name="attn_prompt",
    )(*args)


def _attn_latent_kernel(nq_ref, nk_ref, nv_ref, dq_ref, dk_ref, dv_ref,
                        cnk_ref, cnv_ref, cdk_ref, cdv_ref, tab_ref, cos_ref, sin_ref,
                        lam_ref, sg_ref, on_ref, od_ref,
                        kb, ckb, cvb, q1b, q2b, *, layer):
    ckb[...] = cnk_ref[...].astype(BF16)
    cvb[...] = cnv_ref[...].astype(BF16)
    masked = jnp.full((GRID_W, 2 * GRID_W), MASK_VALUE, F32)

    def with_ones(v):
        return jnp.concatenate([v, jnp.ones(v.shape, BF16)], axis=1)

    left = lax.broadcasted_iota(jnp.int32, (GRID_W, 2 * GRID_W), 1) < GRID_W

    def bias_tile(r, kr):
        lo = min(max(r - WIN_H // 2, 0), GRID_H - WIN_H)
        ok0 = lo <= kr < lo + WIN_H
        ok1 = lo <= kr + 1 < lo + WIN_H
        if not (ok0 or ok1):
            return masked
        tile = tab_ref[kr - r + WIN_H]
        if ok0 and ok1:
            return tile
        return jnp.where(left if ok0 else jnp.logical_not(left), tile, masked)

    for blk in range(GRID_H // NAT_QROWS):
        r0 = blk * NAT_QROWS
        k0 = min(max(r0 - WIN_H // 2, 0), GRID_H - NAT_KROWS)
        rows = slice(r0 * GRID_W, (r0 + NAT_QROWS) * GRID_W)
        keys = slice(k0 * GRID_W, (k0 + NAT_KROWS) * GRID_W)
        q = nq_ref[rows, :]
        bias = jnp.concatenate(
            [jnp.concatenate([bias_tile(r0 + a, k0 + 2 * i) for i in range(NAT_KROWS // 2)], axis=1)
             for a in range(NAT_QROWS)], axis=0)
        s_nb = lax.dot_general(q, nk_ref[keys, :], NT_DIMS, preferred_element_type=F32) + bias
        s_cx = lax.dot_general(q, ckb[...], NT_DIMS, preferred_element_type=F32)
        m = jnp.maximum(jnp.max(s_nb, axis=-1, keepdims=True),
                        jnp.max(s_cx, axis=-1, keepdims=True))
        e_nb = jnp.exp2(s_nb - m).astype(BF16)
        e_cx = jnp.exp2(s_cx - m).astype(BF16)
        oz = (jnp.dot(e_nb, with_ones(nv_ref[keys, :]), preferred_element_type=F32)
              + jnp.dot(e_cx, with_ones(cvb[...]), preferred_element_type=F32))
        on_ref[rows, :] = (oz[:, :HEAD_DIM] / oz[:, HEAD_DIM:]).astype(BF16)

    lane = lax.broadcasted_iota(jnp.int32, (DEC_SEQ, HEAD_DIM), 1)
    first_half = (lane & 16) == 0

    def rope(x):
        rot = jnp.where(first_half, pltpu.roll(x, HEAD_DIM - 16, 1), pltpu.roll(x, 16, 1))
        return x * cos_ref[...] + rot * sin_ref[...]

    q1, q2 = _split_q(rope(dq_ref[...].astype(F32)))
    q1b[...] = q1
    q2b[...] = q2
    kb[...] = rope(dk_ref[...].astype(F32)).astype(BF16)
    ckb[...] = cdk_ref[...].astype(BF16)
    cvb[...] = cdv_ref[...].astype(BF16)
    lam = _lambda(lam_ref, layer)

    def diff_block(j, carry):
        r0 = pl.multiple_of(j * Q_BLOCK, Q_BLOCK)

        def softmax_pv(q):
            s_l = lax.dot_general(q, kb[...], NT_DIMS, preferred_element_type=F32)
            s_c = lax.dot_general(q, ckb[...], NT_DIMS, preferred_element_type=F32)
            m = jnp.maximum(jnp.max(s_l, axis=-1, keepdims=True),
                            jnp.max(s_c, axis=-1, keepdims=True))
            e_l = jnp.exp2(s_l - m).astype(BF16)
            e_c = jnp.exp2(s_c - m).astype(BF16)
            oz = (jnp.dot(e_l, with_ones(dv_ref[...]), preferred_element_type=F32)
                  + jnp.dot(e_c, with_ones(cvb[...]), preferred_element_type=F32))
            return oz[:, :HEAD_DIM] / oz[:, HEAD_DIM:]

        o = (softmax_pv(q1b[pl.ds(r0, Q_BLOCK), :])
             - lam * softmax_pv(q2b[pl.ds(r0, Q_BLOCK), :]))
        od_ref[pl.ds(r0, Q_BLOCK), :] = _sub_ln(o, sg_ref[...], layer).astype(BF16)
        return carry

    lax.fori_loop(0, DEC_SEQ // Q_BLOCK, diff_block, 0, unroll=True)


def _attn_latent(qkv, caches, tab, cos, sin, lam_p, subln, layer):
    h8 = N_HEADS
    blk = lambda off: pl.BlockSpec((DEC_SEQ, HEAD_DIM), lambda b, h: (b, off + h))
    cblk = pl.BlockSpec((None, None, None, PAST_LEN, HEAD_DIM), lambda b, h: (b, layer, h, 0, 0))
    fixed = pl.BlockSpec((DEC_SEQ, HEAD_DIM), lambda b, h: (0, 0))
    out_spec = pl.BlockSpec((DEC_SEQ, HEAD_DIM), lambda b, h: (b, h))
    merged_shape = jax.ShapeDtypeStruct((N_TOK - N_PROMPT, h8 * HEAD_DIM), BF16)
    return pl.pallas_call(
        functools.partial(_attn_latent_kernel, layer=layer),
        out_shape=(merged_shape, merged_shape),
        grid=(DEC_BATCH, h8),
        in_specs=[blk(0), blk(h8), blk(2 * h8), blk(3 * h8), blk(4 * h8), blk(5 * h8),
                  cblk, cblk, cblk, cblk,
                  pl.BlockSpec((None, 2 * WIN_H, GRID_W, 2 * GRID_W), lambda b, h: (h, 0, 0, 0)),
                  fixed, fixed,
                  pl.BlockSpec((None, 4, DIFF_QK_DIM), lambda b, h: (layer, 0, 0)),
                  pl.BlockSpec((None, 1, HEAD_DIM), lambda b, h: (layer, 0, 0))],
        out_specs=(out_spec, out_spec),
        scratch_shapes=[pltpu.VMEM((DEC_SEQ, HEAD_DIM), BF16),
                        pltpu.VMEM((PAST_LEN, HEAD_DIM), BF16), pltpu.VMEM((PAST_LEN, HEAD_DIM), BF16),
                        pltpu.VMEM((DEC_SEQ, HEAD_DIM), BF16), pltpu.VMEM((DEC_SEQ, HEAD_DIM), BF16)],
        compiler_params=_params(48),
        name="attn_latent",
    )(qkv, qkv, qkv, qkv, qkv, qkv, *caches, tab, cos, sin, lam_p, subln)


def _route(logits, bias):
    ex = jnp.exp(logits - jnp.max(logits, axis=0, keepdims=True))
    probs = ex / jnp.sum(ex, axis=0, keepdims=True)
    sel = probs + bias
    srow = [sel[e:e + 1, :] for e in range(N_EXPERTS)]
    prow = [probs[e:e + 1, :] for e in range(N_EXPERTS)]
    gscore = []
    for g in range(N_GROUPS):
        v = srow[g * EXPERTS_PER_GROUP:(g + 1) * EXPERTS_PER_GROUP]
        best = None
        for a in range(EXPERTS_PER_GROUP):
            for b in range(a + 1, EXPERTS_PER_GROUP):
                pair = v[a] + v[b]
                best = pair if best is None else jnp.maximum(best, pair)
        gscore.append(best)
    gbest = gscore[0]
    gidx = jnp.zeros(gbest.shape, jnp.int32)
    for g in range(1, N_GROUPS):
        better = gscore[g] > gbest
        gidx = jnp.where(better, g, gidx)
        gbest = jnp.where(better, gscore[g], gbest)
    neg = jnp.full(gbest.shape, -jnp.inf, F32)
    picks = []
    taken = None
    for _ in range(2):
        best = neg
        idx = jnp.zeros(gbest.shape, jnp.int32)
        wgt = jnp.zeros(gbest.shape, F32)
        for e in range(N_EXPERTS):
            ok = gidx == (e // EXPERTS_PER_GROUP)
            if taken is not None:
                ok = jnp.logical_and(ok, taken != e)
            cand = jnp.where(ok, srow[e], neg)
            better = cand > best
            idx = jnp.where(better, e, idx)
            wgt = jnp.where(better, prow[e], wgt)
            best = jnp.where(better, cand, best)
        picks.append((idx, wgt))
        taken = idx
    (i0, w0), (i1, w1) = picks
    wsum = w0 + w1
    return i0, i1, w0 / wsum, w1 / wsum


def _split_bf16(v):
    hi = v.astype(BF16)
    lo = (v - hi.astype(F32)).astype(BF16)
    return hi, lo


def _load_resident_weight(w_hbm, wbf, stage, sem, n_chunks):
    ck = wbf.shape[0] // n_chunks
    copies = [pltpu.make_async_copy(w_hbm.at[pl.ds(c * ck, ck), :], stage.at[c % 2], sem.at[c % 2])
              for c in range(n_chunks)]
    copies[0].start()
    for c in range(n_chunks):
        if c + 1 < n_chunks:
            copies[c + 1].start()
        copies[c].wait()
        wbf[c * ck:(c + 1) * ck, :] = stage[c % 2].astype(BF16)


def _out_proj_kernel(pn_ref, pd_ref, sn_ref, sd_ref, w_hbm, *rest, layer, split_x):
    if split_x:
        x_in = _select_stream(rest[0], rest[1], OUTPROJ_TM)
        rest = rest[2:]
    else:
        x_in = rest[0][...]
        rest = rest[1:]
    (mod_ref, g_ref, wr_ref, rb_ref, xo_ref, h_ref, idx_ref, cw_ref, wbf, stage, mbuf, sem) = rest
    i = pl.program_id(0)
    half = D_MODEL // 2

    @pl.when(i == 0)
    def _():
        _load_resident_weight(w_hbm.at[layer], wbf, stage, sem, OUTPROJ_WCHUNKS)

    prompt = i < N_PROMPT // OUTPROJ_TM

    @pl.when(prompt)
    def _():
        mbuf[:, :half] = pn_ref[...]
        mbuf[:, half:] = pd_ref[...]

    @pl.when(jnp.logical_not(prompt))
    def _():
        mbuf[:, :half] = sn_ref[...]
        mbuf[:, half:] = sd_ref[...]

    a = jnp.dot(mbuf[...], wbf[...], preferred_element_type=F32)
    x = x_in + mod_ref[2:3, :] * a
    xo_ref[...] = x
    h = _rms(x, g_ref[...]) * (1.0 + mod_ref[4:5, :]) + mod_ref[3:4, :]
    h_ref[...] = h
    w_hi, w_lo = _split_bf16(wr_ref[...])
    h_hi, h_lo = _split_bf16(h)
    part = lax.dot_general(jnp.concatenate([w_hi, w_lo], axis=0), h_hi, NT_DIMS,
                           preferred_element_type=F32)
    logits = (part[:N_EXPERTS] + part[N_EXPERTS:]
              + lax.dot_general(w_hi, h_lo, NT_DIMS, preferred_element_type=F32))
    i0, i1, w0, w1 = _route(logits, rb_ref[...])
    idx_ref[...] = jnp.concatenate([i0, i1], axis=0)
    cw_ref[...] = jnp.concatenate([w0, w1], axis=0)


def _out_proj(merged, w_out, x, mods, layer, gain, w_router_t, router_bias):
    tm = OUTPROJ_TM
    split_x = isinstance(x, tuple)
    x_args = list(x) if split_x else [x]
    x_specs = _stream_specs(tm) if split_x else [pl.BlockSpec((tm, D_MODEL), lambda i: (i, 0))]
    half = D_MODEL // 2
    npt = N_PROMPT // tm
    row = lambda i: (i, 0)
    fixed = lambda i: (0, 0)
    p_row = lambda i: (jnp.minimum(i, npt - 1), 0)
    s_row = lambda i: (jnp.maximum(i - npt, 0), 0)
    return pl.pallas_call(
        functools.partial(_out_proj_kernel, layer=layer, split_x=split_x),
        out_shape=(jax.ShapeDtypeStruct((N_TOK, D_MODEL), F32),
                   jax.ShapeDtypeStruct((N_TOK, D_MODEL), F32),
                   jax.ShapeDtypeStruct((2, N_TOK), jnp.int32),
                   jax.ShapeDtypeStruct((2, N_TOK), F32)),
        grid=(N_TOK // tm,),
        in_specs=[
            pl.BlockSpec((tm, half), p_row),
            pl.BlockSpec((tm, half), p_row),
            pl.BlockSpec((tm, half), s_row),
            pl.BlockSpec((tm, half), s_row),
            pl.BlockSpec(memory_space=pl.ANY),
        ] + x_specs + [
            pl.BlockSpec((None, None, 6, D_MODEL), lambda i: (layer, _mod_row(i, tm), 0, 0)),
            pl.BlockSpec((1, D_MODEL), fixed),
            pl.BlockSpec((N_EXPERTS, D_MODEL), fixed),
            pl.BlockSpec((N_EXPERTS, 1), fixed),
        ],
        out_specs=(pl.BlockSpec((tm, D_MODEL), row), pl.BlockSpec((tm, D_MODEL), row),
                   pl.BlockSpec((2, tm), lambda i: (0, i)),
                   pl.BlockSpec((2, tm), lambda i: (0, i))),
        scratch_shapes=[pltpu.VMEM((D_MODEL, D_MODEL), BF16),
                        pltpu.VMEM((2, D_MODEL // OUTPROJ_WCHUNKS, D_MODEL), F32),
                        pltpu.VMEM((tm, D_MODEL), BF16),
                        pltpu.SemaphoreType.DMA((2,))],
        compiler_params=_params(48),
        name="out_proj_router",
    )(*merged, w_out, *x_args, mods, gain, w_router_t, router_bias)


def _dispatch_kernel(pos_ref, h_ref, xs_in, xs_out, sem):
    del xs_in
    tm = DISPATCH_TM
    base = pl.program_id(0) * tm

    for r in range(tm):
        for k in range(2):
            p = pos_ref[k * N_TOK + base + r]
            pltpu.make_async_copy(h_ref.at[pl.ds(r, 1), :], xs_out.at[pl.ds(p, 1), :], sem).start()
    for _ in range(2):
        pltpu.make_async_copy(h_ref, xs_out.at[pl.ds(0, tm), :], sem).wait()


def _dispatch(pos, h2, xs_buf):
    tm = DISPATCH_TM
    return pl.pallas_call(
        _dispatch_kernel,
        out_shape=jax.ShapeDtypeStruct((EXPERT_CAP, D_MODEL), F32),
        grid_spec=pltpu.PrefetchScalarGridSpec(
            num_scalar_prefetch=1,
            grid=(N_TOK // tm,),
            in_specs=[pl.BlockSpec((tm, D_MODEL), lambda i, pos_ref: (i, 0)),
                      pl.BlockSpec(memory_space=pl.ANY)],
            out_specs=pl.BlockSpec(memory_space=pl.ANY),
            scratch_shapes=[pltpu.SemaphoreType.DMA(())],
        ),
        input_output_aliases={2: 0},
        compiler_params=_params(32),
        name="expert_dispatch",
    )(pos, h2, xs_buf)


def _experts_kernel(we_ref, wc_ref, wdo_ref, wsl_ref, ti_ref, tdo_ref, tsl_ref,
                    xs_ref, wg_ref, wu_ref, wd_ref, o_ref, wg_bf, wu_bf, wd_bf):
    del we_ref, ti_ref
    s = pl.program_id(0)

    def cast_chunk():
        slot = wsl_ref[s]
        c = wc_ref[s]
        ck = D_MODEL // EXPERT_WCHUNKS
        r0 = pl.multiple_of(c * ck, ck)
        wg_bf[slot, pl.ds(r0, ck), :] = wg_ref[...].astype(BF16)
        wu_bf[slot, pl.ds(r0, ck), :] = wu_ref[...].astype(BF16)
        cf = D_FF // EXPERT_WCHUNKS
        r1 = pl.multiple_of(c * cf, cf)
        wd_bf[slot, pl.ds(r1, cf), :] = wd_ref[...].astype(BF16)

    def run_tile():
        slot = tsl_ref[s]
        xb = xs_ref[...].astype(BF16)
        g = jnp.dot(xb, wg_bf[slot], preferred_element_type=F32)
        u = jnp.dot(xb, wu_bf[slot], preferred_element_type=F32)
        hid = (g * jax.nn.sigmoid(g) * u).astype(BF16)
        o_ref[...] = jnp.dot(hid, wd_bf[slot], preferred_element_type=F32)

    load = wdo_ref[s] == 1
    tile = tdo_ref[s] == 1

    @pl.when(jnp.logical_and(load, tile))
    def _():
        cast_chunk()
        run_tile()

    @pl.when(jnp.logical_and(load, jnp.logical_not(tile)))
    def _():
        cast_chunk()

    @pl.when(jnp.logical_and(tile, jnp.logical_not(load)))
    def _():
        run_tile()

    @pl.when(tdo_ref[s] == 2)
    def _():
        o_ref[...] = jnp.zeros_like(o_ref)


def _experts(plan, xs, w_gate, w_up, w_down, layer):
    tm = EXPERT_TM
    ck = D_MODEL // EXPERT_WCHUNKS
    cf = D_FF // EXPERT_WCHUNKS
    tile = lambda s, we, wc, wdo, wsl, ti, tdo, tsl: (ti[s], 0)
    wblk = lambda s, we, wc, wdo, wsl, ti, tdo, tsl: (layer, we[s], wc[s], 0)
    return pl.pallas_call(
        _experts_kernel,
        out_shape=jax.ShapeDtypeStruct((EXPERT_CAP, D_MODEL), F32),
        grid_spec=pltpu.PrefetchScalarGridSpec(
            num_scalar_prefetch=7,
            grid=(EXPERT_STEPS,),
            in_specs=[
                pl.BlockSpec((tm, D_MODEL), tile),
                pl.BlockSpec((None, None, ck, D_FF), wblk),
                pl.BlockSpec((None, None, ck, D_FF), wblk),
                pl.BlockSpec((None, None, cf, D_MODEL), wblk),
            ],
            out_specs=pl.BlockSpec((tm, D_MODEL), tile),
            scratch_shapes=[pltpu.VMEM((2, D_MODEL, D_FF), BF16),
                            pltpu.VMEM((2, D_MODEL, D_FF), BF16),
                            pltpu.VMEM((2, D_FF, D_MODEL), BF16)],
        ),
        compiler_params=_params(56),
        name="expert_mlp",
    )(*plan, xs, w_gate, w_up, w_down)


def _dispatch_plan(idx):
    e = idx.reshape(-1)
    onehot = jnp.arange(N_EXPERTS, dtype=jnp.int32)[:, None] == e[None, :]
    nb = N_PAIRS // PLAN_BLOCK
    upper = jnp.triu(jnp.ones((PLAN_BLOCK, PLAN_BLOCK), F32))
    within = jnp.einsum('ebj,ji->ebi', onehot.astype(F32).reshape(N_EXPERTS, nb, PLAN_BLOCK), upper,
                        preferred_element_type=F32)
    totals = within[:, :, -1]
    before = jnp.cumsum(totals, axis=1) - totals
    csum = (within + before[:, :, None]).reshape(N_EXPERTS, N_PAIRS)
    counts = (before[:, -1] + totals[:, -1]).astype(jnp.int32)
    nt = (counts + EXPERT_TM - 1) // EXPERT_TM
    tile_end = jnp.cumsum(nt)
    tile_start = tile_end - nt
    first_row = (tile_start * EXPERT_TM).astype(F32)[:, None]
    pos = jnp.sum(jnp.where(onehot, csum - 1.0 + first_row, 0.0), axis=0)

    nw = EXPERT_WCHUNKS
    length = jnp.maximum(nt, nw)
    phase_end = nw + jnp.cumsum(length)
    phase_start = phase_end - length
    s = jnp.arange(EXPERT_STEPS, dtype=jnp.int32)
    started = s[:, None] >= phase_start[None, :]
    ex = jnp.sum(started.astype(jnp.int32), axis=1) - 1
    running = started & (s[:, None] < phase_end[None, :])

    def of_phase(v):
        return jnp.sum(jnp.where(running, v[None, :], 0), axis=1)

    k = s - of_phase(phase_start)
    nt_s = of_phase(nt)
    ts_s = of_phase(tile_start)
    live = s < phase_end[-1:]
    last = N_EXPERTS - 1
    w_do = (live & (k < nw) & (ex < last)).astype(jnp.int32)
    w_e = jnp.minimum(ex + 1, last)
    w_c = jnp.where(ex < last, jnp.minimum(k, nw - 1), nw - 1)
    w_slot = (ex + 1) % 2
    t_do = (live & (k < nt_s)).astype(jnp.int32)
    t_idx = jnp.maximum(ts_s + jnp.minimum(k + 1, nt_s) - 1, 0)
    z = tile_end[-1:] + (s - phase_end[-1:])
    t_do = jnp.where(jnp.logical_not(live) & (z < EXPERT_TILES), 2, t_do)
    t_idx = jnp.where(live, t_idx, jnp.minimum(z, EXPERT_TILES - 1))
    t_slot = jnp.maximum(ex, 0) % 2
    plan = tuple(a.astype(jnp.int32) for a in (w_e, w_c, w_do, w_slot, t_idx, t_do, t_slot))
    return pos.astype(jnp.int32), plan


def _bias_tables(nat_rpb):
    w = GRID_W
    c = jnp.arange(w)[:, None]
    kc = jnp.arange(w)[None, :]
    qstart = jnp.clip(c - WIN_W // 2, 0, w - WIN_W)
    valid = (kc >= qstart) & (kc < qstart + WIN_W)
    n_off = nat_rpb.shape[-1]
    select = ((kc - c + WIN_W - 1)[None] == jnp.arange(n_off)[:, None, None]).astype(F32)
    tab = jnp.einsum('lhdj,jck->lhdck', nat_rpb, select, precision=lax.Precision.HIGHEST)
    tab = jnp.where(valid, tab * LOG2E, MASK_VALUE)
    tab = jnp.pad(tab, [(0, 0), (0, 0), (1, 1), (0, 0), (0, 0)], constant_values=MASK_VALUE)
    return jnp.concatenate([tab[:, :, :-1], tab[:, :, 1:]], axis=-1)


def _rope_tables():
    t = jnp.arange(DEC_SEQ)
    row = (t // GRID_W).astype(F32)
    col = (t % GRID_W).astype(F32)
    nf = DIFF_QK_DIM // 4
    inv = ROPE_BASE ** (-jnp.arange(nf, dtype=F32) / nf)
    ar = row[:, None] * inv[None, :]
    ac = col[:, None] * inv[None, :]
    cos = jnp.concatenate([jnp.cos(ar), jnp.cos(ar), jnp.cos(ac), jnp.cos(ac)], axis=-1)
    sin = jnp.concatenate([-jnp.sin(ar), jnp.sin(ar), -jnp.sin(ac), jnp.sin(ac)], axis=-1)
    return jnp.tile(cos, (1, 2)), jnp.tile(sin, (1, 2))


def kernel(x_prompt, x_sample, cache_nat_k, cache_nat_v, cache_diff_k, cache_diff_v, c, c_ctx,
           w_ada, b_ada, norm1, norm2, norm_final, w_in, w_out, nat_rpb, diff_lambda, diff_subln,
           w_router, router_bias, w_gate, w_up, w_down):
    x = (x_prompt.reshape(N_PROMPT, D_MODEL), x_sample.reshape(DEC_BATCH * DEC_SEQ, D_MODEL))
    cond =jnp.concatenate([c_ctx[None, :], c, jnp.zeros((8 - 1 - DEC_BATCH, D_MODEL), F32)], axis=0)
    mods = _modulation(cond, w_ada, b_ada)
    tabs = _bias_tables(nat_rpb)
    cos, sin = _rope_tables()
    subln = diff_subln.reshape(DEPTH, 1, HEAD_DIM)
    w_router_t = w_router.T
    rbias = router_bias.reshape(N_EXPERTS, 1)
    lat_caches = (cache_nat_k, cache_nat_v, cache_diff_k, cache_diff_v)
    ones = jnp.ones((SECTION,), F32)
    qscale = jnp.concatenate([ones * (HEAD_DIM ** -0.5 * LOG2E), ones, ones,
                              ones * (DIFF_QK_DIM ** -0.5 * LOG2E), ones, ones]).reshape(1, IN_COLS)

    new_caches = tuple(jnp.zeros((BATCH, DEPTH, N_HEADS, SEQ, HEAD_DIM), F32) for _ in range(4))
    xs_buf = jnp.zeros((EXPERT_CAP, D_MODEL), F32)
    ys = pos = cw = None
    for layer in range(DEPTH):
        gain1 = norm1[layer].reshape(1, D_MODEL)
        if layer == 0:
            h = _pre_first(*x, mods, layer, gain1)
        else:
            x, h = _pre_combine(x, ys, pos, cw, mods, layer - 1, gain1, final=False)
        qkv_p, new_caches = _in_proj_prompt(h, w_in, qscale, new_caches, layer)
        qkv_s = _in_proj_latent(h, w_in, qscale, layer)
        pn, pd = _attn_prompt(qkv_p, diff_lambda, subln, layer)
        sn, sd = _attn_latent(qkv_s, lat_caches, tabs[layer], cos, sin, diff_lambda, subln, layer)
        x, h2, idx, cwt = _out_proj((pn, pd, sn, sd), w_out, x, mods, layer,
                                    norm2[layer].reshape(1, D_MODEL), w_router_t, rbias)
        pos, plan = _dispatch_plan(idx)
        cw = cwt.T
        xs_buf = _dispatch(pos, h2, xs_buf)
        ys = _experts(plan, xs_buf, w_gate, w_up, w_down, layer)
    yp, ysm = _pre_combine(x, ys, pos, cw, mods, DEPTH - 1, norm_final.reshape(1, D_MODEL), final=True)
    return (yp.reshape(BATCH, SEQ, D_MODEL), ysm.reshape(DEC_BATCH, DEC_SEQ, D_MODEL)) + tuple(new_caches)
```

```python
import functools
import math

import jax
import jax.numpy as jnp
from jax import lax
from jax.experimental import pallas as pl
from jax.experimental.pallas import tpu as pltpu

D_MODEL = 2048
BATCH = 16
SEQ = 256
DEPTH = 4
DEC_BATCH = 4
DEC_SEQ = 1024
PAST_LEN = 512
GRID_W = 64
GRID_H = DEC_SEQ // GRID_W
N_HEADS = 8
HEAD_DIM = 128
DIFF_QK_DIM = 64
WIN_H = 8
WIN_W = 16
N_EXPERTS = 16
N_GROUPS = 4
EXPERTS_PER_GROUP = N_EXPERTS // N_GROUPS
D_FF = 1024
ROPE_BASE = 10000.0
EPS = 1e-6
IN_COLS = 6 * N_HEADS * HEAD_DIM
N_PROMPT = BATCH * SEQ
N_TOK = N_PROMPT + DEC_BATCH * DEC_SEQ
N_PAIRS = 2 * N_TOK

PRE_TM = 256
INPROJ_TM = 2048
INPROJ_TN = 512
OUTPROJ_TM = 256
OUTPROJ_WCHUNKS = 4
DISPATCH_TM = 256
EXPERT_TM = 256
EXPERT_CAP = N_PAIRS + N_EXPERTS * EXPERT_TM
EXPERT_TILES = EXPERT_CAP // EXPERT_TM
EXPERT_WCHUNKS = 4
EXPERT_STEPS = EXPERT_WCHUNKS * (N_EXPERTS + 1) + EXPERT_TILES
PLAN_BLOCK = 128
MOD_TN = 1024
Q_BLOCK = 256
PROMPT_HEADS = 4
NAT_QROWS = 4
NAT_KROWS = 12
LOG2E = 1.4426950408889634
MASK_VALUE = -1e30

F32 = jnp.float32
BF16 = jnp.bfloat16
MIB = 1024 * 1024
NT_DIMS = (((1,), (1,)), ((), ()))


def _params(vmem_mib):
    return pltpu.CompilerParams(vmem_limit_bytes=vmem_mib * MIB)


def _lam_init(layer):
    return 0.8 - 0.6 * math.exp(-0.3 * layer)


def _rms(x, gain):
    return x * lax.rsqrt(jnp.mean(x * x, axis=-1, keepdims=True) + EPS) * gain


def _mod_row(i, tm):
    first = i * tm
    return jnp.where(first < N_PROMPT, 0, 1 + (first - N_PROMPT) // DEC_SEQ)


def _mod_kernel(cond_ref, w_ref, b_ref, o_ref):
    c = cond_ref[...]
    s = c * jax.nn.sigmoid(c)
    o_ref[...] = jnp.dot(s.astype(BF16), w_ref[...].astype(BF16),
                         preferred_element_type=F32) + b_ref[...]


def _modulation(cond, w_ada, b_ada):
    n = 6 * D_MODEL
    out = pl.pallas_call(
        _mod_kernel,
        out_shape=jax.ShapeDtypeStruct((DEPTH, 8, n), F32),
        grid=(DEPTH, n // MOD_TN),
        in_specs=[
            pl.BlockSpec((8, D_MODEL), lambda l, j: (0, 0)),
            pl.BlockSpec((None, D_MODEL, MOD_TN), lambda l, j: (l, 0, j)),
            pl.BlockSpec((None, 1, MOD_TN), lambda l, j: (l, 0, j)),
        ],
        out_specs=pl.BlockSpec((None, 8, MOD_TN), lambda l, j: (l, 0, j)),
        compiler_params=_params(40),
        name="adaln_modulation",
    )(cond, w_ada, b_ada.reshape(DEPTH, 1, n))
    return out.reshape(DEPTH, 8, 6, D_MODEL)


def _gather_expert_rows(pos_ref, ys_hbm, ybuf, sem, tm):
    i = pl.program_id(0)
    n = pl.num_programs(0)

    def issue_tile(tile, slot):
        for r in range(tm):
            for k in range(2):
                p = pos_ref[k * N_TOK + tile * tm + r]
                pltpu.make_async_copy(ys_hbm.at[pl.ds(p, 1), :],
                                      ybuf.at[slot, k, pl.ds(r, 1), :], sem.at[slot, k]).start(priority=k)

    @pl.when(i == 0)
    def _():
        issue_tile(0, 0)

    @pl.when(i + 1 < n)
    def _():
        issue_tile(i + 1, (i + 1) % 2)

    slot = i % 2
    for k in range(2):
        pltpu.make_async_copy(ys_hbm.at[pl.ds(0, tm), :], ybuf.at[slot, k], sem.at[slot, k]).wait()
    return slot


def _select_stream(xp_ref, xs_ref, tm):
    prompt = pl.program_id(0) < N_PROMPT // tm
    return jnp.where(prompt, xp_ref[...], xs_ref[...])


def _pre_first_kernel(xp_ref, xs_ref, mod_ref, g_ref, h_ref):
    x = _select_stream(xp_ref, xs_ref, PRE_TM)
    h = _rms(x, g_ref[...]) * (1.0 + mod_ref[1:2, :]) + mod_ref[0:1, :]
    h_ref[...] = h.astype(BF16)


def _pre_mid_kernel(pos_ref, x_ref, ys_hbm, cw_ref, modp_ref, mod_ref, g_ref,
                    xo_ref, h_ref, ybuf, sem):
    slot = _gather_expert_rows(pos_ref, ys_hbm, ybuf, sem, PRE_TM)
    cw = cw_ref[...]
    y = cw[:, 0:1] * ybuf[slot, 0] + cw[:, 1:2] * ybuf[slot, 1]
    x = x_ref[...] + modp_ref[5:6, :] * y
    xo_ref[...] = x
    h = _rms(x, g_ref[...]) * (1.0 + mod_ref[1:2, :]) + mod_ref[0:1, :]
    h_ref[...] = h.astype(BF16)


def _pre_final_kernel(pos_ref, x_ref, ys_hbm, cw_ref, modp_ref, g_ref, yp_ref, ys_ref, ybuf, sem):
    slot = _gather_expert_rows(pos_ref, ys_hbm, ybuf, sem, PRE_TM)
    cw = cw_ref[...]
    y = cw[:, 0:1] * ybuf[slot, 0] + cw[:, 1:2] * ybuf[slot, 1]
    x = x_ref[...] + modp_ref[5:6, :] * y
    out = _rms(x, g_ref[...])
    prompt = pl.program_id(0) < N_PROMPT // PRE_TM

    @pl.when(prompt)
    def _():
        yp_ref[...] = out

    @pl.when(jnp.logical_not(prompt))
    def _():
        ys_ref[...] = out


def _stream_specs(tm):
    npt = N_PROMPT // tm
    return [pl.BlockSpec((tm, D_MODEL), lambda i, *_: (jnp.minimum(i, npt - 1), 0)),
            pl.BlockSpec((tm, D_MODEL), lambda i, *_: (jnp.maximum(i - npt, 0), 0))]


def _pre_first(xp, xs, mods, layer, gain):
    tm = PRE_TM
    return pl.pallas_call(
        _pre_first_kernel,
        out_shape=jax.ShapeDtypeStruct((N_TOK, D_MODEL), BF16),
        grid=(N_TOK // tm,),
        in_specs=_stream_specs(tm) + [
            pl.BlockSpec((None, None, 6, D_MODEL), lambda i: (layer, _mod_row(i, tm), 0, 0)),
            pl.BlockSpec((1, D_MODEL), lambda i: (0, 0)),
        ],
        out_specs=pl.BlockSpec((tm, D_MODEL), lambda i: (i, 0)),
        compiler_params=_params(32),
        name="pre_first",
    )(xp, xs, mods, gain)


def _pre_combine(x, ys, pos, cw, mods, prev_layer, gain, final):
    tm = PRE_TM
    row = lambda i, pos_ref: (i, 0)
    fixed = lambda i, pos_ref: (0, 0)
    in_specs = [
        pl.BlockSpec((tm, D_MODEL), row),
        pl.BlockSpec(memory_space=pl.ANY),
        pl.BlockSpec((tm, 2), row),
        pl.BlockSpec((None, None, 6, D_MODEL),
                     lambda i, pos_ref: (prev_layer, _mod_row(i, tm), 0, 0)),
    ]
    args = [x, ys, cw, mods]
    if final:
        kern = _pre_final_kernel
        out_shape = (jax.ShapeDtypeStruct((N_PROMPT, D_MODEL), F32),
                     jax.ShapeDtypeStruct((N_TOK - N_PROMPT, D_MODEL), F32))
        out_specs = tuple(_stream_specs(tm))
    else:
        kern = _pre_mid_kernel
        in_specs.append(pl.BlockSpec((None, None, 6, D_MODEL),
                                     lambda i, pos_ref: (prev_layer + 1, _mod_row(i, tm), 0, 0)))
        args.append(mods)
        out_shape = (jax.ShapeDtypeStruct((N_TOK, D_MODEL), F32),
                     jax.ShapeDtypeStruct((N_TOK, D_MODEL), BF16))
        out_specs = (pl.BlockSpec((tm, D_MODEL), row), pl.BlockSpec((tm, D_MODEL), row))
    in_specs.append(pl.BlockSpec((1, D_MODEL), fixed))
    args.append(gain)
    return pl.pallas_call(
        kern,
        out_shape=out_shape,
        grid_spec=pltpu.PrefetchScalarGridSpec(
            num_scalar_prefetch=1,
            grid=(N_TOK // tm,),
            in_specs=in_specs,
            out_specs=out_specs,
            scratch_shapes=[pltpu.VMEM((2, 2, tm, D_MODEL), F32), pltpu.SemaphoreType.DMA((2, 2))],
        ),
        compiler_params=_params(40),
        name="pre_final" if final else "pre_combine",
    )(pos, *args)


def _in_proj_kernel(h_ref, w_ref, o_ref):
    o_ref[...] = jnp.dot(h_ref[...], w_ref[...].astype(BF16), preferred_element_type=F32)


def _in_proj(h, w_in, layer):
    tm, tn = INPROJ_TM, INPROJ_TN
    return pl.pallas_call(
        _in_proj_kernel,
        out_shape=jax.ShapeDtypeStruct((N_TOK, IN_COLS), F32),
        grid=(N_TOK // tm, IN_COLS // tn),
        in_specs=[
            pl.BlockSpec((tm, D_MODEL), lambda i, j: (i, 0)),
            pl.BlockSpec((None, D_MODEL, tn), lambda i, j: (layer, 0, j)),
        ],
        out_specs=pl.BlockSpec((tm, tn), lambda i, j: (i, j)),
        compiler_params=_params(48),
        name="in_proj",
    )(h, w_in)


def _lambda(lam_ref, layer):
    lp = lam_ref[...]
    a = jnp.sum(lp[0:1, :] * lp[1:2, :], axis=-1, keepdims=True)
    b = jnp.sum(lp[2:3, :] * lp[3:4, :], axis=-1, keepdims=True)
    return jnp.exp(a) - jnp.exp(b) + _lam_init(layer)


def _split_q(q):
    lane = lax.broadcasted_iota(jnp.int32, q.shape, 1)
    q1 = jnp.where(lane < DIFF_QK_DIM, q, 0.0).astype(BF16)
    q2 = jnp.where(lane >= DIFF_QK_DIM, q, 0.0).astype(BF16)
    return q1, q2


def _sub_ln(o, gain, layer):
    return _rms(o, gain) * (1.0 - _lam_init(layer))


def _attn_prompt_kernel(nq_ref, nk_ref, nv_ref, dq_ref, dk_ref, dv_ref, lam_ref, sg_ref,
                        c0_in, c1_in, c2_in, c3_in,
                        on_ref, od_ref, onk_ref, onv_ref, odk_ref, odv_ref, *, layer):
    del c0_in, c1_in, c2_in, c3_in
    ones = jnp.ones((SEQ, HEAD_DIM), BF16)
    lam = _lambda(lam_ref, layer)

    def softmax_pv(q, k, v):
        s = lax.dot_general(q, k, NT_DIMS, preferred_element_type=F32)
        e = jnp.exp2(s - jnp.max(s, axis=-1, keepdims=True)).astype(BF16)
        oz = jnp.dot(e, jnp.concatenate([v, ones], axis=1), preferred_element_type=F32)
        return oz[:, :HEAD_DIM] / oz[:, HEAD_DIM:]

    for hh in range(PROMPT_HEADS):
        cols = slice(hh * HEAD_DIM, (hh + 1) * HEAD_DIM)
        nk = nk_ref[:, cols]
        nv = nv_ref[:, cols]
        dk = dk_ref[:, cols]
        dv = dv_ref[:, cols]
        onk_ref[hh] = nk
        onv_ref[hh] = nv
        odk_ref[hh] = dk
        odv_ref[hh] = dv
        q = (nq_ref[:, cols] * (HEAD_DIM ** -0.5 * LOG2E)).astype(BF16)
        on_ref[:, cols] = softmax_pv(q, nk.astype(BF16), nv.astype(BF16)).astype(BF16)
        q1, q2 = _split_q(dq_ref[:, cols] * (DIFF_QK_DIM ** -0.5 * LOG2E))
        kb = dk.astype(BF16)
        vb = dv.astype(BF16)
        o = softmax_pv(q1, kb, vb) - lam * softmax_pv(q2, kb, vb)
        od_ref[:, cols] = _sub_ln(o, sg_ref[...], layer).astype(BF16)


def _attn_prompt(qkv, lam_p, subln, caches, layer):
    hp = PROMPT_HEADS
    h8 = N_HEADS // hp
    blk = lambda off: pl.BlockSpec((SEQ, hp * HEAD_DIM), lambda b, h: (b, off + h))
    in_specs = [blk(0), blk(h8), blk(2 * h8), blk(3 * h8), blk(4 * h8), blk(5 * h8),
                pl.BlockSpec((None, 4, DIFF_QK_DIM), lambda b, h: (layer, 0, 0)),
                pl.BlockSpec((None, 1, HEAD_DIM), lambda b, h: (layer, 0, 0))]
    args = [qkv] * 6 + [lam_p, subln]
    aliases = {}
    for n, cache in enumerate(caches):
        aliases[len(args)] = 2 + n
        in_specs.append(pl.BlockSpec(memory_space=pl.ANY))
        args.append(cache)
    cache_shape = jax.ShapeDtypeStruct((BATCH, DEPTH, N_HEADS, SEQ, HEAD_DIM), F32)
    cache_spec = pl.BlockSpec((None, None, hp, SEQ, HEAD_DIM), lambda b, h: (b, layer, h, 0, 0))
    merged_shape = jax.ShapeDtypeStruct((N_PROMPT, N_HEADS * HEAD_DIM), BF16)
    merged_spec = pl.BlockSpec((SEQ, hp * HEAD_DIM), lambda b, h: (b, h))
    out = pl.pallas_call(
        functools.partial(_attn_prompt_kernel, layer=layer),
        out_shape=(merged_shape, merged_shape,
                   cache_shape, cache_shape, cache_shape, cache_shape),
        grid=(BATCH, h8),
        in_specs=in_specs,
        out_specs=(merged_spec, merged_spec,
                   cache_spec, cache_spec, cache_spec, cache_spec),
        input_output_aliases=aliases,
        compiler_params=_params(32),
        name="attn_prompt",
    )(*args)
    return out[0], out[1], out[2:]


def _attn_latent_kernel(nq_ref, nk_ref, nv_ref, dq_ref, dk_ref, dv_ref,
                        cnk_ref, cnv_ref, cdk_ref, cdv_ref, tab_ref, cos_ref, sin_ref,
                        lam_ref, sg_ref, on_ref, od_ref,
                        kb, vb, ckb, cvb, q1b, q2b, *, layer):
    kb[...] = nk_ref[...].astype(BF16)
    vb[...] = nv_ref[...].astype(BF16)
    ckb[...] = cnk_ref[...].astype(BF16)
    cvb[...] = cnv_ref[...].astype(BF16)
    masked = jnp.full((GRID_W, 2 * GRID_W), MASK_VALUE, F32)

    def with_ones(v):
        return jnp.concatenate([v, jnp.ones(v.shape, BF16)], axis=1)

    left = lax.broadcasted_iota(jnp.int32, (GRID_W, 2 * GRID_W), 1) < GRID_W

    def bias_tile(r, kr):
        lo = min(max(r - WIN_H // 2, 0), GRID_H - WIN_H)
        ok0 = lo <= kr < lo + WIN_H
        ok1 = lo <= kr + 1 < lo + WIN_H
        if not (ok0 or ok1):
            return masked
        tile = tab_ref[kr - r + WIN_H]
        if ok0 and ok1:
            return tile
        return jnp.where(left if ok0 else jnp.logical_not(left), tile, masked)

    for blk in range(GRID_H // NAT_QROWS):
        r0 = blk * NAT_QROWS
        k0 = min(max(r0 - WIN_H // 2, 0), GRID_H - NAT_KROWS)
        rows = slice(r0 * GRID_W, (r0 + NAT_QROWS) * GRID_W)
        keys = slice(k0 * GRID_W, (k0 + NAT_KROWS) * GRID_W)
        q = (nq_ref[rows, :] * (HEAD_DIM ** -0.5 * LOG2E)).astype(BF16)
        bias = jnp.concatenate(
            [jnp.concatenate([bias_tile(r0 + a, k0 + 2 * i) for i in range(NAT_KROWS // 2)], axis=1)
             for a in range(NAT_QROWS)], axis=0)
        s_nb = lax.dot_general(q, kb[keys, :], NT_DIMS, preferred_element_type=F32) + bias
        s_cx = lax.dot_general(q, ckb[...], NT_DIMS, preferred_element_type=F32)
        m = jnp.maximum(jnp.max(s_nb, axis=-1, keepdims=True),
                        jnp.max(s_cx, axis=-1, keepdims=True))
        e_nb = jnp.exp2(s_nb - m).astype(BF16)
        e_cx = jnp.exp2(s_cx - m).astype(BF16)
        oz = (jnp.dot(e_nb, with_ones(vb[keys, :]), preferred_element_type=F32)
              + jnp.dot(e_cx, with_ones(cvb[...]), preferred_element_type=F32))
        on_ref[rows, :] = (oz[:, :HEAD_DIM] / oz[:, HEAD_DIM:]).astype(BF16)

    lane = lax.broadcasted_iota(jnp.int32, (DEC_SEQ, HEAD_DIM), 1)
    first_half = (lane & 16) == 0

    def rope(x):
        rot = jnp.where(first_half, pltpu.roll(x, HEAD_DIM - 16, 1), pltpu.roll(x, 16, 1))
        return x * cos_ref[...] + rot * sin_ref[...]

    q1, q2 = _split_q(rope(dq_ref[...]) * (DIFF_QK_DIM ** -0.5 * LOG2E))
    q1b[...] = q1
    q2b[...] = q2
    kb[...] = rope(dk_ref[...]).astype(BF16)
    vb[...] = dv_ref[...].astype(BF16)
    ckb[...] = cdk_ref[...].astype(BF16)
    cvb[...] = cdv_ref[...].astype(BF16)
    lam = _lambda(lam_ref, layer)

    def diff_block(j, carry):
        r0 = pl.multiple_of(j * Q_BLOCK, Q_BLOCK)

        def softmax_pv(q):
            s_l = lax.dot_general(q, kb[...], NT_DIMS, preferred_element_type=F32)
            s_c = lax.dot_general(q, ckb[...], NT_DIMS, preferred_element_type=F32)
            m = jnp.maximum(jnp.max(s_l, axis=-1, keepdims=True),
                            jnp.max(s_c, axis=-1, keepdims=True))
            e_l = jnp.exp2(s_l - m).astype(BF16)
            e_c = jnp.exp2(s_c - m).astype(BF16)
            oz = (jnp.dot(e_l, with_ones(vb[...]), preferred_element_type=F32)
                  + jnp.dot(e_c, with_ones(cvb[...]), preferred_element_type=F32))
            return oz[:, :HEAD_DIM] / oz[:, HEAD_DIM:]

        o = (softmax_pv(q1b[pl.ds(r0, Q_BLOCK), :])
             - lam * softmax_pv(q2b[pl.ds(r0, Q_BLOCK), :]))
        od_ref[pl.ds(r0, Q_BLOCK), :] = _sub_ln(o, sg_ref[...], layer).astype(BF16)
        return carry

    lax.fori_loop(0, DEC_SEQ // Q_BLOCK, diff_block, 0, unroll=True)


def _attn_latent(qkv, caches, tab, cos, sin, lam_p, subln, layer):
    h8 = N_HEADS
    row0 = N_PROMPT // DEC_SEQ
    blk = lambda off: pl.BlockSpec((DEC_SEQ, HEAD_DIM), lambda b, h: (row0 + b, off + h))
    cblk = pl.BlockSpec((None, None, None, PAST_LEN, HEAD_DIM), lambda b, h: (b, layer, h, 0, 0))
    fixed = pl.BlockSpec((DEC_SEQ, HEAD_DIM), lambda b, h: (0, 0))
    out_spec = pl.BlockSpec((DEC_SEQ, HEAD_DIM), lambda b, h: (b, h))
    merged_shape = jax.ShapeDtypeStruct((N_TOK - N_PROMPT, h8 * HEAD_DIM), BF16)
    return pl.pallas_call(
        functools.partial(_attn_latent_kernel, layer=layer),
        out_shape=(merged_shape, merged_shape),
        grid=(DEC_BATCH, h8),
        in_specs=[blk(0), blk(h8), blk(2 * h8), blk(3 * h8), blk(4 * h8), blk(5 * h8),
                  cblk, cblk, cblk, cblk,
                  pl.BlockSpec((None, 2 * WIN_H, GRID_W, 2 * GRID_W), lambda b, h: (h, 0, 0, 0)),
                  fixed, fixed,
                  pl.BlockSpec((None, 4, DIFF_QK_DIM), lambda b, h: (layer, 0, 0)),
                  pl.BlockSpec((None, 1, HEAD_DIM), lambda b, h: (layer, 0, 0))],
        out_specs=(out_spec, out_spec),
        scratch_shapes=[pltpu.VMEM((DEC_SEQ, HEAD_DIM), BF16), pltpu.VMEM((DEC_SEQ, HEAD_DIM), BF16),
                        pltpu.VMEM((PAST_LEN, HEAD_DIM), BF16), pltpu.VMEM((PAST_LEN, HEAD_DIM), BF16),
                        pltpu.VMEM((DEC_SEQ, HEAD_DIM), BF16), pltpu.VMEM((DEC_SEQ, HEAD_DIM), BF16)],
        compiler_params=_params(48),
        name="attn_latent",
    )(qkv, qkv, qkv, qkv, qkv, qkv, *caches, tab, cos, sin, lam_p, subln)


def _route(logits, bias):
    ex = jnp.exp(logits - jnp.max(logits, axis=0, keepdims=True))
    probs = ex / jnp.sum(ex, axis=0, keepdims=True)
    sel = probs + bias
    srow = [sel[e:e + 1, :] for e in range(N_EXPERTS)]
    prow = [probs[e:e + 1, :] for e in range(N_EXPERTS)]
    gscore = []
    for g in range(N_GROUPS):
        v = srow[g * EXPERTS_PER_GROUP:(g + 1) * EXPERTS_PER_GROUP]
        best = None
        for a in range(EXPERTS_PER_GROUP):
            for b in range(a + 1, EXPERTS_PER_GROUP):
                pair = v[a] + v[b]
                best = pair if best is None else jnp.maximum(best, pair)
        gscore.append(best)
    gbest = gscore[0]
    gidx = jnp.zeros(gbest.shape, jnp.int32)
    for g in range(1, N_GROUPS):
        better = gscore[g] > gbest
        gidx = jnp.where(better, g, gidx)
        gbest = jnp.where(better, gscore[g], gbest)
    neg = jnp.full(gbest.shape, -jnp.inf, F32)
    picks = []
    taken = None
    for _ in range(2):
        best = neg
        idx = jnp.zeros(gbest.shape, jnp.int32)
        wgt = jnp.zeros(gbest.shape, F32)
        for e in range(N_EXPERTS):
            ok = gidx == (e // EXPERTS_PER_GROUP)
            if taken is not None:
                ok = jnp.logical_and(ok, taken != e)
            cand = jnp.where(ok, srow[e], neg)
            better = cand > best
            idx = jnp.where(better, e, idx)
            wgt = jnp.where(better, prow[e], wgt)
            best = jnp.where(better, cand, best)
        picks.append((idx, wgt))
        taken = idx
    (i0, w0), (i1, w1) = picks
    wsum = w0 + w1
    return i0, i1, w0 / wsum, w1 / wsum


def _split_bf16(v):
    hi = v.astype(BF16)
    lo = (v - hi.astype(F32)).astype(BF16)
    return hi, lo


def _load_resident_weight(w_hbm, wbf, stage, sem, n_chunks):
    ck = wbf.shape[0] // n_chunks
    copies = [pltpu.make_async_copy(w_hbm.at[pl.ds(c * ck, ck), :], stage.at[c % 2], sem.at[c % 2])
              for c in range(n_chunks)]
    copies[0].start()
    for c in range(n_chunks):
        if c + 1 < n_chunks:
            copies[c + 1].start()
        copies[c].wait()
        wbf[c * ck:(c + 1) * ck, :] = stage[c % 2].astype(BF16)


def _out_proj_kernel(pn_ref, pd_ref, sn_ref, sd_ref, w_hbm, *rest, layer, split_x):
    if split_x:
        x_in = _select_stream(rest[0], rest[1], OUTPROJ_TM)
        rest = rest[2:]
    else:
        x_in = rest[0][...]
        rest = rest[1:]
    (mod_ref, g_ref, wr_ref, rb_ref, xo_ref, h_ref, idx_ref, cw_ref, wbf, stage, mbuf, sem) = rest
    i = pl.program_id(0)
    half = D_MODEL // 2

    @pl.when(i == 0)
    def _():
        _load_resident_weight(w_hbm.at[layer], wbf, stage, sem, OUTPROJ_WCHUNKS)

    prompt = i < N_PROMPT // OUTPROJ_TM

    @pl.when(prompt)
    def _():
        mbuf[:, :half] = pn_ref[...]
        mbuf[:, half:] = pd_ref[...]

    @pl.when(jnp.logical_not(prompt))
    def _():
        mbuf[:, :half] = sn_ref[...]
        mbuf[:, half:] = sd_ref[...]

    a = jnp.dot(mbuf[...], wbf[...], preferred_element_type=F32)
    x = x_in + mod_ref[2:3, :] * a
    xo_ref[...] = x
    h = _rms(x, g_ref[...]) * (1.0 + mod_ref[4:5, :]) + mod_ref[3:4, :]
    h_ref[...] = h
    w_hi, w_lo = _split_bf16(wr_ref[...])
    h_hi, h_lo = _split_bf16(h)
    part = lax.dot_general(jnp.concatenate([w_hi, w_lo], axis=0), h_hi, NT_DIMS,
                           preferred_element_type=F32)
    logits = (part[:N_EXPERTS] + part[N_EXPERTS:]
              + lax.dot_general(w_hi, h_lo, NT_DIMS, preferred_element_type=F32))
    i0, i1, w0, w1 = _route(logits, rb_ref[...])
    idx_ref[...] = jnp.concatenate([i0, i1], axis=0)
    cw_ref[...] = jnp.concatenate([w0, w1], axis=0)


def _out_proj(merged, w_out, x, mods, layer, gain, w_router_t, router_bias):
    tm = OUTPROJ_TM
    split_x = isinstance(x, tuple)
    x_args = list(x) if split_x else [x]
    x_specs = _stream_specs(tm) if split_x else [pl.BlockSpec((tm, D_MODEL), lambda i: (i, 0))]
    half = D_MODEL // 2
    npt = N_PROMPT // tm
    row = lambda i: (i, 0)
    fixed = lambda i: (0, 0)
    p_row = lambda i: (jnp.minimum(i, npt - 1), 0)
    s_row = lambda i: (jnp.maximum(i - npt, 0), 0)
    return pl.pallas_call(
        functools.partial(_out_proj_kernel, layer=layer, split_x=split_x),
        out_shape=(jax.ShapeDtypeStruct((N_TOK, D_MODEL), F32),
                   jax.ShapeDtypeStruct((N_TOK, D_MODEL), F32),
                   jax.ShapeDtypeStruct((2, N_TOK), jnp.int32),
                   jax.ShapeDtypeStruct((2, N_TOK), F32)),
        grid=(N_TOK // tm,),
        in_specs=[
            pl.BlockSpec((tm, half), p_row),
            pl.BlockSpec((tm, half), p_row),
            pl.BlockSpec((tm, half), s_row),
            pl.BlockSpec((tm, half), s_row),
            pl.BlockSpec(memory_space=pl.ANY),
        ] + x_specs + [
            pl.BlockSpec((None, None, 6, D_MODEL), lambda i: (layer, _mod_row(i, tm), 0, 0)),
            pl.BlockSpec((1, D_MODEL), fixed),
            pl.BlockSpec((N_EXPERTS, D_MODEL), fixed),
            pl.BlockSpec((N_EXPERTS, 1), fixed),
        ],
        out_specs=(pl.BlockSpec((tm, D_MODEL), row), pl.BlockSpec((tm, D_MODEL), row),
                   pl.BlockSpec((2, tm), lambda i: (0, i)),
                   pl.BlockSpec((2, tm), lambda i: (0, i))),
        scratch_shapes=[pltpu.VMEM((D_MODEL, D_MODEL), BF16),
                        pltpu.VMEM((2, D_MODEL // OUTPROJ_WCHUNKS, D_MODEL), F32),
                        pltpu.VMEM((tm, D_MODEL), BF16),
                        pltpu.SemaphoreType.DMA((2,))],
        compiler_params=_params(48),
        name="out_proj_router",
    )(*merged, w_out, *x_args, mods, gain, w_router_t, router_bias)


def _dispatch_kernel(pos_ref, h_ref, xs_in, xs_out, sem):
    del xs_in
    tm = DISPATCH_TM
    base = pl.program_id(0) * tm

    for r in range(tm):
        for k in range(2):
            p = pos_ref[k * N_TOK + base + r]
            pltpu.make_async_copy(h_ref.at[pl.ds(r, 1), :], xs_out.at[pl.ds(p, 1), :],
                                  sem).start(priority=k)
    for _ in range(2):
        pltpu.make_async_copy(h_ref, xs_out.at[pl.ds(0, tm), :], sem).wait()


def _dispatch(pos, h2, xs_buf):
    tm = DISPATCH_TM
    return pl.pallas_call(
        _dispatch_kernel,
        out_shape=jax.ShapeDtypeStruct((EXPERT_CAP, D_MODEL), F32),
        grid_spec=pltpu.PrefetchScalarGridSpec(
            num_scalar_prefetch=1,
            grid=(N_TOK // tm,),
            in_specs=[pl.BlockSpec((tm, D_MODEL), lambda i, pos_ref: (i, 0)),
                      pl.BlockSpec(memory_space=pl.ANY)],
            out_specs=pl.BlockSpec(memory_space=pl.ANY),
            scratch_shapes=[pltpu.SemaphoreType.DMA(())],
        ),
        input_output_aliases={2: 0},
        compiler_params=_params(32),
        name="expert_dispatch",
    )(pos, h2, xs_buf)


def _experts_kernel(we_ref, wc_ref, wdo_ref, wsl_ref, ti_ref, tdo_ref, tsl_ref,
                    xs_ref, wg_ref, wu_ref, wd_ref, o_ref, wg_bf, wu_bf, wd_bf):
    del we_ref, ti_ref
    s = pl.program_id(0)

    def cast_chunk():
        slot = wsl_ref[s]
        c = wc_ref[s]
        ck = D_MODEL // EXPERT_WCHUNKS
        r0 = pl.multiple_of(c * ck, ck)
        wg_bf[slot, pl.ds(r0, ck), :] = wg_ref[...].astype(BF16)
        wu_bf[slot, pl.ds(r0, ck), :] = wu_ref[...].astype(BF16)
        cf = D_FF // EXPERT_WCHUNKS
        r1 = pl.multiple_of(c * cf, cf)
        wd_bf[slot, pl.ds(r1, cf), :] = wd_ref[...].astype(BF16)

    def run_tile():
        slot = tsl_ref[s]
        xb = xs_ref[...].astype(BF16)
        g = jnp.dot(xb, wg_bf[slot], preferred_element_type=F32)
        u = jnp.dot(xb, wu_bf[slot], preferred_element_type=F32)
        hid = (g * jax.nn.sigmoid(g) * u).astype(BF16)
        o_ref[...] = jnp.dot(hid, wd_bf[slot], preferred_element_type=F32)

    load = wdo_ref[s] == 1
    tile = tdo_ref[s] == 1

    @pl.when(jnp.logical_and(load, tile))
    def _():
        cast_chunk()
        run_tile()

    @pl.when(jnp.logical_and(load, jnp.logical_not(tile)))
    def _():
        cast_chunk()

    @pl.when(jnp.logical_and(tile, jnp.logical_not(load)))
    def _():
        run_tile()

    @pl.when(tdo_ref[s] == 2)
    def _():
        o_ref[...] = jnp.zeros_like(o_ref)


def _experts(plan, xs, w_gate, w_up, w_down, layer):
    tm = EXPERT_TM
    ck = D_MODEL // EXPERT_WCHUNKS
    cf = D_FF // EXPERT_WCHUNKS
    tile = lambda s, we, wc, wdo, wsl, ti, tdo, tsl: (ti[s], 0)
    wblk = lambda s, we, wc, wdo, wsl, ti, tdo, tsl: (layer, we[s], wc[s], 0)
    return pl.pallas_call(
        _experts_kernel,
        out_shape=jax.ShapeDtypeStruct((EXPERT_CAP, D_MODEL), F32),
        grid_spec=pltpu.PrefetchScalarGridSpec(
            num_scalar_prefetch=7,
            grid=(EXPERT_STEPS,),
            in_specs=[
                pl.BlockSpec((tm, D_MODEL), tile),
                pl.BlockSpec((None, None, ck, D_FF), wblk),
                pl.BlockSpec((None, None, ck, D_FF), wblk),
                pl.BlockSpec((None, None, cf, D_MODEL), wblk),
            ],
            out_specs=pl.BlockSpec((tm, D_MODEL), tile),
            scratch_shapes=[pltpu.VMEM((2, D_MODEL, D_FF), BF16),
                            pltpu.VMEM((2, D_MODEL, D_FF), BF16),
                            pltpu.VMEM((2, D_FF, D_MODEL), BF16)],
        ),
        compiler_params=_params(56),
        name="expert_mlp",
    )(*plan, xs, w_gate, w_up, w_down)


def _dispatch_plan(idx):
    e = idx.reshape(-1)
    onehot = jnp.arange(N_EXPERTS, dtype=jnp.int32)[:, None] == e[None, :]
    nb = N_PAIRS // PLAN_BLOCK
    upper = jnp.triu(jnp.ones((PLAN_BLOCK, PLAN_BLOCK), F32))
    within = jnp.einsum('ebj,ji->ebi', onehot.astype(F32).reshape(N_EXPERTS, nb, PLAN_BLOCK), upper,
                        preferred_element_type=F32)
    totals = within[:, :, -1]
    before = jnp.cumsum(totals, axis=1) - totals
    csum = (within + before[:, :, None]).reshape(N_EXPERTS, N_PAIRS)
    counts = (before[:, -1] + totals[:, -1]).astype(jnp.int32)
    nt = (counts + EXPERT_TM - 1) // EXPERT_TM
    tile_end = jnp.cumsum(nt)
    tile_start = tile_end - nt
    first_row = (tile_start * EXPERT_TM).astype(F32)[:, None]
    pos = jnp.sum(jnp.where(onehot, csum - 1.0 + first_row, 0.0), axis=0)

    nw = EXPERT_WCHUNKS
    length = jnp.maximum(nt, nw)
    phase_end = nw + jnp.cumsum(length)
    phase_start = phase_end - length
    s = jnp.arange(EXPERT_STEPS, dtype=jnp.int32)
    started = s[:, None] >= phase_start[None, :]
    ex = jnp.sum(started.astype(jnp.int32), axis=1) - 1
    running = started & (s[:, None] < phase_end[None, :])

    def of_phase(v):
        return jnp.sum(jnp.where(running, v[None, :], 0), axis=1)

    k = s - of_phase(phase_start)
    nt_s = of_phase(nt)
    ts_s = of_phase(tile_start)
    live = s < phase_end[-1:]
    last = N_EXPERTS - 1
    w_do = (live & (k < nw) & (ex < last)).astype(jnp.int32)
    w_e = jnp.minimum(ex + 1, last)
    w_c = jnp.where(ex < last, jnp.minimum(k, nw - 1), nw - 1)
    w_slot = (ex + 1) % 2
    t_do = (live & (k < nt_s)).astype(jnp.int32)
    t_idx = jnp.maximum(ts_s + jnp.minimum(k + 1, nt_s) - 1, 0)
    z = tile_end[-1:] + (s - phase_end[-1:])
    t_do = jnp.where(jnp.logical_not(live) & (z < EXPERT_TILES), 2, t_do)
    t_idx = jnp.where(live, t_idx, jnp.minimum(z, EXPERT_TILES - 1))
    t_slot = jnp.maximum(ex, 0) % 2
    plan = tuple(a.astype(jnp.int32) for a in (w_e, w_c, w_do, w_slot, t_idx, t_do, t_slot))
    return pos.astype(jnp.int32), plan


def _bias_tables(nat_rpb):
    w = GRID_W
    c = jnp.arange(w)[:, None]
    kc = jnp.arange(w)[None, :]
    qstart = jnp.clip(c - WIN_W // 2, 0, w - WIN_W)
    valid = (kc >= qstart) & (kc < qstart + WIN_W)
    n_off = nat_rpb.shape[-1]
    select = ((kc - c + WIN_W - 1)[None] == jnp.arange(n_off)[:, None, None]).astype(F32)
    tab = jnp.einsum('lhdj,jck->lhdck', nat_rpb, select, precision=lax.Precision.HIGHEST)
    tab = jnp.where(valid, tab * LOG2E, MASK_VALUE)
    tab = jnp.pad(tab, [(0, 0), (0, 0), (1, 1), (0, 0), (0, 0)], constant_values=MASK_VALUE)
    return jnp.concatenate([tab[:, :, :-1], tab[:, :, 1:]], axis=-1)


def _rope_tables():
    t = jnp.arange(DEC_SEQ)
    row = (t // GRID_W).astype(F32)
    col = (t % GRID_W).astype(F32)
    nf = DIFF_QK_DIM // 4
    inv = ROPE_BASE ** (-jnp.arange(nf, dtype=F32) / nf)
    ar = row[:, None] * inv[None, :]
    ac = col[:, None] * inv[None, :]
    cos = jnp.concatenate([jnp.cos(ar), jnp.cos(ar), jnp.cos(ac), jnp.cos(ac)], axis=-1)
    sin = jnp.concatenate([-jnp.sin(ar), jnp.sin(ar), -jnp.sin(ac), jnp.sin(ac)], axis=-1)
    return jnp.tile(cos, (1, 2)), jnp.tile(sin, (1, 2))


def kernel(x_prompt, x_sample, cache_nat_k, cache_nat_v, cache_diff_k, cache_diff_v, c, c_ctx,
           w_ada, b_ada, norm1, norm2, norm_final, w_in, w_out, nat_rpb, diff_lambda, diff_subln,
           w_router, router_bias, w_gate, w_up, w_down):
    x = (x_prompt.reshape(N_PROMPT, D_MODEL), x_sample.reshape(DEC_BATCH * DEC_SEQ, D_MODEL))
    cond =jnp.concatenate([c_ctx[None, :], c, jnp.zeros((8 - 1 - DEC_BATCH, D_MODEL), F32)], axis=0)
    mods = _modulation(cond, w_ada, b_ada)
    tabs = _bias_tables(nat_rpb)
    cos, sin = _rope_tables()
    subln = diff_subln.reshape(DEPTH, 1, HEAD_DIM)
    w_router_t = w_router.T
    rbias = router_bias.reshape(N_EXPERTS, 1)
    lat_caches = (cache_nat_k, cache_nat_v, cache_diff_k, cache_diff_v)

    new_caches = tuple(jnp.zeros((BATCH, DEPTH, N_HEADS, SEQ, HEAD_DIM), F32) for _ in range(4))
    xs_buf = jnp.zeros((EXPERT_CAP, D_MODEL), F32)
    ys = pos = cw = None
    for layer in range(DEPTH):
        gain1 = norm1[layer].reshape(1, D_MODEL)
        if layer == 0:
            h = _pre_first(*x, mods, layer, gain1)
        else:
            x, h = _pre_combine(x, ys, pos, cw, mods, layer - 1, gain1, final=False)
        qkv = _in_proj(h, w_in, layer)
        pn, pd, new_caches = _attn_prompt(qkv, diff_lambda, subln, new_caches, layer)
        sn, sd = _attn_latent(qkv, lat_caches, tabs[layer], cos, sin, diff_lambda, subln, layer)
        x, h2, idx, cwt = _out_proj((pn, pd, sn, sd), w_out, x, mods, layer,
                                    norm2[layer].reshape(1, D_MODEL), w_router_t, rbias)
        pos, plan = _dispatch_plan(idx)
        cw = cwt.T
        xs_buf = _dispatch(pos, h2, xs_buf)
        ys = _experts(plan, xs_buf, w_gate, w_up, w_down, layer)
    yp, ysm = _pre_combine(x, ys, pos, cw, mods, DEPTH - 1, norm_final.reshape(1, D_MODEL), final=True)
    return (yp.reshape(BATCH, SEQ, D_MODEL), ysm.reshape(DEC_BATCH, DEC_SEQ, D_MODEL)) + tuple(new_caches)
```
